```python
import math
import jax, jax.numpy as jnp
from jax import lax
import numpy as np

D_MODEL = 2048
BATCH = 1
SEQ = 16384
DEPTH = 4
DEC_BATCH = 8
DEC_SEQ = 4096
PAST_LEN = 128

MLA_HEADS = 16
MLA_Q_RANK = 512
MLA_KV_RANK = 512
MLA_NOPE_DIM = 128
MLA_ROPE_DIM = 64
MLA_V_DIM = 128
MLA_WIDTH = MLA_HEADS * MLA_V_DIM
MLA_QK_DIM = MLA_NOPE_DIM + MLA_ROPE_DIM
ROPE_THETA = 10000.0
Q_BLOCK = 128
ML_HEADS = 8
ML_HEAD_DIM = 128
ML_WIDTH = ML_HEADS * ML_HEAD_DIM
ML_CHUNK = 128
ML_FGATE_BIAS_LO = 3.0
ML_FGATE_BIAS_HI = 6.0
N_IN = MLA_Q_RANK + MLA_KV_RANK + MLA_ROPE_DIM + MLA_WIDTH + 5 * ML_WIDTH + 4 * ML_HEADS + 2 * D_MODEL
DEEPNORM_ALPHA = (2 * DEPTH) ** 0.25
DEEPNORM_BETA = (8 * DEPTH) ** -0.25
LN_EPS = 1e-5

kernel_name = 'hybrid_mla_mlstm_deepnorm_encoder'


def layer_norm(x, w=None, b=None):
    xf = x.astype(jnp.float32)
    mu = jnp.mean(xf, axis=-1, keepdims=True)
    var = jnp.mean(jnp.square(xf - mu), axis=-1, keepdims=True)
    y = (xf - mu) * lax.rsqrt(var + LN_EPS)
    if w is not None:
        y = y * w.astype(jnp.float32) + b.astype(jnp.float32)
    return y.astype(x.dtype)


def rms_norm(x, w):
    xf = x.astype(jnp.float32)
    y = xf * lax.rsqrt(jnp.mean(jnp.square(xf), axis=-1, keepdims=True) + 1e-6)
    return (y * w.astype(jnp.float32)).astype(x.dtype)


def rope_tables(seq_len):
    inv_freq = ROPE_THETA ** (-jnp.arange(0, MLA_ROPE_DIM, 2, dtype=jnp.float32) / MLA_ROPE_DIM)
    ang = jnp.arange(seq_len, dtype=jnp.float32)[:, None] * inv_freq[None, :]
    return jnp.cos(ang), jnp.sin(ang)


def apply_rope(x, cos, sin):
    xf = x.astype(jnp.float32)
    x1, x2 = jnp.split(xf, 2, axis=-1)
    return jnp.concatenate([x1 * cos - x2 * sin, x2 * cos + x1 * sin], axis=-1).astype(x.dtype)


def split_points():
    sizes = [MLA_Q_RANK, MLA_KV_RANK, MLA_ROPE_DIM, MLA_WIDTH,
             ML_WIDTH, ML_WIDTH, ML_WIDTH, ML_WIDTH, ML_WIDTH, 4 * ML_HEADS]
    return [int(v) for v in np.cumsum(sizes)]


def mla_attention(q_nope, q_rope, k_nope, k_rope, v):
    B, S, H, _ = q_nope.shape
    nq = S // Q_BLOCK
    qn = q_nope.reshape(B, nq, Q_BLOCK, H, MLA_NOPE_DIM).transpose(1, 0, 2, 3, 4)
    qr = q_rope.reshape(B, nq, Q_BLOCK, H, MLA_ROPE_DIM).transpose(1, 0, 2, 3, 4)
    scale = MLA_QK_DIM ** -0.5

    def block(args):
        qnb, qrb = args
        s = jnp.einsum('bqhd,bkhd->bhqk', qnb, k_nope) + jnp.einsum('bqhr,bkr->bhqk', qrb, k_rope)
        p = jax.nn.softmax(s.astype(jnp.float32) * scale, axis=-1).astype(v.dtype)
        return jnp.einsum('bhqk,bkhd->bqhd', p, v)

    out = lax.map(block, (qn, qr))
    return out.transpose(1, 0, 2, 3, 4).reshape(B, S, H * MLA_V_DIM)


def mlstm_chunkwise(q, k, v, i_pre, f_pre):
    B, S, H, DH = q.shape
    nc = S // ML_CHUNK
    f32 = jnp.float32

    def chunks(t):
        return t.astype(f32).reshape(B, nc, ML_CHUNK, H, -1).transpose(1, 0, 3, 2, 4)

    def gchunks(t):
        return t.astype(f32).reshape(B, nc, ML_CHUNK, H).transpose(1, 0, 3, 2)

    qc = chunks(q)
    kc = chunks(k) * (DH ** -0.5)
    vc = chunks(v)
    ic = gchunks(i_pre)
    lfc = jax.nn.log_sigmoid(gchunks(f_pre))
    tri = jnp.tril(jnp.ones((ML_CHUNK, ML_CHUNK), dtype=bool))

    def step(carry, xs):
        C, n, m = carry
        qb, kb, vb, ib, lfb = xs
        b = jnp.cumsum(lfb, axis=-1)
        log_d = jnp.where(tri, b[..., :, None] - b[..., None, :] + ib[..., None, :], -jnp.inf)
        inter = b + m[..., None]
        m_t = jnp.maximum(inter, jnp.max(log_d, axis=-1))
        d_mat = jnp.exp(log_d - m_t[..., None])
        s_qk = jnp.einsum('bhtd,bhsd->bhts', qb, kb) * d_mat
        w_inter = jnp.exp(inter - m_t)
        num = jnp.einsum('bhts,bhse->bhte', s_qk, vb) + w_inter[..., None] * jnp.einsum('bhed,bhtd->bhte', C, qb)
        den = jnp.sum(s_qk, axis=-1) + w_inter * jnp.einsum('bhd,bhtd->bht', n, qb)
        h = num / jnp.maximum(jnp.abs(den), jnp.exp(-m_t))[..., None]
        b_last = b[..., -1]
        log_w = b_last[..., None] - b + ib
        m_new = jnp.maximum(b_last + m, jnp.max(log_w, axis=-1))
        w = jnp.exp(log_w - m_new[..., None])
        decay = jnp.exp(b_last + m - m_new)
        C = decay[..., None, None] * C + jnp.einsum('bhse,bhsd->bhed', w[..., None] * vb, kb)
        n = decay[..., None] * n + jnp.einsum('bhs,bhsd->bhd', w, kb)
        return (C, n, m_new), h

    init = (jnp.zeros((B, H, DH, DH), f32), jnp.zeros((B, H, DH), f32), jnp.zeros((B, H), f32))
    _, h = lax.scan(step, init, (qc, kc, vc, ic, lfc))
    return h.transpose(1, 0, 3, 2, 4).reshape(B, S, H, DH).astype(q.dtype)


def encoder_layer(x, c, w_ada, b_ada, w_in, b_if, q_norm_w, kv_norm_w, w_q_b, w_kv_b,
                  ml_norm_w, w_proj_a, w_proj_b, w_out, ln_w, ln_b):
    B, S, _ = x.shape
    mod = jax.nn.silu(c) @ w_ada + b_ada
    shift, scale, gate = jnp.split(mod, 3, axis=-1)
    u = layer_norm(x) * (1 + scale[:, None, :]) + shift[:, None, :]
    proj = u @ w_in
    (q_lat, kv_lat, k_rope, z_b, ml_q, ml_k, ml_v, ml_o, z_a, ml_g, merge_g) = jnp.split(proj, split_points(), axis=-1)

    cos, sin = rope_tables(S)
    q = (rms_norm(q_lat, q_norm_w) @ w_q_b).reshape(B, S, MLA_HEADS, MLA_QK_DIM)
    q_nope = q[..., :MLA_NOPE_DIM]
    q_rope = apply_rope(q[..., MLA_NOPE_DIM:], cos[None, :, None, :], sin[None, :, None, :])
    kv = (rms_norm(kv_lat, kv_norm_w) @ w_kv_b).reshape(B, S, MLA_HEADS, MLA_NOPE_DIM + MLA_V_DIM)
    k_nope = kv[..., :MLA_NOPE_DIM]
    v = kv[..., MLA_NOPE_DIM:]
    k_rope = apply_rope(k_rope, cos[None], sin[None])
    attn = mla_attention(q_nope, q_rope, k_nope, k_rope, v)
    y_b = (attn * jax.nn.silu(z_b)) @ w_proj_b

    def heads(t):
        return t.reshape(B, S, ML_HEADS, ML_HEAD_DIM)

    def rev(t):
        return jnp.flip(t, axis=1)

    g = ml_g.reshape(B, S, 4, ML_HEADS) + b_if
    qh, kh, vh = heads(ml_q), heads(ml_k), heads(ml_v)
    h_fwd = mlstm_chunkwise(qh, kh, vh, g[:, :, 0], g[:, :, 1])
    h_bwd = rev(mlstm_chunkwise(rev(qh), rev(kh), rev(vh), rev(g[:, :, 2]), rev(g[:, :, 3])))
    h = layer_norm(h_fwd + h_bwd) * ml_norm_w.reshape(ML_HEADS, ML_HEAD_DIM)
    h = jax.nn.sigmoid(heads(ml_o)) * h
    y_a = (h.reshape(B, S, ML_WIDTH) * jax.nn.silu(z_a)) @ w_proj_a

    g_a, g_b = jnp.split(merge_g, 2, axis=-1)
    merged = jax.nn.sigmoid(g_a) * y_a + jax.nn.sigmoid(g_b) * y_b
    out = merged @ w_out
    return layer_norm(DEEPNORM_ALPHA * x + gate[:, None, :] * out, ln_w, ln_b)


def trunk(x, c, w_ada, b_ada, w_in, b_if, q_norm_w, kv_norm_w, w_q_b, w_kv_b,
          ml_norm_w, w_proj_a, w_proj_b, w_out, ln_w, ln_b):
    for l in range(DEPTH):
        x = encoder_layer(x, c, w_ada[l], b_ada[l], w_in[l], b_if[l], q_norm_w[l], kv_norm_w[l],
                          w_q_b[l], w_kv_b[l], ml_norm_w[l], w_proj_a[l], w_proj_b[l], w_out[l],
                          ln_w[l], ln_b[l])
    return x


def setup_inputs(seed: int = 0) -> dict:
    key = jax.random.key(seed)
    ks = jax.random.split(key, 24)
    f32 = jnp.float32

    def nrm(k, shape, s):
        return jax.random.normal(k, shape, f32) * s

    zeros_h = jnp.zeros((ML_HEADS,), f32)
    f_base = jnp.linspace(ML_FGATE_BIAS_LO, ML_FGATE_BIAS_HI, ML_HEADS, dtype=f32)
    b_if = jnp.stack([zeros_h, f_base, zeros_h, f_base])[None] + nrm(ks[7], (DEPTH, 4, ML_HEADS), 0.1)
    return {
        'x_prompt': nrm(ks[0], (BATCH, SEQ, D_MODEL), 1.0),
        'x_sample': nrm(ks[1], (DEC_BATCH, DEC_SEQ, D_MODEL), 1.0),
        'c_prompt': nrm(ks[2], (BATCH, D_MODEL), 1.0),
        'c_sample': nrm(ks[3], (DEC_BATCH, D_MODEL), 1.0),
        'w_ada': nrm(ks[4], (DEPTH, D_MODEL, 3 * D_MODEL), 0.5 * D_MODEL ** -0.5),
        'b_ada': nrm(ks[5], (DEPTH, 3 * D_MODEL), 0.02),
        'w_in': nrm(ks[6], (DEPTH, D_MODEL, N_IN), D_MODEL ** -0.5),
        'b_if': b_if,
        'q_norm_w': 1.0 + nrm(ks[8], (DEPTH, MLA_Q_RANK), 0.02),
        'kv_norm_w': 1.0 + nrm(ks[9], (DEPTH, MLA_KV_RANK), 0.02),
        'w_q_b': nrm(ks[10], (DEPTH, MLA_Q_RANK, MLA_HEADS * MLA_QK_DIM), MLA_Q_RANK ** -0.5),
        'w_kv_b': nrm(ks[11], (DEPTH, MLA_KV_RANK, MLA_HEADS * (MLA_NOPE_DIM + MLA_V_DIM)), MLA_KV_RANK ** -0.5),
        'ml_norm_w': 1.0 + nrm(ks[12], (DEPTH, ML_WIDTH), 0.02),
        'w_proj_a': nrm(ks[13], (DEPTH, ML_WIDTH, D_MODEL), DEEPNORM_BETA * ML_WIDTH ** -0.5),
        'w_proj_b': nrm(ks[14], (DEPTH, MLA_WIDTH, D_MODEL), DEEPNORM_BETA * MLA_WIDTH ** -0.5),
        'w_out': nrm(ks[15], (DEPTH, D_MODEL, D_MODEL), DEEPNORM_BETA * D_MODEL ** -0.5),
        'ln_w': 1.0 + nrm(ks[16], (DEPTH, D_MODEL), 0.02),
        'ln_b': nrm(ks[17], (DEPTH, D_MODEL), 0.02),
    }


def reference(x_prompt, x_sample, c_prompt, c_sample, w_ada, b_ada, w_in, b_if, q_norm_w, kv_norm_w,
              w_q_b, w_kv_b, ml_norm_w, w_proj_a, w_proj_b, w_out, ln_w, ln_b):
    y_prompt = trunk(x_prompt, c_prompt, w_ada, b_ada, w_in, b_if, q_norm_w, kv_norm_w, w_q_b, w_kv_b,
                     ml_norm_w, w_proj_a, w_proj_b, w_out, ln_w, ln_b)
    y_sample = trunk(x_sample, c_sample, w_ada, b_ada, w_in, b_if, q_norm_w, kv_norm_w, w_q_b, w_kv_b,
                     ml_norm_w, w_proj_a, w_proj_b, w_out, ln_w, ln_b)
    return (y_prompt, y_sample)
```

```python
import functools

import jax
import jax.numpy as jnp
from jax import lax
from jax.experimental import pallas as pl
from jax.experimental.pallas import tpu as pltpu

F32 = jnp.float32
BF16 = jnp.bfloat16

MLA_HEADS = 16
MLA_NOPE_DIM = 128
MLA_ROPE_DIM = 64
MLA_V_DIM = 128
MLA_QK_DIM = MLA_NOPE_DIM + MLA_ROPE_DIM
ROPE_THETA = 10000.0
ML_HEADS = 8
ML_HEAD_DIM = 128
ML_CHUNK = 128
LN_EPS = 1e-5
RMS_EPS = 1e-6

LANES = 128
HEAD_LANES = 2 * LANES
VMEM_LIMIT_BYTES = 56 * 1024 * 1024

MOD_TN = 1024
INPROJ_TM = 1024
INPROJ_TN = 1024
QKV_TM = 256
ATTN_TQ = 512
ATTN_TK = 512
TAIL_TM = 256


def _cparams(n_axes):
    return pltpu.CompilerParams(
        dimension_semantics=("arbitrary",) * n_axes,
        vmem_limit_bytes=VMEM_LIMIT_BYTES,
    )


def _sigmoid(x):
    return 1.0 / (1.0 + jnp.exp(-x))


def _silu(x):
    return x * _sigmoid(x)


def _log_sigmoid(x):
    return jnp.minimum(x, 0.0) - jnp.log(1.0 + jnp.exp(-jnp.abs(x)))


def _layer_norm_rows(x):
    mu = jnp.mean(x, axis=-1, keepdims=True)
    xc = x - mu
    var = jnp.mean(xc * xc, axis=-1, keepdims=True)
    return xc * lax.rsqrt(var + LN_EPS)


def _mod_kernel(c_ref, w_ref, b_ref, o_ref):
    c = c_ref[...]
    a = _silu(c).astype(BF16)
    o_ref[0] = jnp.dot(a, w_ref[0].astype(BF16), preferred_element_type=F32) + b_ref[0]


def _adaln_mod(c_pad, w_ada, b_ada):
    depth, d, n = w_ada.shape
    rows = c_pad.shape[0]
    tn = min(MOD_TN, n)
    return pl.pallas_call(
        _mod_kernel,
        grid=(depth, n // tn),
        in_specs=[
            pl.BlockSpec((rows, d), lambda l, j: (0, 0)),
            pl.BlockSpec((1, d, tn), lambda l, j: (l, 0, j)),
            pl.BlockSpec((1, 1, tn), lambda l, j: (l, 0, j)),
        ],
        out_specs=pl.BlockSpec((1, rows, tn), lambda l, j: (l, 0, j)),
        out_shape=jax.ShapeDtypeStruct((depth, rows, n), F32),
        compiler_params=_cparams(2),
        name="adaln_mod",
    )(c_pad, w_ada, b_ada.reshape(depth, 1, n))


def _inproj_kernel(x_ref, sh_ref, sc_ref, wm_ref, ws_ref, pm_ref, ps_ref, u_ref):
    @pl.when(pl.program_id(1) == 0)
    def _():
        u = _layer_norm_rows(x_ref[...]) * (1.0 + sc_ref[0]) + sh_ref[0]
        ub = u.astype(BF16)
        u_ref[...] = ub
        ps_ref[...] = jnp.dot(ub, ws_ref[...], preferred_element_type=F32)

    pm_ref[...] = jnp.dot(u_ref[...], wm_ref[...], preferred_element_type=F32).astype(BF16)


def _inproj(x, shift, scale, w_main, w_small, seg):
    t, d = x.shape
    n = w_main.shape[1]
    ns = w_small.shape[1]
    tm = min(INPROJ_TM, seg)
    tn = min(INPROJ_TN, n)
    seg_of = lambda i, j: ((i * tm) // seg, 0, 0)
    return pl.pallas_call(
        _inproj_kernel,
        grid=(t // tm, n // tn),
        in_specs=[
            pl.BlockSpec((tm, d), lambda i, j: (i, 0)),
            pl.BlockSpec((1, 1, d), seg_of),
            pl.BlockSpec((1, 1, d), seg_of),
            pl.BlockSpec((d, tn), lambda i, j: (0, j)),
            pl.BlockSpec((d, ns), lambda i, j: (0, 0)),
        ],
        out_specs=[
            pl.BlockSpec((tm, tn), lambda i, j: (i, j)),
            pl.BlockSpec((tm, ns), lambda i, j: (i, 0)),
        ],
        out_shape=[
            jax.ShapeDtypeStruct((t, n), BF16),
            jax.ShapeDtypeStruct((t, ns), F32),
        ],
        scratch_shapes=[pltpu.VMEM((tm, d), BF16)],
        compiler_params=_cparams(2),
        name="inproj",
    )(x, shift, scale, w_main, w_small)


def _rope_lanes(x, cs, sn):
    return x * cs + pltpu.roll(x, LANES // 2, axis=1) * sn


def _qkv_kernel(ql_ref, kvl_ref, ps_ref, cs_ref, sn_ref, qnw_ref, kvnw_ref, wq_ref, wk_ref, wv_ref,
                q_out, k_out, v_out, *, heads, qk_scale):
    def rms(v, w):
        return (v * lax.rsqrt(jnp.mean(v * v, axis=-1, keepdims=True) + RMS_EPS) * w).astype(BF16)

    qn = rms(ql_ref[...].astype(F32), qnw_ref[...])
    kvn = rms(kvl_ref[...].astype(F32), kvnw_ref[...])
    cs = cs_ref[...]
    sn = sn_ref[...]
    kr = _rope_lanes(ps_ref[...], cs, sn).astype(BF16)
    tm = kr.shape[0]
    lane = lax.broadcasted_iota(jnp.int32, (tm, LANES), 1)
    ones_col = jnp.where(lane == 0, 1.0, 0.0).astype(BF16)
    for h in range(heads):
        qa = jnp.dot(qn, wq_ref[:, h * HEAD_LANES:(h + 1) * HEAD_LANES], preferred_element_type=F32)
        q_out[h, :, :LANES] = (qa[:, :LANES] * qk_scale).astype(BF16)
        q_out[h, :, LANES:] = (_rope_lanes(qa[:, LANES:], cs, sn) * qk_scale).astype(BF16)
        ka = jnp.dot(kvn, wk_ref[:, h * LANES:(h + 1) * LANES], preferred_element_type=F32)
        k_out[h, :, :LANES] = ka.astype(BF16)
        k_out[h, :, LANES:] = kr
        va = jnp.dot(kvn, wv_ref[:, h * LANES:(h + 1) * LANES], preferred_element_type=F32)
        v_out[h, :, :LANES] = va.astype(BF16)
        v_out[h, :, LANES:] = ones_col


def _qkv(pm, ps, cs_tab, sn_tab, qnw, kvnw, wq, wk, wv, *, t_prompt, dec_seq, q_rank, kv_rank):
    t = pm.shape[0]
    heads = MLA_HEADS
    tm = min(QKV_TM, dec_seq)
    assert q_rank == kv_rank and q_rank % LANES == 0
    np_blocks = t_prompt // tm
    dec_blocks = dec_seq // tm

    def pos_block(i):
        return (jnp.where(i < np_blocks, i, (i - np_blocks) % dec_blocks), 0)

    out_sds = jax.ShapeDtypeStruct((heads, t, HEAD_LANES), BF16)
    out_spec = pl.BlockSpec((heads, tm, HEAD_LANES), lambda i: (0, i, 0))
    kern = functools.partial(_qkv_kernel, heads=heads, qk_scale=MLA_QK_DIM ** -0.5)
    return pl.pallas_call(
        kern,
        grid=(t // tm,),
        in_specs=[
            pl.BlockSpec((tm, q_rank), lambda i: (i, 0)),
            pl.BlockSpec((tm, kv_rank), lambda i: (i, 1)),
            pl.BlockSpec((tm, LANES), lambda i: (i, 0)),
            pl.BlockSpec((tm, LANES), pos_block),
            pl.BlockSpec((tm, LANES), pos_block),
            pl.BlockSpec((1, q_rank), lambda i: (0, 0)),
            pl.BlockSpec((1, kv_rank), lambda i: (0, 0)),
            pl.BlockSpec(wq.shape, lambda i: (0, 0)),
            pl.BlockSpec(wk.shape, lambda i: (0, 0)),
            pl.BlockSpec(wv.shape, lambda i: (0, 0)),
        ],
        out_specs=[out_spec, out_spec, out_spec],
        out_shape=[out_sds, out_sds, out_sds],
        compiler_params=_cparams(1),
        name="qkv_up",
    )(pm, pm, ps, cs_tab, sn_tab, qnw, kvnw, wq, wk, wv)


def _attn_kernel(q_ref, k_ref, v_ref, zb_ref, o_ref, *, tk, nk):
    q = q_ref[0]
    tq = q.shape[0]

    def body(j, carry):
        m, acc = carry
        start = pl.multiple_of(j * tk, tk)
        ks = k_ref[0, pl.ds(start, tk), :]
        vs = v_ref[0, pl.ds(start, tk), :]
        s = lax.dot_general(q, ks, (((1,), (1,)), ((), ())), preferred_element_type=F32)
        m_new = jnp.maximum(m, jnp.max(s, axis=1, keepdims=True))
        p = jnp.exp(s - m_new).astype(BF16)
        acc = acc * jnp.exp(m - m_new) + jnp.dot(p, vs, preferred_element_type=F32)
        return m_new, acc

    m0 = jnp.full((tq, 1), -jnp.inf, F32)
    acc0 = jnp.zeros((tq, HEAD_LANES), F32)
    _, acc = lax.fori_loop(0, nk, body, (m0, acc0))
    o = acc[:, :LANES] / acc[:, LANES:LANES + 1]
    o_ref[...] = (o * _silu(zb_ref[...].astype(F32))).astype(BF16)


def _attention(qf, kf, vf, pm, *, row0, n_seq, seq, zb_col0):
    heads = qf.shape[0]
    tq = min(ATTN_TQ, seq)
    tk = min(ATTN_TK, seq)
    nq = seq // tq
    qb0 = row0 // tq
    sb0 = row0 // seq
    zc0 = zb_col0 // LANES
    kern = functools.partial(_attn_kernel, tk=tk, nk=seq // tk)
    return pl.pallas_call(
        kern,
        grid=(n_seq, heads, nq),
        in_specs=[
            pl.BlockSpec((1, tq, HEAD_LANES), lambda n, h, i: (h, qb0 + n * nq + i, 0)),
            pl.BlockSpec((1, seq, HEAD_LANES), lambda n, h, i: (h, sb0 + n, 0)),
            pl.BlockSpec((1, seq, HEAD_LANES), lambda n, h, i: (h, sb0 + n, 0)),
            pl.BlockSpec((tq, LANES), lambda n, h, i: (qb0 + n * nq + i, zc0 + h)),
        ],
        out_specs=pl.BlockSpec((tq, LANES), lambda n, h, i: (n * nq + i, h)),
        out_shape=jax.ShapeDtypeStruct((n_seq * seq, heads * LANES), BF16),
        compiler_params=_cparams(3),
        name="mla_attention",
    )(qf, kf, vf, pm)


def _mlstm_kernel(qf_ref, kf_ref, vf_ref, gf_ref, qb_ref, kb_ref, vb_ref, gb_ref, bif_ref,
                  hf_ref, hb_ref, st_ref, m_ref, *, heads, k_scale):
    L = ML_CHUNK
    dh = ML_HEAD_DIM

    @pl.when(pl.program_id(1) == 0)
    def _():
        st_ref[...] = jnp.zeros_like(st_ref)
        m_ref[...] = jnp.zeros_like(m_ref)

    row = lax.broadcasted_iota(jnp.int32, (L, L), 0)
    col = lax.broadcasted_iota(jnp.int32, (L, L), 1)
    lane = lax.broadcasted_iota(jnp.int32, (L, LANES), 1)
    ones_col = jnp.where(lane == 0, 1.0, 0.0).astype(BF16)

    dirs = ((qf_ref, kf_ref, vf_ref, gf_ref, hf_ref, col <= row, L - 1),
            (qb_ref, kb_ref, vb_ref, gb_ref, hb_ref, col >= row, 0))
    for d, (q_ref, k_ref, v_ref, g_ref, h_ref, mask, last) in enumerate(dirs):
        g = g_ref[...] + bif_ref[...]
        lf = _log_sigmoid(g)
        b_all = jnp.dot(mask.astype(F32), lf, preferred_element_type=F32,
                        precision=lax.Precision.HIGHEST)
        g_t = g.T
        b_t = b_all.T
        for h in range(heads):
            ci = 2 * heads * d + h
            cf = ci + heads
            idx = d * heads + h
            bcol = b_all[:, cf:cf + 1]
            brow = b_t[cf:cf + 1, :]
            icol = g[:, ci:ci + 1]
            irow = g_t[ci:ci + 1, :]
            m_prev = m_ref[idx][:, 0:1]
            logd = jnp.where(mask, bcol - brow + irow, -jnp.inf)
            inter = bcol + m_prev
            m_t = jnp.maximum(inter, jnp.max(logd, axis=1, keepdims=True))
            dmat = jnp.exp(logd - m_t)
            w_inter = jnp.exp(inter - m_t)
            q = q_ref[:, h * dh:(h + 1) * dh]
            k = k_ref[:, h * dh:(h + 1) * dh]
            v_aug = jnp.concatenate([v_ref[:, h * dh:(h + 1) * dh], ones_col], axis=1)
            s = lax.dot_general(q, k, (((1,), (1,)), ((), ())), preferred_element_type=F32)
            s = (s * k_scale * dmat).astype(BF16)
            st_prev = st_ref[idx]
            nd = (jnp.dot(s, v_aug, preferred_element_type=F32)
                  + w_inter * jnp.dot(q, st_prev.astype(BF16), preferred_element_type=F32))
            den = nd[:, dh:dh + 1]
            hout = nd[:, :dh] / jnp.maximum(jnp.abs(den), jnp.exp(-m_t))
            h_ref[:, h * dh:(h + 1) * dh] = hout.astype(h_ref.dtype)
            m_new = m_t[last:last + 1, :]
            b_last = bcol[last:last + 1, :]
            wcol = jnp.exp(b_last - bcol + icol - m_new) * k_scale
            decay = jnp.exp(b_last + m_prev - m_new)
            wv = (v_aug.astype(F32) * wcol).astype(BF16)
            upd = lax.dot_general(k, wv, (((0,), (0,)), ((), ())), preferred_element_type=F32)
            st_ref[idx] = decay * st_prev + upd
            m_ref[idx] = jnp.broadcast_to(m_new, (1, LANES))


def _mlstm(pm, ps, bif, *, row0, n_seq, seq, q_col0):
    heads = ML_HEADS
    width = heads * ML_HEAD_DIM
    L = ML_CHUNK
    nc = seq // L
    rb0 = row0 // L
    qc = q_col0 // width
    fwd = lambda n, c: rb0 + n * nc + c
    bwd = lambda n, c: rb0 + n * nc + (nc - 1 - c)

    def col_spec(chunk_of, cb):
        return pl.BlockSpec((L, width), lambda n, c: (chunk_of(n, c), cb))

    gate_f = pl.BlockSpec((L, LANES), lambda n, c: (fwd(n, c), 1))
    gate_b = pl.BlockSpec((L, LANES), lambda n, c: (bwd(n, c), 1))
    out_sds = jax.ShapeDtypeStruct((n_seq * seq, width), BF16)
    kern = functools.partial(_mlstm_kernel, heads=heads, k_scale=ML_HEAD_DIM ** -0.5)
    return pl.pallas_call(
        kern,
        grid=(n_seq, nc),
        in_specs=[
            col_spec(fwd, qc), col_spec(fwd, qc + 1), col_spec(fwd, qc + 2), gate_f,
            col_spec(bwd, qc), col_spec(bwd, qc + 1), col_spec(bwd, qc + 2), gate_b,
            pl.BlockSpec((1, LANES), lambda n, c: (0, 0)),
        ],
        out_specs=[
            pl.BlockSpec((L, width), lambda n, c: (n * nc + c, 0)),
            pl.BlockSpec((L, width), lambda n, c: (n * nc + (nc - 1 - c), 0)),
        ],
        out_shape=[out_sds, out_sds],
        scratch_shapes=[
            pltpu.VMEM((2 * heads, ML_HEAD_DIM, HEAD_LANES), F32),
            pltpu.VMEM((2 * heads, 1, LANES), F32),
        ],
        compiler_params=_cparams(2),
        name="mlstm_bidir",
    )(pm, pm, pm, ps, pm, pm, pm, ps, bif)


def _tail_kernel(x_ref, attn_ref, hf_ref, hb_ref, o_ref, za_ref, ga_ref, gb_ref, gate_ref,
                 mlw_ref, wpa_ref, wpb_ref, wo_ref, lnw_ref, lnb_ref, y_ref, *, heads, alpha):
    dh = ML_HEAD_DIM
    hs = hf_ref[...].astype(F32) + hb_ref[...].astype(F32)
    hn = jnp.concatenate(
        [_layer_norm_rows(hs[:, h * dh:(h + 1) * dh]) for h in range(heads)], axis=1)
    a_in = hn * mlw_ref[...] * _sigmoid(o_ref[...].astype(F32)) * _silu(za_ref[...].astype(F32))
    y_a = jnp.dot(a_in.astype(BF16), wpa_ref[...], preferred_element_type=F32)
    y_b = jnp.dot(attn_ref[...], wpb_ref[...], preferred_element_type=F32)
    merged = (_sigmoid(ga_ref[...].astype(F32)) * y_a + _sigmoid(gb_ref[...].astype(F32)) * y_b)
    out = jnp.dot(merged.astype(BF16), wo_ref[...], preferred_element_type=F32)
    r = alpha * x_ref[...] + gate_ref[0] * out
    y_ref[...] = _layer_norm_rows(r) * lnw_ref[...] + lnb_ref[...]


def _tail(x, attn, hf, hb, pm, gate, mlw, wpa, wpb, wo, lnw, lnb, *, seg, alpha, o_col0, za_col0, g_col0):
    t, d = x.shape
    wa = hf.shape[1]
    tm = min(TAIL_TM, seg)
    const = lambda shape: pl.BlockSpec(shape, lambda i: (0,) * len(shape), pipeline_mode=pl.Buffered(1))
    kern = functools.partial(_tail_kernel, heads=ML_HEADS, alpha=alpha)
    return pl.pallas_call(
        kern,
        grid=(t // tm,),
        in_specs=[
            pl.BlockSpec((tm, d), lambda i: (i, 0)),
            pl.BlockSpec((tm, attn.shape[1]), lambda i: (i, 0)),
            pl.BlockSpec((tm, wa), lambda i: (i, 0)),
            pl.BlockSpec((tm, wa), lambda i: (i, 0)),
            pl.BlockSpec((tm, wa), lambda i: (i, o_col0 // wa)),
            pl.BlockSpec((tm, wa), lambda i: (i, za_col0 // wa)),
            pl.BlockSpec((tm, d), lambda i: (i, g_col0 // d)),
            pl.BlockSpec((tm, d), lambda i: (i, g_col0 // d + 1)),
            pl.BlockSpec((1, 1, d), lambda i: ((i * tm) // seg, 0, 0)),
            const((1, wa)),
            const(wpa.shape),
            const(wpb.shape),
            const(wo.shape),
            const((1, d)),
            const((1, d)),
        ],
        out_specs=pl.BlockSpec((tm, d), lambda i: (i, 0)),
        out_shape=jax.ShapeDtypeStruct((t, d), F32),
        compiler_params=_cparams(1),
        name="tail",
    )(x, attn, hf, hb, pm, pm, pm, pm, gate, mlw, wpa, wpb, wo, lnw, lnb)


def _rot_half(w):
    half = w.shape[-1] // 2
    return jnp.concatenate([-w[..., half:], w[..., :half]], axis=-1)


def _prep_weights(w_in, b_if, w_q_b, w_kv_b, w_proj_a, w_proj_b, w_out, d_model, q_rank, kv_rank):
    depth = w_in.shape[0]
    wa = ML_HEADS * ML_HEAD_DIM
    wb = MLA_HEADS * MLA_V_DIM
    sizes = [q_rank, kv_rank, MLA_ROPE_DIM, wb, wa, wa, wa, wa, wa, 4 * ML_HEADS, 2 * d_model]
    offs = [0]
    for s in sizes:
        offs.append(offs[-1] + s)
    part = lambda i: w_in[:, :, offs[i]:offs[i + 1]]
    (q_lat, kv_lat, k_rope, z_b, ml_q, ml_k, ml_v, ml_o, z_a, ml_g, merge_g) = [part(i) for i in range(11)]
    main_parts = [q_lat, kv_lat, z_b, ml_q, ml_k, ml_v, ml_o, z_a, merge_g]
    w_main = jnp.concatenate(main_parts, axis=-1).astype(BF16)
    names = ["q_lat", "kv_lat", "z_b", "ml_q", "ml_k", "ml_v", "ml_o", "z_a", "merge_g"]
    cols, o = {}, 0
    for nme, p in zip(names, main_parts):
        cols[nme] = o
        o += p.shape[-1]
    pad = jnp.zeros(w_in.shape[:2] + (LANES - 4 * ML_HEADS,), w_in.dtype)
    w_small = jnp.concatenate([k_rope, _rot_half(k_rope), ml_g, pad], axis=-1).astype(BF16)

    wq = w_q_b.reshape(depth, q_rank, MLA_HEADS, MLA_QK_DIM)
    wq_r = wq[..., MLA_NOPE_DIM:]
    wq = jnp.concatenate([wq[..., :MLA_NOPE_DIM], wq_r, _rot_half(wq_r)], axis=-1)
    wq = wq.reshape(depth, q_rank, MLA_HEADS * HEAD_LANES).astype(BF16)
    wkv = w_kv_b.reshape(depth, kv_rank, MLA_HEADS, MLA_NOPE_DIM + MLA_V_DIM)
    wk = wkv[..., :MLA_NOPE_DIM].reshape(depth, kv_rank, MLA_HEADS * MLA_NOPE_DIM).astype(BF16)
    wv = wkv[..., MLA_NOPE_DIM:].reshape(depth, kv_rank, MLA_HEADS * MLA_V_DIM).astype(BF16)
    bif = jnp.pad(b_if.reshape(depth, 1, 4 * ML_HEADS), ((0, 0), (0, 0), (0, LANES - 4 * ML_HEADS)))
    return dict(w_main=w_main, w_small=w_small, cols=cols, wq=wq, wk=wk, wv=wv, bif=bif,
                wpa=w_proj_a.astype(BF16), wpb=w_proj_b.astype(BF16), wo=w_out.astype(BF16))


def _rope_lane_tables(seq_len):
    inv_freq = ROPE_THETA ** (-jnp.arange(0, MLA_ROPE_DIM, 2, dtype=F32) / MLA_ROPE_DIM)
    ang = jnp.arange(seq_len, dtype=F32)[:, None] * inv_freq[None, :]
    zeros = jnp.zeros((seq_len, LANES - MLA_ROPE_DIM), F32)
    cos, sin = jnp.cos(ang), jnp.sin(ang)
    return (jnp.concatenate([cos, cos, zeros], axis=1), jnp.concatenate([sin, sin, zeros], axis=1))


def kernel(x_prompt, x_sample, c_prompt, c_sample, w_ada, b_ada, w_in, b_if, q_norm_w, kv_norm_w,
           w_q_b, w_kv_b, ml_norm_w, w_proj_a, w_proj_b, w_out, ln_w, ln_b):
    batch, seq, d = x_prompt.shape
    dec_batch, dec_seq, _ = x_sample.shape
    depth = w_ada.shape[0]
    q_rank = q_norm_w.shape[1]
    kv_rank = kv_norm_w.shape[1]
    alpha = (2 * depth) ** 0.25
    assert seq % dec_seq == 0 and dec_seq % ML_CHUNK == 0
    seg = dec_seq
    t_prompt = batch * seq
    t = t_prompt + dec_batch * dec_seq

    x = jnp.concatenate([x_prompt.reshape(t_prompt, d), x_sample.reshape(dec_batch * dec_seq, d)], axis=0)
    c_all = jnp.concatenate([c_prompt, c_sample], axis=0)
    n_cond = c_all.shape[0]
    rows = -(-n_cond // 16) * 16
    c_pad = jnp.pad(c_all, ((0, rows - n_cond), (0, 0)))
    seg_cond = jnp.concatenate([jnp.repeat(jnp.arange(batch), seq // seg),
                                batch + jnp.arange(dec_batch)])

    mod = _adaln_mod(c_pad, w_ada, b_ada)
    mod = mod[:, seg_cond, :].reshape(depth, t // seg, 1, 3, d)
    shift, scale, gate = mod[..., 0, :], mod[..., 1, :], mod[..., 2, :]

    w = _prep_weights(w_in, b_if, w_q_b, w_kv_b, w_proj_a, w_proj_b, w_out, d, q_rank, kv_rank)
    cols = w["cols"]
    cs_tab, sn_tab = _rope_lane_tables(max(seq, dec_seq))
    groups = ((0, batch, seq), (t_prompt, dec_batch, dec_seq))

    for l in range(depth):
        pm, ps = _inproj(x, shift[l], scale[l], w["w_main"][l], w["w_small"][l], seg)
        qf, kf, vf = _qkv(pm, ps, cs_tab, sn_tab, q_norm_w[l][None], kv_norm_w[l][None],
                          w["wq"][l], w["wk"][l], w["wv"][l],
                          t_prompt=t_prompt, dec_seq=dec_seq, q_rank=q_rank, kv_rank=kv_rank)
        attn, hf, hb = [], [], []
        for row0, n_seq, s_len in groups:
            attn.append(_attention(qf, kf, vf, pm, row0=row0, n_seq=n_seq, seq=s_len, zb_col0=cols["z_b"]))
            f, b = _mlstm(pm, ps, w["bif"][l], row0=row0, n_seq=n_seq, seq=s_len, q_col0=cols["ml_q"])
            hf.append(f)
            hb.append(b)
        x = _tail(x, jnp.concatenate(attn, axis=0), jnp.concatenate(hf, axis=0), jnp.concatenate(hb, axis=0),
                  pm, gate[l], ml_norm_w[l][None], w["wpa"][l], w["wpb"][l], w["wo"][l],
                  ln_w[l][None], ln_b[l][None], seg=seg, alpha=alpha,
                  o_col0=cols["ml_o"], za_col0=cols["z_a"], g_col0=cols["merge_g"])

    y_prompt = x[:t_prompt].reshape(batch, seq, d)
    y_sample = x[t_prompt:].reshape(dec_batch, dec_seq, d)
    return (y_prompt, y_sample)
```

```python
import functools

import jax
import jax.numpy as jnp
from jax import lax
from jax.experimental import pallas as pl
from jax.experimental.pallas import tpu as pltpu

F32 = jnp.float32
BF16 = jnp.bfloat16

MLA_HEADS = 16
MLA_NOPE_DIM = 128
MLA_ROPE_DIM = 64
MLA_V_DIM = 128
MLA_QK_DIM = MLA_NOPE_DIM + MLA_ROPE_DIM
ROPE_THETA = 10000.0
ML_HEADS = 8
ML_HEAD_DIM = 128
ML_CHUNK = 128
LN_EPS = 1e-5
RMS_EPS = 1e-6
LOG2_E = 1.4426950408889634

LANES = 128
HEAD_LANES = 2 * LANES
VMEM_LIMIT_BYTES = 56 * 1024 * 1024

MOD_TN = 1024
INPROJ_TM = 1024
INPROJ_TN = 1024
QKV_TM = 256
ATTN_TQ = 1024
ATTN_TK = 512
ATTN_RB = 128
TAIL_TM = 256


def _cparams(n_axes):
    return pltpu.CompilerParams(
        dimension_semantics=("arbitrary",) * n_axes,
        vmem_limit_bytes=VMEM_LIMIT_BYTES,
    )


def _sigmoid(x):
    return 1.0 / (1.0 + jnp.exp(-x))


def _silu(x):
    return x * _sigmoid(x)


def _log_sigmoid(x):
    return jnp.minimum(x, 0.0) - jnp.log(1.0 + jnp.exp(-jnp.abs(x)))


def _layer_norm_rows(x):
    mu = jnp.mean(x, axis=-1, keepdims=True)
    xc = x - mu
    var = jnp.mean(xc * xc, axis=-1, keepdims=True)
    return xc * lax.rsqrt(var + LN_EPS)


def _mod_kernel(c_ref, w_ref, b_ref, o_ref):
    c = c_ref[...]
    a = _silu(c).astype(BF16)
    o_ref[0] = jnp.dot(a, w_ref[0].astype(BF16), preferred_element_type=F32) + b_ref[0]


def _adaln_mod(c_pad, w_ada, b_ada):
    depth, d, n = w_ada.shape
    rows = c_pad.shape[0]
    tn = min(MOD_TN, n)
    return pl.pallas_call(
        _mod_kernel,
        grid=(depth, n // tn),
        in_specs=[
            pl.BlockSpec((rows, d), lambda l, j: (0, 0)),
            pl.BlockSpec((1, d, tn), lambda l, j: (l, 0, j)),
            pl.BlockSpec((1, 1, tn), lambda l, j: (l, 0, j)),
        ],
        out_specs=pl.BlockSpec((1, rows, tn), lambda l, j: (l, 0, j)),
        out_shape=jax.ShapeDtypeStruct((depth, rows, n), F32),
        compiler_params=_cparams(2),
        name="adaln_mod",
    )(c_pad, w_ada, b_ada.reshape(depth, 1, n))


def _inproj_kernel(x_ref, sh_ref, sc_ref, wm_ref, ws_ref, pm_ref, ps_ref, u_ref):
    @pl.when(pl.program_id(1) == 0)
    def _():
        u = _layer_norm_rows(x_ref[...]) * (1.0 + sc_ref[0]) + sh_ref[0]
        ub = u.astype(BF16)
        u_ref[...] = ub
        ps_ref[...] = jnp.dot(ub, ws_ref[...], preferred_element_type=F32)

    pm_ref[...] = jnp.dot(u_ref[...], wm_ref[...], preferred_element_type=F32).astype(BF16)


def _inproj(x, shift, scale, w_main, w_small, seg):
    t, d = x.shape
    n = w_main.shape[1]
    ns = w_small.shape[1]
    tm = min(INPROJ_TM, seg)
    tn = min(INPROJ_TN, n)
    seg_of = lambda i, j: ((i * tm) // seg, 0, 0)
    return pl.pallas_call(
        _inproj_kernel,
        grid=(t // tm, n // tn),
        in_specs=[
            pl.BlockSpec((tm, d), lambda i, j: (i, 0)),
            pl.BlockSpec((1, 1, d), seg_of),
            pl.BlockSpec((1, 1, d), seg_of),
            pl.BlockSpec((d, tn), lambda i, j: (0, j)),
            pl.BlockSpec((d, ns), lambda i, j: (0, 0)),
        ],
        out_specs=[
            pl.BlockSpec((tm, tn), lambda i, j: (i, j)),
            pl.BlockSpec((tm, ns), lambda i, j: (i, 0)),
        ],
        out_shape=[
            jax.ShapeDtypeStruct((t, n), BF16),
            jax.ShapeDtypeStruct((t, ns), F32),
        ],
        scratch_shapes=[pltpu.VMEM((tm, d), BF16)],
        compiler_params=_cparams(2),
        name="inproj",
    )(x, shift, scale, w_main, w_small)


def _rope_lanes(x, cs, sn):
    return x * cs + pltpu.roll(x, LANES // 2, axis=1) * sn


def _qkv_kernel(ql_ref, kvl_ref, ps_ref, cs_ref, sn_ref, qnw_ref, kvnw_ref, wq_ref, wk_ref, wv_ref,
                q_out, k_out, v_out, *, heads, qk_scale):
    def rms(v, w):
        return (v * lax.rsqrt(jnp.mean(v * v, axis=-1, keepdims=True) + RMS_EPS) * w).astype(BF16)

    qn = rms(ql_ref[...].astype(F32), qnw_ref[...])
    kvn = rms(kvl_ref[...].astype(F32), kvnw_ref[...])
    cs = cs_ref[...]
    sn = sn_ref[...]
    kr = _rope_lanes(ps_ref[...], cs, sn).astype(BF16)
    tm = kr.shape[0]
    lane = lax.broadcasted_iota(jnp.int32, (tm, LANES), 1)
    ones_col = jnp.where(lane == 0, 1.0, 0.0).astype(BF16)
    for h in range(heads):
        qa = jnp.dot(qn, wq_ref[:, h * HEAD_LANES:(h + 1) * HEAD_LANES], preferred_element_type=F32)
        q_out[h, :, :LANES] = (qa[:, :LANES] * qk_scale).astype(BF16)
        q_out[h, :, LANES:] = (_rope_lanes(qa[:, LANES:], cs, sn) * qk_scale).astype(BF16)
        ka = jnp.dot(kvn, wk_ref[:, h * LANES:(h + 1) * LANES], preferred_element_type=F32)
        k_out[h, :, :LANES] = ka.astype(BF16)
        k_out[h, :, LANES:] = kr
        va = jnp.dot(kvn, wv_ref[:, h * LANES:(h + 1) * LANES], preferred_element_type=F32)
        v_out[h, :, :LANES] = va.astype(BF16)
        v_out[h, :, LANES:] = ones_col


def _qkv(pm, ps, cs_tab, sn_tab, qnw, kvnw, wq, wk, wv, *, t_prompt, dec_seq, q_rank, kv_rank):
    t = pm.shape[0]
    heads = MLA_HEADS
    tm = min(QKV_TM, dec_seq)
    assert q_rank == kv_rank and q_rank % LANES == 0
    np_blocks = t_prompt // tm
    dec_blocks = dec_seq // tm

    def pos_block(i):
        return (jnp.where(i < np_blocks, i, (i - np_blocks) % dec_blocks), 0)

    out_sds = jax.ShapeDtypeStruct((heads, t, HEAD_LANES), BF16)
    out_spec = pl.BlockSpec((heads, tm, HEAD_LANES), lambda i: (0, i, 0))
    kern = functools.partial(_qkv_kernel, heads=heads, qk_scale=MLA_QK_DIM ** -0.5 * LOG2_E)
    return pl.pallas_call(
        kern,
        grid=(t // tm,),
        in_specs=[
            pl.BlockSpec((tm, q_rank), lambda i: (i, 0)),
            pl.BlockSpec((tm, kv_rank), lambda i: (i, 1)),
            pl.BlockSpec((tm, LANES), lambda i: (i, 0)),
            pl.BlockSpec((tm, LANES), pos_block),
            pl.BlockSpec((tm, LANES), pos_block),
            pl.BlockSpec((1, q_rank), lambda i: (0, 0)),
            pl.BlockSpec((1, kv_rank), lambda i: (0, 0)),
            pl.BlockSpec(wq.shape, lambda i: (0, 0)),
            pl.BlockSpec(wk.shape, lambda i: (0, 0)),
            pl.BlockSpec(wv.shape, lambda i: (0, 0)),
        ],
        out_specs=[out_spec, out_spec, out_spec],
        out_shape=[out_sds, out_sds, out_sds],
        compiler_params=_cparams(1),
        name="qkv_up",
    )(pm, pm, ps, cs_tab, sn_tab, qnw, kvnw, wq, wk, wv)


def _attn_kernel(q_ref, k_ref, v_ref, zb_ref, o_ref, s0_ref, s1_ref, m_ref, acc_ref, *, tk, nk, rb):
    q = q_ref[0]
    tq = q.shape[0]

    def scores(j, s_ref):
        start = pl.multiple_of(j * tk, tk)
        s_ref[...] = lax.dot_general(q, k_ref[0, pl.ds(start, tk), :], (((1,), (1,)), ((), ())),
                                     preferred_element_type=F32)

    def softmax_pv(j, s_ref):
        start = pl.multiple_of(j * tk, tk)
        v = v_ref[0, pl.ds(start, tk), :]
        for r in range(tq // rb):
            rows = slice(r * rb, (r + 1) * rb)
            tiles = [s_ref[rows, c * LANES:(c + 1) * LANES] for c in range(tk // LANES)]
            mx = tiles[0]
            for t in tiles[1:]:
                mx = jnp.maximum(mx, t)
            m_prev = m_ref[rows, :]
            m_new = jnp.maximum(m_prev, jnp.max(mx, axis=1, keepdims=True))
            p = jnp.concatenate([jnp.exp2(t - m_new).astype(BF16) for t in tiles], axis=1)
            alpha = jnp.exp2(m_prev - m_new)
            pv = jnp.dot(p, v, preferred_element_type=F32)
            for c in range(HEAD_LANES // LANES):
                cols = slice(c * LANES, (c + 1) * LANES)
                acc_ref[rows, cols] = acc_ref[rows, cols] * alpha + pv[:, cols]
            m_ref[rows, :] = m_new

    m_ref[...] = jnp.full(m_ref.shape, -jnp.inf, F32)
    acc_ref[...] = jnp.zeros(acc_ref.shape, F32)
    scores(0, s0_ref)

    def body(jj, carry):
        j = 2 * jj
        scores(j + 1, s1_ref)
        softmax_pv(j, s0_ref)
        scores(jnp.minimum(j + 2, nk - 1), s0_ref)
        softmax_pv(j + 1, s1_ref)
        return carry

    lax.fori_loop(0, nk // 2, body, 0, unroll=2)
    o = acc_ref[:, :LANES] / acc_ref[:, LANES:LANES + 1]
    o_ref[...] = (o * _silu(zb_ref[...].astype(F32))).astype(BF16)


def _attention(qf, kf, vf, pm, *, row0, n_seq, seq, zb_col0):
    heads = qf.shape[0]
    tq = min(ATTN_TQ, seq)
    tk = min(ATTN_TK, seq)
    nq = seq // tq
    qb0 = row0 // tq
    sb0 = row0 // seq
    zc0 = zb_col0 // LANES
    nk = seq // tk
    assert nk % 2 == 0
    kern = functools.partial(_attn_kernel, tk=tk, nk=nk, rb=min(ATTN_RB, tq))
    return pl.pallas_call(
        kern,
        grid=(n_seq, heads, nq),
        in_specs=[
            pl.BlockSpec((1, tq, HEAD_LANES), lambda n, h, i: (h, qb0 + n * nq + i, 0)),
            pl.BlockSpec((1, seq, HEAD_LANES), lambda n, h, i: (h, sb0 + n, 0)),
            pl.BlockSpec((1, seq, HEAD_LANES), lambda n, h, i: (h, sb0 + n, 0)),
            pl.BlockSpec((tq, LANES), lambda n, h, i: (qb0 + n * nq + i, zc0 + h)),
        ],
        out_specs=pl.BlockSpec((tq, LANES), lambda n, h, i: (n * nq + i, h)),
        out_shape=jax.ShapeDtypeStruct((n_seq * seq, heads * LANES), BF16),
        scratch_shapes=[
            pltpu.VMEM((tq, tk), F32),
            pltpu.VMEM((tq, tk), F32),
            pltpu.VMEM((tq, LANES), F32),
            pltpu.VMEM((tq, HEAD_LANES), F32),
        ],
        compiler_params=_cparams(3),
        name="mla_attention",
    )(qf, kf, vf, pm)


def _mlstm_kernel(qf_ref, kf_ref, vf_ref, gf_ref, qb_ref, kb_ref, vb_ref, gb_ref, bif_ref,
                  hf_ref, hb_ref, st_ref, m_ref, *, heads, k_scale):
    L = ML_CHUNK
    dh = ML_HEAD_DIM

    @pl.when(pl.program_id(1) == 0)
    def _():
        st_ref[...] = jnp.zeros_like(st_ref)
        m_ref[...] = jnp.zeros_like(m_ref)

    row = lax.broadcasted_iota(jnp.int32, (L, L), 0)
    col = lax.broadcasted_iota(jnp.int32, (L, L), 1)
    lane = lax.broadcasted_iota(jnp.int32, (L, LANES), 1)
    ones_col = jnp.where(lane == 0, 1.0, 0.0).astype(BF16)

    dirs = ((qf_ref, kf_ref, vf_ref, gf_ref, hf_ref, col <= row, L - 1),
            (qb_ref, kb_ref, vb_ref, gb_ref, hb_ref, col >= row, 0))
    for d, (q_ref, k_ref, v_ref, g_ref, h_ref, mask, last) in enumerate(dirs):
        g = g_ref[...] + bif_ref[...]
        lf = _log_sigmoid(g)
        b_all = jnp.dot(mask.astype(F32), lf, preferred_element_type=F32,
                        precision=lax.Precision.HIGHEST)
        g_t = g.T
        b_t = b_all.T
        for h in range(heads):
            ci = 2 * heads * d + h
            cf = ci + heads
            idx = d * heads + h
            bcol = b_all[:, cf:cf + 1]
            brow = b_t[cf:cf + 1, :]
            icol = g[:, ci:ci + 1]
            irow = g_t[ci:ci + 1, :]
            m_prev = m_ref[idx][:, 0:1]
            logd = jnp.where(mask, bcol - brow + irow, -jnp.inf)
            inter = bcol + m_prev
            m_t = jnp.maximum(inter, jnp.max(logd, axis=1, keepdims=True))
            dmat = jnp.exp(logd - m_t)
            w_inter = jnp.exp(inter - m_t)
            q = q_ref[:, h * dh:(h + 1) * dh]
            k = k_ref[:, h * dh:(h + 1) * dh]
            v_aug = jnp.concatenate([v_ref[:, h * dh:(h + 1) * dh], ones_col], axis=1)
            s = lax.dot_general(q, k, (((1,), (1,)), ((), ())), preferred_element_type=F32)
            s = (s * k_scale * dmat).astype(BF16)
            st_prev = st_ref[idx]
            nd = (jnp.dot(s, v_aug, preferred_element_type=F32)
                  + w_inter * jnp.dot(q, st_prev.astype(BF16), preferred_element_type=F32))
            den = nd[:, dh:dh + 1]
            hout = nd[:, :dh] / jnp.maximum(jnp.abs(den), jnp.exp(-m_t))
            h_ref[:, h * dh:(h + 1) * dh] = hout.astype(h_ref.dtype)
            m_new = m_t[last:last + 1, :]
            b_last = bcol[last:last + 1, :]
            wcol = jnp.exp(b_last - bcol + icol - m_new) * k_scale
            decay = jnp.exp(b_last + m_prev - m_new)
            wv = (v_aug.astype(F32) * wcol).astype(BF16)
            upd = lax.dot_general(k, wv, (((0,), (0,)), ((), ())), preferred_element_type=F32)
            st_ref[idx] = decay * st_prev + upd
            m_ref[idx] = jnp.broadcast_to(m_new, (1, LANES))


def _mlstm(pm, ps, bif, *, row0, n_seq, seq, q_col0):
    heads = ML_HEADS
    width = heads * ML_HEAD_DIM
    L = ML_CHUNK
    nc = seq // L
    rb0 = row0 // L
    qc = q_col0 // width
    fwd = lambda n, c: rb0 + n * nc + c
    bwd = lambda n, c: rb0 + n * nc + (nc - 1 - c)

    def col_spec(chunk_of, cb):
        return pl.BlockSpec((L, width), lambda n, c: (chunk_of(n, c), cb))

    gate_f = pl.BlockSpec((L, LANES), lambda n, c: (fwd(n, c), 1))
    gate_b = pl.BlockSpec((L, LANES), lambda n, c: (bwd(n, c), 1))
    out_sds = jax.ShapeDtypeStruct((n_seq * seq, width), BF16)
    kern = functools.partial(_mlstm_kernel, heads=heads, k_scale=ML_HEAD_DIM ** -0.5)
    return pl.pallas_call(
        kern,
        grid=(n_seq, nc),
        in_specs=[
            col_spec(fwd, qc), col_spec(fwd, qc + 1), col_spec(fwd, qc + 2), gate_f,
            col_spec(bwd, qc), col_spec(bwd, qc + 1), col_spec(bwd, qc + 2), gate_b,
            pl.BlockSpec((1, LANES), lambda n, c: (0, 0)),
        ],
        out_specs=[
            pl.BlockSpec((L, width), lambda n, c: (n * nc + c, 0)),
            pl.BlockSpec((L, width), lambda n, c: (n * nc + (nc - 1 - c), 0)),
        ],
        out_shape=[out_sds, out_sds],
        scratch_shapes=[
            pltpu.VMEM((2 * heads, ML_HEAD_DIM, HEAD_LANES), F32),
            pltpu.VMEM((2 * heads, 1, LANES), F32),
        ],
        compiler_params=_cparams(2),
        name="mlstm_bidir",
    )(pm, pm, pm, ps, pm, pm, pm, ps, bif)


def _tail_kernel(x_ref, attn_ref, hf_ref, hb_ref, o_ref, za_ref, ga_ref, gb_ref, gate_ref,
                 mlw_ref, wpa_ref, wpb_ref, wo_ref, lnw_ref, lnb_ref, y_ref, *, heads, alpha):
    dh = ML_HEAD_DIM
    hs = hf_ref[...].astype(F32) + hb_ref[...].astype(F32)
    hn = jnp.concatenate(
        [_layer_norm_rows(hs[:, h * dh:(h + 1) * dh]) for h in range(heads)], axis=1)
    a_in = hn * mlw_ref[...] * _sigmoid(o_ref[...].astype(F32)) * _silu(za_ref[...].astype(F32))
    y_a = jnp.dot(a_in.astype(BF16), wpa_ref[...], preferred_element_type=F32)
    y_b = jnp.dot(attn_ref[...], wpb_ref[...], preferred_element_type=F32)
    merged = (_sigmoid(ga_ref[...].astype(F32)) * y_a + _sigmoid(gb_ref[...].astype(F32)) * y_b)
    out = jnp.dot(merged.astype(BF16), wo_ref[...], preferred_element_type=F32)
    r = alpha * x_ref[...] + gate_ref[0] * out
    y_ref[...] = _layer_norm_rows(r) * lnw_ref[...] + lnb_ref[...]


def _tail(x, attn, hf, hb, pm, gate, mlw, wpa, wpb, wo, lnw, lnb, *, seg, alpha, o_col0, za_col0, g_col0):
    t, d = x.shape
    wa = hf.shape[1]
    tm = min(TAIL_TM, seg)
    const = lambda shape: pl.BlockSpec(shape, lambda i: (0,) * len(shape), pipeline_mode=pl.Buffered(1))
    kern = functools.partial(_tail_kernel, heads=ML_HEADS, alpha=alpha)
    return pl.pallas_call(
        kern,
        grid=(t // tm,),
        in_specs=[
            pl.BlockSpec((tm, d), lambda i: (i, 0)),
            pl.BlockSpec((tm, attn.shape[1]), lambda i: (i, 0)),
            pl.BlockSpec((tm, wa), lambda i: (i, 0)),
            pl.BlockSpec((tm, wa), lambda i: (i, 0)),
            pl.BlockSpec((tm, wa), lambda i: (i, o_col0 // wa)),
            pl.BlockSpec((tm, wa), lambda i: (i, za_col0 // wa)),
            pl.BlockSpec((tm, d), lambda i: (i, g_col0 // d)),
            pl.BlockSpec((tm, d), lambda i: (i, g_col0 // d + 1)),
            pl.BlockSpec((1, 1, d), lambda i: ((i * tm) // seg, 0, 0)),
            const((1, wa)),
            const(wpa.shape),
            const(wpb.shape),
            const(wo.shape),
            const((1, d)),
            const((1, d)),
        ],
        out_specs=pl.BlockSpec((tm, d), lambda i: (i, 0)),
        out_shape=jax.ShapeDtypeStruct((t, d), F32),
        compiler_params=_cparams(1),
        name="tail",
    )(x, attn, hf, hb, pm, pm, pm, pm, gate, mlw, wpa, wpb, wo, lnw, lnb)


def _rot_half(w):
    half = w.shape[-1] // 2
    return jnp.concatenate([-w[..., half:], w[..., :half]], axis=-1)


def _prep_weights(w_in, b_if, w_q_b, w_kv_b, w_proj_a, w_proj_b, w_out, d_model, q_rank, kv_rank):
    depth = w_in.shape[0]
    wa = ML_HEADS * ML_HEAD_DIM
    wb = MLA_HEADS * MLA_V_DIM
    sizes = [q_rank, kv_rank, MLA_ROPE_DIM, wb, wa, wa, wa, wa, wa, 4 * ML_HEADS, 2 * d_model]
    offs = [0]
    for s in sizes:
        offs.append(offs[-1] + s)
    part = lambda i: w_in[:, :, offs[i]:offs[i + 1]]
    (q_lat, kv_lat, k_rope, z_b, ml_q, ml_k, ml_v, ml_o, z_a, ml_g, merge_g) = [part(i) for i in range(11)]
    main_parts = [q_lat, kv_lat, z_b, ml_q, ml_k, ml_v, ml_o, z_a, merge_g]
    w_main = jnp.concatenate(main_parts, axis=-1).astype(BF16)
    names = ["q_lat", "kv_lat", "z_b", "ml_q", "ml_k", "ml_v", "ml_o", "z_a", "merge_g"]
    cols, o = {}, 0
    for nme, p in zip(names, main_parts):
        cols[nme] = o
        o += p.shape[-1]
    pad = jnp.zeros(w_in.shape[:2] + (LANES - 4 * ML_HEADS,), w_in.dtype)
    w_small = jnp.concatenate([k_rope, _rot_half(k_rope), ml_g, pad], axis=-1).astype(BF16)

    wq = w_q_b.reshape(depth, q_rank, MLA_HEADS, MLA_QK_DIM)
    wq_r = wq[..., MLA_NOPE_DIM:]
    wq = jnp.concatenate([wq[..., :MLA_NOPE_DIM], wq_r, _rot_half(wq_r)], axis=-1)
    wq = wq.reshape(depth, q_rank, MLA_HEADS * HEAD_LANES).astype(BF16)
    wkv = w_kv_b.reshape(depth, kv_rank, MLA_HEADS, MLA_NOPE_DIM + MLA_V_DIM)
    wk = wkv[..., :MLA_NOPE_DIM].reshape(depth, kv_rank, MLA_HEADS * MLA_NOPE_DIM).astype(BF16)
    wv = wkv[..., MLA_NOPE_DIM:].reshape(depth, kv_rank, MLA_HEADS * MLA_V_DIM).astype(BF16)
    bif = jnp.pad(b_if.reshape(depth, 1, 4 * ML_HEADS), ((0, 0), (0, 0), (0, LANES - 4 * ML_HEADS)))
    return dict(w_main=w_main, w_small=w_small, cols=cols, wq=wq, wk=wk, wv=wv, bif=bif,
                wpa=w_proj_a.astype(BF16), wpb=w_proj_b.astype(BF16), wo=w_out.astype(BF16))


def _rope_lane_tables(seq_len):
    inv_freq = ROPE_THETA ** (-jnp.arange(0, MLA_ROPE_DIM, 2, dtype=F32) / MLA_ROPE_DIM)
    ang = jnp.arange(seq_len, dtype=F32)[:, None] * inv_freq[None, :]
    zeros = jnp.zeros((seq_len, LANES - MLA_ROPE_DIM), F32)
    cos, sin = jnp.cos(ang), jnp.sin(ang)
    return (jnp.concatenate([cos, cos, zeros], axis=1), jnp.concatenate([sin, sin, zeros], axis=1))


def kernel(x_prompt, x_sample, c_prompt, c_sample, w_ada, b_ada, w_in, b_if, q_norm_w, kv_norm_w,
           w_q_b, w_kv_b, ml_norm_w, w_proj_a, w_proj_b, w_out, ln_w, ln_b):
    batch, seq, d = x_prompt.shape
    dec_batch, dec_seq, _ = x_sample.shape
    depth = w_ada.shape[0]
    q_rank = q_norm_w.shape[1]
    kv_rank = kv_norm_w.shape[1]
    alpha = (2 * depth) ** 0.25
    assert seq % dec_seq == 0 and dec_seq % ML_CHUNK == 0
    seg = dec_seq
    t_prompt = batch * seq
    t = t_prompt + dec_batch * dec_seq

    x = jnp.concatenate([x_prompt.reshape(t_prompt, d), x_sample.reshape(dec_batch * dec_seq, d)], axis=0)
    c_all = jnp.concatenate([c_prompt, c_sample], axis=0)
    n_cond = c_all.shape[0]
    rows = -(-n_cond // 16) * 16
    c_pad = jnp.pad(c_all, ((0, rows - n_cond), (0, 0)))
    seg_cond = jnp.concatenate([jnp.repeat(jnp.arange(batch), seq // seg),
                                batch + jnp.arange(dec_batch)])

    mod = _adaln_mod(c_pad, w_ada, b_ada)
    mod = mod[:, seg_cond, :].reshape(depth, t // seg, 1, 3, d)
    shift, scale, gate = mod[..., 0, :], mod[..., 1, :], mod[..., 2, :]

    w = _prep_weights(w_in, b_if, w_q_b, w_kv_b, w_proj_a, w_proj_b, w_out, d, q_rank, kv_rank)
    cols = w["cols"]
    cs_tab, sn_tab = _rope_lane_tables(max(seq, dec_seq))
    groups = ((0, batch, seq), (t_prompt, dec_batch, dec_seq))

    for l in range(depth):
        pm, ps = _inproj(x, shift[l], scale[l], w["w_main"][l], w["w_small"][l], seg)
        qf, kf, vf = _qkv(pm, ps, cs_tab, sn_tab, q_norm_w[l][None], kv_norm_w[l][None],
                          w["wq"][l], w["wk"][l], w["wv"][l],
                          t_prompt=t_prompt, dec_seq=dec_seq, q_rank=q_rank, kv_rank=kv_rank)
        attn, hf, hb = [], [], []
        for row0, n_seq, s_len in groups:
            attn.append(_attention(qf, kf, vf, pm, row0=row0, n_seq=n_seq, seq=s_len, zb_col0=cols["z_b"]))
            f, b = _mlstm(pm, ps, w["bif"][l], row0=row0, n_seq=n_seq, seq=s_len, q_col0=cols["ml_q"])
            hf.append(f)
            hb.append(b)
        x = _tail(x, jnp.concatenate(attn, axis=0), jnp.concatenate(hf, axis=0), jnp.concatenate(hb, axis=0),
                  pm, gate[l], ml_norm_w[l][None], w["wpa"][l], w["wpb"][l], w["wo"][l],
                  ln_w[l][None], ln_b[l][None], seg=seg, alpha=alpha,
                  o_col0=cols["ml_o"], za_col0=cols["z_a"], g_col0=cols["merge_g"])

    y_prompt = x[:t_prompt].reshape(batch, seq, d)
    y_sample = x[t_prompt:].reshape(dec_batch, dec_seq, d)
    return (y_prompt, y_sample)
```

```python
import functools

import jax
import jax.numpy as jnp
from jax import lax
from jax.experimental import pallas as pl
from jax.experimental.pallas import tpu as pltpu

F32 = jnp.float32
BF16 = jnp.bfloat16

MLA_HEADS = 16
MLA_NOPE_DIM = 128
MLA_ROPE_DIM = 64
MLA_V_DIM = 128
MLA_QK_DIM = MLA_NOPE_DIM + MLA_ROPE_DIM
ROPE_THETA = 10000.0
ML_HEADS = 8
ML_HEAD_DIM = 128
ML_CHUNK = 128
LN_EPS = 1e-5
RMS_EPS = 1e-6
LOG2_E = 1.4426950408889634

LANES = 128
HEAD_LANES = 2 * LANES
VT_ROWS = MLA_V_DIM + 16
VMEM_LIMIT_BYTES = 56 * 1024 * 1024

MOD_TN = 1024
INPROJ_TM = 1024
INPROJ_TN = 1024
QKV_TM = 256
ATTN_TQ = 2048
ATTN_TK = 512
ATTN_CB = 128
ATTN_CB_PV = 256
TAIL_TM = 256


def _cparams(n_axes):
    return pltpu.CompilerParams(
        dimension_semantics=("arbitrary",) * n_axes,
        vmem_limit_bytes=VMEM_LIMIT_BYTES,
    )


def _sigmoid(x):
    return 1.0 / (1.0 + jnp.exp(-x))


def _silu(x):
    return x * _sigmoid(x)


def _log_sigmoid(x):
    return jnp.minimum(x, 0.0) - jnp.log(1.0 + jnp.exp(-jnp.abs(x)))


def _layer_norm_rows(x):
    mu = jnp.mean(x, axis=-1, keepdims=True)
    xc = x - mu
    var = jnp.mean(xc * xc, axis=-1, keepdims=True)
    return xc * lax.rsqrt(var + LN_EPS)


def _mod_kernel(c_ref, w_ref, b_ref, o_ref):
    c = c_ref[...]
    a = _silu(c).astype(BF16)
    o_ref[0] = jnp.dot(a, w_ref[0].astype(BF16), preferred_element_type=F32) + b_ref[0]


def _adaln_mod(c_pad, w_ada, b_ada):
    depth, d, n = w_ada.shape
    rows = c_pad.shape[0]
    tn = min(MOD_TN, n)
    return pl.pallas_call(
        _mod_kernel,
        grid=(depth, n // tn),
        in_specs=[
            pl.BlockSpec((rows, d), lambda l, j: (0, 0)),
            pl.BlockSpec((1, d, tn), lambda l, j: (l, 0, j)),
            pl.BlockSpec((1, 1, tn), lambda l, j: (l, 0, j)),
        ],
        out_specs=pl.BlockSpec((1, rows, tn), lambda l, j: (l, 0, j)),
        out_shape=jax.ShapeDtypeStruct((depth, rows, n), F32),
        compiler_params=_cparams(2),
        name="adaln_mod",
    )(c_pad, w_ada, b_ada.reshape(depth, 1, n))


def _inproj_kernel(x_ref, sh_ref, sc_ref, wm_ref, ws_ref, pm_ref, ps_ref, u_ref):
    @pl.when(pl.program_id(1) == 0)
    def _():
        u = _layer_norm_rows(x_ref[...]) * (1.0 + sc_ref[0]) + sh_ref[0]
        ub = u.astype(BF16)
        u_ref[...] = ub
        ps_ref[...] = jnp.dot(ub, ws_ref[...], preferred_element_type=F32)

    pm_ref[...] = jnp.dot(u_ref[...], wm_ref[...], preferred_element_type=F32).astype(BF16)


def _inproj(x, shift, scale, w_main, w_small, seg):
    t, d = x.shape
    n = w_main.shape[1]
    ns = w_small.shape[1]
    tm = min(INPROJ_TM, seg)
    tn = min(INPROJ_TN, n)
    seg_of = lambda i, j: ((i * tm) // seg, 0, 0)
    return pl.pallas_call(
        _inproj_kernel,
        grid=(t // tm, n // tn),
        in_specs=[
            pl.BlockSpec((tm, d), lambda i, j: (i, 0)),
            pl.BlockSpec((1, 1, d), seg_of),
            pl.BlockSpec((1, 1, d), seg_of),
            pl.BlockSpec((d, tn), lambda i, j: (0, j)),
            pl.BlockSpec((d, ns), lambda i, j: (0, 0)),
        ],
        out_specs=[
            pl.BlockSpec((tm, tn), lambda i, j: (i, j)),
            pl.BlockSpec((tm, ns), lambda i, j: (i, 0)),
        ],
        out_shape=[
            jax.ShapeDtypeStruct((t, n), BF16),
            jax.ShapeDtypeStruct((t, ns), F32),
        ],
        scratch_shapes=[pltpu.VMEM((tm, d), BF16)],
        compiler_params=_cparams(2),
        name="inproj",
    )(x, shift, scale, w_main, w_small)


def _rope_lanes(x, cs, sn):
    return x * cs + pltpu.roll(x, LANES // 2, axis=1) * sn


def _qkv_kernel(ql_ref, kvl_ref, ps_ref, cs_ref, sn_ref, qnw_ref, kvnw_ref, wq_ref, wk_ref, wv_ref,
                q_out, k_out, vt_out, *, heads, qk_scale):
    def rms(v, w):
        return (v * lax.rsqrt(jnp.mean(v * v, axis=-1, keepdims=True) + RMS_EPS) * w).astype(BF16)

    qn = rms(ql_ref[...].astype(F32), qnw_ref[...])
    kvn = rms(kvl_ref[...].astype(F32), kvnw_ref[...])
    cs = cs_ref[...]
    sn = sn_ref[...]
    kr = _rope_lanes(ps_ref[...], cs, sn).astype(BF16)
    tm = kr.shape[0]
    sub = lax.broadcasted_iota(jnp.int32, (VT_ROWS - MLA_V_DIM, tm), 0)
    ones_row = jnp.where(sub == 0, 1.0, 0.0).astype(BF16)
    for h in range(heads):
        qa = jnp.dot(qn, wq_ref[:, h * HEAD_LANES:(h + 1) * HEAD_LANES], preferred_element_type=F32)
        q_out[h, :, :LANES] = (qa[:, :LANES] * qk_scale).astype(BF16)
        q_out[h, :, LANES:] = (_rope_lanes(qa[:, LANES:], cs, sn) * qk_scale).astype(BF16)
        ka = jnp.dot(kvn, wk_ref[:, h * LANES:(h + 1) * LANES], preferred_element_type=F32)
        k_out[h, :, :LANES] = ka.astype(BF16)
        k_out[h, :, LANES:] = kr
        va = jnp.dot(kvn, wv_ref[:, h * LANES:(h + 1) * LANES], preferred_element_type=F32)
        vt_out[h, 0, :MLA_V_DIM, :] = va.T.astype(BF16)
        vt_out[h, 0, MLA_V_DIM:, :] = ones_row


def _qkv(pm, ps, cs_tab, sn_tab, qnw, kvnw, wq, wk, wv, *, t_prompt, dec_seq, q_rank, kv_rank, tkv):
    t = pm.shape[0]
    heads = MLA_HEADS
    tm = min(QKV_TM, tkv)
    assert q_rank == kv_rank and q_rank % LANES == 0 and tkv % tm == 0 and dec_seq % tkv == 0
    per_kv = tkv // tm
    np_blocks = t_prompt // tm
    dec_blocks = dec_seq // tm

    def pos_block(i):
        return (jnp.where(i < np_blocks, i, (i - np_blocks) % dec_blocks), 0)

    out_sds = jax.ShapeDtypeStruct((heads, t, HEAD_LANES), BF16)
    out_spec = pl.BlockSpec((heads, tm, HEAD_LANES), lambda i: (0, i, 0))
    kern = functools.partial(_qkv_kernel, heads=heads, qk_scale=MLA_QK_DIM ** -0.5 * LOG2_E)
    return pl.pallas_call(
        kern,
        grid=(t // tm,),
        in_specs=[
            pl.BlockSpec((tm, q_rank), lambda i: (i, 0)),
            pl.BlockSpec((tm, kv_rank), lambda i: (i, 1)),
            pl.BlockSpec((tm, LANES), lambda i: (i, 0)),
            pl.BlockSpec((tm, LANES), pos_block),
            pl.BlockSpec((tm, LANES), pos_block),
            pl.BlockSpec((1, q_rank), lambda i: (0, 0)),
            pl.BlockSpec((1, kv_rank), lambda i: (0, 0)),
            pl.BlockSpec(wq.shape, lambda i: (0, 0)),
            pl.BlockSpec(wk.shape, lambda i: (0, 0)),
            pl.BlockSpec(wv.shape, lambda i: (0, 0)),
        ],
        out_specs=[out_spec, out_spec,
                   pl.BlockSpec((heads, 1, VT_ROWS, tm), lambda i: (0, i // per_kv, 0, i % per_kv))],
        out_shape=[out_sds, out_sds, jax.ShapeDtypeStruct((heads, t // tkv, VT_ROWS, tkv), BF16)],
        compiler_params=_cparams(1),
        name="qkv_up",
    )(pm, pm, ps, cs_tab, sn_tab, qnw, kvnw, wq, wk, wv)


def _attn_kernel(q_ref, k_ref, vt_ref, zb_ref, o_ref, s0_ref, s1_ref, p0_ref, p1_ref, a0_ref, a1_ref,
                 m_ref, acc_ref, *, tk, nk, cb, cb_pv):
    q = q_ref[0]
    tq = q.shape[0]

    def scores(j, s_ref):
        start = pl.multiple_of(j * tk, tk)
        s_ref[...] = lax.dot_general(k_ref[0, pl.ds(start, tk), :], q, (((1,), (1,)), ((), ())),
                                     preferred_element_type=F32)

    def softmax(s_ref, p_ref, a_ref):
        for c in range(tq // cb):
            cols = slice(c * cb, (c + 1) * cb)
            t = s_ref[:, cols]
            m_prev = m_ref[:, cols]
            m_new = jnp.maximum(m_prev, jnp.max(t, axis=0, keepdims=True))
            p_ref[:, cols] = jnp.exp2(t - m_new[0:1, :]).astype(BF16)
            a_ref[:, cols] = jnp.exp2(m_prev - m_new)
            m_ref[:, cols] = m_new

    def pv(j, p_ref, a_ref):
        vt = vt_ref[0, j]
        for c in range(tq // cb_pv):
            cols = slice(c * cb_pv, (c + 1) * cb_pv)
            upd = jnp.dot(vt, p_ref[:, cols], preferred_element_type=F32)
            acc_ref[:, cols] = acc_ref[:, cols] * a_ref[0:1, cols] + upd

    m_ref[...] = jnp.full(m_ref.shape, -jnp.inf, F32)
    acc_ref[...] = jnp.zeros(acc_ref.shape, F32)
    scores(0, s0_ref)
    scores(1, s1_ref)
    softmax(s0_ref, p0_ref, a0_ref)

    def body(jj, carry):
        j = 2 * jj
        scores(j + 2, s0_ref)
        softmax(s1_ref, p1_ref, a1_ref)
        pv(j, p0_ref, a0_ref)
        scores(j + 3, s1_ref)
        softmax(s0_ref, p0_ref, a0_ref)
        pv(j + 1, p1_ref, a1_ref)
        return carry

    lax.fori_loop(0, nk // 2 - 1, body, 0)
    softmax(s1_ref, p1_ref, a1_ref)
    pv(nk - 2, p0_ref, a0_ref)
    pv(nk - 1, p1_ref, a1_ref)
    o = (acc_ref[:MLA_V_DIM, :] / acc_ref[MLA_V_DIM:MLA_V_DIM + 1, :]).T
    o_ref[...] = (o * _silu(zb_ref[...].astype(F32))).astype(BF16)


def _with_carried_outputs(kern, n_in, prev):
    if prev is None:
        return kern, [], {}, ()

    def body(*refs):
        return kern(*refs[:n_in], *refs[n_in + len(prev):])

    specs = [pl.BlockSpec(memory_space=pl.ANY)] * len(prev)
    return body, specs, {n_in + k: k for k in range(len(prev))}, tuple(prev)


def _attention(qf, kf, vtf, pm, prev, *, t_total, row0, n_seq, seq, zb_col0):
    heads = qf.shape[0]
    tk = vtf.shape[3]
    tq = min(ATTN_TQ, seq)
    nq = seq // tq
    qb0 = row0 // tq
    sb0 = row0 // seq
    zc0 = zb_col0 // LANES
    nk = seq // tk
    assert nk % 2 == 0
    kern = functools.partial(_attn_kernel, tk=tk, nk=nk, cb=min(ATTN_CB, tq), cb_pv=min(ATTN_CB_PV, tq))
    kern, prev_specs, aliases, prev = _with_carried_outputs(kern, 4, prev)
    return pl.pallas_call(
        kern,
        grid=(n_seq, heads, nq),
        in_specs=[
            pl.BlockSpec((1, tq, HEAD_LANES), lambda n, h, i: (h, qb0 + n * nq + i, 0)),
            pl.BlockSpec((1, seq, HEAD_LANES), lambda n, h, i: (h, sb0 + n, 0)),
            pl.BlockSpec((1, nk, VT_ROWS, tk), lambda n, h, i: (h, sb0 + n, 0, 0)),
            pl.BlockSpec((tq, LANES), lambda n, h, i: (qb0 + n * nq + i, zc0 + h)),
        ] + prev_specs,
        out_specs=pl.BlockSpec((tq, LANES), lambda n, h, i: (qb0 + n * nq + i, h)),
        out_shape=jax.ShapeDtypeStruct((t_total, heads * LANES), BF16),
        input_output_aliases=aliases,
        scratch_shapes=[
            pltpu.VMEM((tk, tq), F32),
            pltpu.VMEM((tk, tq), F32),
            pltpu.VMEM((tk, tq), BF16),
            pltpu.VMEM((tk, tq), BF16),
            pltpu.VMEM((8, tq), F32),
            pltpu.VMEM((8, tq), F32),
            pltpu.VMEM((8, tq), F32),
            pltpu.VMEM((VT_ROWS, tq), F32),
        ],
        compiler_params=_cparams(3),
        name="mla_attention",
    )(qf, kf, vtf, pm, *prev)


def _mlstm_kernel(qf_ref, kf_ref, vf_ref, gf_ref, qb_ref, kb_ref, vb_ref, gb_ref, bif_ref,
                  hf_ref, hb_ref, st_ref, m_ref, *, heads, k_scale):
    L = ML_CHUNK
    dh = ML_HEAD_DIM

    @pl.when(pl.program_id(1) == 0)
    def _():
        st_ref[...] = jnp.zeros_like(st_ref)
        m_ref[...] = jnp.zeros_like(m_ref)

    row = lax.broadcasted_iota(jnp.int32, (L, L), 0)
    col = lax.broadcasted_iota(jnp.int32, (L, L), 1)
    lane = lax.broadcasted_iota(jnp.int32, (L, LANES), 1)
    ones_col = jnp.where(lane == 0, 1.0, 0.0).astype(BF16)

    dirs = ((qf_ref, kf_ref, vf_ref, gf_ref, hf_ref, col <= row, L - 1),
            (qb_ref, kb_ref, vb_ref, gb_ref, hb_ref, col >= row, 0))
    for d, (q_ref, k_ref, v_ref, g_ref, h_ref, mask, last) in enumerate(dirs):
        g = g_ref[...] + bif_ref[...]
        lf = _log_sigmoid(g)
        b_all = jnp.dot(mask.astype(F32), lf, preferred_element_type=F32,
                        precision=lax.Precision.HIGHEST)
        g_t = g.T
        b_t = b_all.T
        for h in range(heads):
            ci = 2 * heads * d + h
            cf = ci + heads
            idx = d * heads + h
            bcol = b_all[:, cf:cf + 1]
            brow = b_t[cf:cf + 1, :]
            icol = g[:, ci:ci + 1]
            irow = g_t[ci:ci + 1, :]
            m_prev = m_ref[idx][:, 0:1]
            logd = jnp.where(mask, bcol - brow + irow, -jnp.inf)
            inter = bcol + m_prev
            m_t = jnp.maximum(inter, jnp.max(logd, axis=1, keepdims=True))
            dmat = jnp.exp(logd - m_t)
            w_inter = jnp.exp(inter - m_t)
            q = q_ref[:, h * dh:(h + 1) * dh]
            k = k_ref[:, h * dh:(h + 1) * dh]
            v_aug = jnp.concatenate([v_ref[:, h * dh:(h + 1) * dh], ones_col], axis=1)
            s = lax.dot_general(q, k, (((1,), (1,)), ((), ())), preferred_element_type=F32)
            s = (s * k_scale * dmat).astype(BF16)
            st_prev = st_ref[idx]
            nd = (jnp.dot(s, v_aug, preferred_element_type=F32)
                  + w_inter * jnp.dot(q, st_prev.astype(BF16), preferred_element_type=F32))
            den = nd[:, dh:dh + 1]
            hout = nd[:, :dh] / jnp.maximum(jnp.abs(den), jnp.exp(-m_t))
            h_ref[:, h * dh:(h + 1) * dh] = hout.astype(h_ref.dtype)
            m_new = m_t[last:last + 1, :]
            b_last = bcol[last:last + 1, :]
            wcol = jnp.exp(b_last - bcol + icol - m_new) * k_scale
            decay = jnp.exp(b_last + m_prev - m_new)
            wv = (v_aug.astype(F32) * wcol).astype(BF16)
            upd = lax.dot_general(k, wv, (((0,), (0,)), ((), ())), preferred_element_type=F32)
            st_ref[idx] = decay * st_prev + upd
            m_ref[idx] = jnp.broadcast_to(m_new, (1, LANES))


def _mlstm(pm, ps, bif, prev, *, t_total, row0, n_seq, seq, q_col0):
    heads = ML_HEADS
    width = heads * ML_HEAD_DIM
    L = ML_CHUNK
    nc = seq // L
    rb0 = row0 // L
    qc = q_col0 // width
    fwd = lambda n, c: rb0 + n * nc + c
    bwd = lambda n, c: rb0 + n * nc + (nc - 1 - c)

    def col_spec(chunk_of, cb):
        return pl.BlockSpec((L, width), lambda n, c: (chunk_of(n, c), cb))

    gate_f = pl.BlockSpec((L, LANES), lambda n, c: (fwd(n, c), 1))
    gate_b = pl.BlockSpec((L, LANES), lambda n, c: (bwd(n, c), 1))
    out_sds = jax.ShapeDtypeStruct((t_total, width), BF16)
    kern = functools.partial(_mlstm_kernel, heads=heads, k_scale=ML_HEAD_DIM ** -0.5)
    kern, prev_specs, aliases, prev = _with_carried_outputs(kern, 9, prev)
    return pl.pallas_call(
        kern,
        grid=(n_seq, nc),
        in_specs=[
            col_spec(fwd, qc), col_spec(fwd, qc + 1), col_spec(fwd, qc + 2), gate_f,
            col_spec(bwd, qc), col_spec(bwd, qc + 1), col_spec(bwd, qc + 2), gate_b,
            pl.BlockSpec((1, LANES), lambda n, c: (0, 0)),
        ] + prev_specs,
        out_specs=[
            pl.BlockSpec((L, width), lambda n, c: (fwd(n, c), 0)),
            pl.BlockSpec((L, width), lambda n, c: (bwd(n, c), 0)),
        ],
        out_shape=[out_sds, out_sds],
        input_output_aliases=aliases,
        scratch_shapes=[
            pltpu.VMEM((2 * heads, ML_HEAD_DIM, HEAD_LANES), F32),
            pltpu.VMEM((2 * heads, 1, LANES), F32),
        ],
        compiler_params=_cparams(2),
        name="mlstm_bidir",
    )(pm, pm, pm, ps, pm, pm, pm, ps, bif, *prev)


def _tail_kernel(x_ref, attn_ref, hf_ref, hb_ref, o_ref, za_ref, ga_ref, gb_ref, gate_ref,
                 mlw_ref, wpa_ref, wpb_ref, wo_ref, lnw_ref, lnb_ref, y_ref, *, heads, alpha):
    dh = ML_HEAD_DIM
    hs = hf_ref[...].astype(F32) + hb_ref[...].astype(F32)
    hn = jnp.concatenate(
        [_layer_norm_rows(hs[:, h * dh:(h + 1) * dh]) for h in range(heads)], axis=1)
    a_in = hn * mlw_ref[...] * _sigmoid(o_ref[...].astype(F32)) * _silu(za_ref[...].astype(F32))
    y_a = jnp.dot(a_in.astype(BF16), wpa_ref[...], preferred_element_type=F32)
    y_b = jnp.dot(attn_ref[...], wpb_ref[...], preferred_element_type=F32)
    merged = (_sigmoid(ga_ref[...].astype(F32)) * y_a + _sigmoid(gb_ref[...].astype(F32)) * y_b)
    out = jnp.dot(merged.astype(BF16), wo_ref[...], preferred_element_type=F32)
    r = alpha * x_ref[...] + gate_ref[0] * out
    y_ref[...] = _layer_norm_rows(r) * lnw_ref[...] + lnb_ref[...]


def _tail(x, attn, hf, hb, pm, gate, mlw, wpa, wpb, wo, lnw, lnb, *, seg, alpha, o_col0, za_col0, g_col0):
    t, d = x.shape
    wa = hf.shape[1]
    tm = min(TAIL_TM, seg)
    const = lambda shape: pl.BlockSpec(shape, lambda i: (0,) * len(shape), pipeline_mode=pl.Buffered(1))
    kern = functools.partial(_tail_kernel, heads=ML_HEADS, alpha=alpha)
    return pl.pallas_call(
        kern,
        grid=(t // tm,),
        in_specs=[
            pl.BlockSpec((tm, d), lambda i: (i, 0)),
            pl.BlockSpec((tm, attn.shape[1]), lambda i: (i, 0)),
            pl.BlockSpec((tm, wa), lambda i: (i, 0)),
            pl.BlockSpec((tm, wa), lambda i: (i, 0)),
            pl.BlockSpec((tm, wa), lambda i: (i, o_col0 // wa)),
            pl.BlockSpec((tm, wa), lambda i: (i, za_col0 // wa)),
            pl.BlockSpec((tm, d), lambda i: (i, g_col0 // d)),
            pl.BlockSpec((tm, d), lambda i: (i, g_col0 // d + 1)),
            pl.BlockSpec((1, 1, d), lambda i: ((i * tm) // seg, 0, 0)),
            const((1, wa)),
            const(wpa.shape),
            const(wpb.shape),
            const(wo.shape),
            const((1, d)),
            const((1, d)),
        ],
        out_specs=pl.BlockSpec((tm, d), lambda i: (i, 0)),
        out_shape=jax.ShapeDtypeStruct((t, d), F32),
        compiler_params=_cparams(1),
        name="tail",
    )(x, attn, hf, hb, pm, pm, pm, pm, gate, mlw, wpa, wpb, wo, lnw, lnb)


def _rot_half(w):
    half = w.shape[-1] // 2
    return jnp.concatenate([-w[..., half:], w[..., :half]], axis=-1)


def _prep_weights(w_in, b_if, w_q_b, w_kv_b, w_proj_a, w_proj_b, w_out, d_model, q_rank, kv_rank):
    depth = w_in.shape[0]
    wa = ML_HEADS * ML_HEAD_DIM
    wb = MLA_HEADS * MLA_V_DIM
    sizes = [q_rank, kv_rank, MLA_ROPE_DIM, wb, wa, wa, wa, wa, wa, 4 * ML_HEADS, 2 * d_model]
    offs = [0]
    for s in sizes:
        offs.append(offs[-1] + s)
    part = lambda i: w_in[:, :, offs[i]:offs[i + 1]]
    (q_lat, kv_lat, k_rope, z_b, ml_q, ml_k, ml_v, ml_o, z_a, ml_g, merge_g) = [part(i) for i in range(11)]
    main_parts = [q_lat, kv_lat, z_b, ml_q, ml_k, ml_v, ml_o, z_a, merge_g]
    w_main = jnp.concatenate(main_parts, axis=-1).astype(BF16)
    names = ["q_lat", "kv_lat", "z_b", "ml_q", "ml_k", "ml_v", "ml_o", "z_a", "merge_g"]
    cols, o = {}, 0
    for nme, p in zip(names, main_parts):
        cols[nme] = o
        o += p.shape[-1]
    pad = jnp.zeros(w_in.shape[:2] + (LANES - 4 * ML_HEADS,), w_in.dtype)
    w_small = jnp.concatenate([k_rope, _rot_half(k_rope), ml_g, pad], axis=-1).astype(BF16)

    wq = w_q_b.reshape(depth, q_rank, MLA_HEADS, MLA_QK_DIM)
    wq_r = wq[..., MLA_NOPE_DIM:]
    wq = jnp.concatenate([wq[..., :MLA_NOPE_DIM], wq_r, _rot_half(wq_r)], axis=-1)
    wq = wq.reshape(depth, q_rank, MLA_HEADS * HEAD_LANES).astype(BF16)
    wkv = w_kv_b.reshape(depth, kv_rank, MLA_HEADS, MLA_NOPE_DIM + MLA_V_DIM)
    wk = wkv[..., :MLA_NOPE_DIM].reshape(depth, kv_rank, MLA_HEADS * MLA_NOPE_DIM).astype(BF16)
    wv = wkv[..., MLA_NOPE_DIM:].reshape(depth, kv_rank, MLA_HEADS * MLA_V_DIM).astype(BF16)
    bif = jnp.pad(b_if.reshape(depth, 1, 4 * ML_HEADS), ((0, 0), (0, 0), (0, LANES - 4 * ML_HEADS)))
    return dict(w_main=w_main, w_small=w_small, cols=cols, wq=wq, wk=wk, wv=wv, bif=bif,
                wpa=w_proj_a.astype(BF16), wpb=w_proj_b.astype(BF16), wo=w_out.astype(BF16))


def _rope_lane_tables(seq_len):
    inv_freq = ROPE_THETA ** (-jnp.arange(0, MLA_ROPE_DIM, 2, dtype=F32) / MLA_ROPE_DIM)
    ang = jnp.arange(seq_len, dtype=F32)[:, None] * inv_freq[None, :]
    zeros = jnp.zeros((seq_len, LANES - MLA_ROPE_DIM), F32)
    cos, sin = jnp.cos(ang), jnp.sin(ang)
    return (jnp.concatenate([cos, cos, zeros], axis=1), jnp.concatenate([sin, sin, zeros], axis=1))


def kernel(x_prompt, x_sample, c_prompt, c_sample, w_ada, b_ada, w_in, b_if, q_norm_w, kv_norm_w,
           w_q_b, w_kv_b, ml_norm_w, w_proj_a, w_proj_b, w_out, ln_w, ln_b):
    batch, seq, d = x_prompt.shape
    dec_batch, dec_seq, _ = x_sample.shape
    depth = w_ada.shape[0]
    q_rank = q_norm_w.shape[1]
    kv_rank = kv_norm_w.shape[1]
    alpha = (2 * depth) ** 0.25
    assert seq % dec_seq == 0 and dec_seq % ML_CHUNK == 0
    seg = dec_seq
    t_prompt = batch * seq
    t = t_prompt + dec_batch * dec_seq

    x = jnp.concatenate([x_prompt.reshape(t_prompt, d), x_sample.reshape(dec_batch * dec_seq, d)], axis=0)
    c_all = jnp.concatenate([c_prompt, c_sample], axis=0)
    n_cond = c_all.shape[0]
    rows = -(-n_cond // 16) * 16
    c_pad = jnp.pad(c_all, ((0, rows - n_cond), (0, 0)))
    seg_cond = jnp.concatenate([jnp.repeat(jnp.arange(batch), seq // seg),
                                batch + jnp.arange(dec_batch)])

    mod = _adaln_mod(c_pad, w_ada, b_ada)
    mod = mod[:, seg_cond, :].reshape(depth, t // seg, 1, 3, d)
    shift, scale, gate = mod[..., 0, :], mod[..., 1, :], mod[..., 2, :]

    w = _prep_weights(w_in, b_if, w_q_b, w_kv_b, w_proj_a, w_proj_b, w_out, d, q_rank, kv_rank)
    cols = w["cols"]
    cs_tab, sn_tab = _rope_lane_tables(max(seq, dec_seq))
    groups = ((0, batch, seq), (t_prompt, dec_batch, dec_seq))

    for l in range(depth):
        pm, ps = _inproj(x, shift[l], scale[l], w["w_main"][l], w["w_small"][l], seg)
        qf, kf, vf = _qkv(pm, ps, cs_tab, sn_tab, q_norm_w[l][None], kv_norm_w[l][None],
                          w["wq"][l], w["wk"][l], w["wv"][l],
                          t_prompt=t_prompt, dec_seq=dec_seq, q_rank=q_rank, kv_rank=kv_rank,
                          tkv=min(ATTN_TK, dec_seq))
        attn, hfb = None, None
        for row0, n_seq, s_len in groups:
            attn = [_attention(qf, kf, vf, pm, attn, t_total=t, row0=row0, n_seq=n_seq, seq=s_len,
                               zb_col0=cols["z_b"])]
            hfb = _mlstm(pm, ps, w["bif"][l], hfb, t_total=t, row0=row0, n_seq=n_seq, seq=s_len,
                         q_col0=cols["ml_q"])
        x = _tail(x, attn[0], hfb[0], hfb[1],
                  pm, gate[l], ml_norm_w[l][None], w["wpa"][l], w["wpb"][l], w["wo"][l],
                  ln_w[l][None], ln_b[l][None], seg=seg, alpha=alpha,
                  o_col0=cols["ml_o"], za_col0=cols["z_a"], g_col0=cols["merge_g"])

    y_prompt = x[:t_prompt].reshape(batch, seq, d)
    y_sample = x[t_prompt:].reshape(dec_batch, dec_seq, d)
    return (y_prompt, y_sample)
```

```python
import functools

import jax
import jax.numpy as jnp
from jax import lax
from jax.experimental import pallas as pl
from jax.experimental.pallas import tpu as pltpu

F32 = jnp.float32
BF16 = jnp.bfloat16

MLA_HEADS = 16
MLA_NOPE_DIM = 128
MLA_ROPE_DIM = 64
MLA_V_DIM = 128
MLA_QK_DIM = MLA_NOPE_DIM + MLA_ROPE_DIM
ROPE_THETA = 10000.0
ML_HEADS = 8
ML_HEAD_DIM = 128
ML_CHUNK = 128
LN_EPS = 1e-5
RMS_EPS = 1e-6
LOG2_E = 1.4426950408889634

LANES = 128
HEAD_LANES = 2 * LANES
VT_ROWS = MLA_V_DIM + 16
VMEM_LIMIT_BYTES = 56 * 1024 * 1024

MOD_TN = 1024
INPROJ_TM = 1024
INPROJ_TN = 1024
QKV_TM = 256
ATTN_TQ = 2048
ATTN_TK = 512
ATTN_CB = 128
ATTN_CB_PV = 256
TAIL_TM = 256


def _cparams(n_axes):
    return pltpu.CompilerParams(
        dimension_semantics=("arbitrary",) * n_axes,
        vmem_limit_bytes=VMEM_LIMIT_BYTES,
    )


def _sigmoid(x):
    return 1.0 / (1.0 + jnp.exp(-x))


def _silu(x):
    return x * _sigmoid(x)


def _log_sigmoid(x):
    return jnp.minimum(x, 0.0) - jnp.log(1.0 + jnp.exp(-jnp.abs(x)))


def _layer_norm_rows(x):
    mu = jnp.mean(x, axis=-1, keepdims=True)
    xc = x - mu
    var = jnp.mean(xc * xc, axis=-1, keepdims=True)
    return xc * lax.rsqrt(var + LN_EPS)


def _mod_kernel(c_ref, w_ref, b_ref, o_ref):
    c = c_ref[...]
    a = _silu(c).astype(BF16)
    o_ref[0] = jnp.dot(a, w_ref[0].astype(BF16), preferred_element_type=F32) + b_ref[0]


def _adaln_mod(c_pad, w_ada, b_ada):
    depth, d, n = w_ada.shape
    rows = c_pad.shape[0]
    tn = min(MOD_TN, n)
    return pl.pallas_call(
        _mod_kernel,
        grid=(depth, n // tn),
        in_specs=[
            pl.BlockSpec((rows, d), lambda l, j: (0, 0)),
            pl.BlockSpec((1, d, tn), lambda l, j: (l, 0, j)),
            pl.BlockSpec((1, 1, tn), lambda l, j: (l, 0, j)),
        ],
        out_specs=pl.BlockSpec((1, rows, tn), lambda l, j: (l, 0, j)),
        out_shape=jax.ShapeDtypeStruct((depth, rows, n), F32),
        compiler_params=_cparams(2),
        name="adaln_mod",
    )(c_pad, w_ada, b_ada.reshape(depth, 1, n))


def _inproj_kernel(x_ref, sh_ref, sc_ref, wm_ref, ws_ref, pm_ref, ps_ref, u_ref):
    @pl.when(pl.program_id(1) == 0)
    def _():
        u = _layer_norm_rows(x_ref[...]) * (1.0 + sc_ref[0]) + sh_ref[0]
        ub = u.astype(BF16)
        u_ref[...] = ub
        ps_ref[...] = jnp.dot(ub, ws_ref[...], preferred_element_type=F32)

    pm_ref[...] = jnp.dot(u_ref[...], wm_ref[...], preferred_element_type=F32).astype(BF16)


def _inproj(x, shift, scale, w_main, w_small, seg):
    t, d = x.shape
    n = w_main.shape[1]
    ns = w_small.shape[1]
    tm = min(INPROJ_TM, seg)
    tn = min(INPROJ_TN, n)
    seg_of = lambda i, j: ((i * tm) // seg, 0, 0)
    return pl.pallas_call(
        _inproj_kernel,
        grid=(t // tm, n // tn),
        in_specs=[
            pl.BlockSpec((tm, d), lambda i, j: (i, 0)),
            pl.BlockSpec((1, 1, d), seg_of),
            pl.BlockSpec((1, 1, d), seg_of),
            pl.BlockSpec((d, tn), lambda i, j: (0, j)),
            pl.BlockSpec((d, ns), lambda i, j: (0, 0)),
        ],
        out_specs=[
            pl.BlockSpec((tm, tn), lambda i, j: (i, j)),
            pl.BlockSpec((tm, ns), lambda i, j: (i, 0)),
        ],
        out_shape=[
            jax.ShapeDtypeStruct((t, n), BF16),
            jax.ShapeDtypeStruct((t, ns), F32),
        ],
        scratch_shapes=[pltpu.VMEM((tm, d), BF16)],
        compiler_params=_cparams(2),
        name="inproj",
    )(x, shift, scale, w_main, w_small)


def _rope_lanes(x, cs, sn):
    return x * cs + pltpu.roll(x, LANES // 2, axis=1) * sn


def _qkv_kernel(ql_ref, kvl_ref, ps_ref, cs_ref, sn_ref, qnw_ref, kvnw_ref, wq_ref, wk_ref, wv_ref,
                qt_out, k_out, vt_out, *, heads, qk_scale):
    def rms(v, w):
        return (v * lax.rsqrt(jnp.mean(v * v, axis=-1, keepdims=True) + RMS_EPS) * w).astype(BF16)

    qn = rms(ql_ref[...].astype(F32), qnw_ref[...])
    kvn = rms(kvl_ref[...].astype(F32), kvnw_ref[...])
    cs = cs_ref[...]
    sn = sn_ref[...]
    kr = _rope_lanes(ps_ref[...], cs, sn).astype(BF16)
    tm = kr.shape[0]
    sub = lax.broadcasted_iota(jnp.int32, (VT_ROWS - MLA_V_DIM, tm), 0)
    ones_row = jnp.where(sub == 0, 1.0, 0.0).astype(BF16)
    for h in range(heads):
        qa = jnp.dot(qn, wq_ref[:, h * HEAD_LANES:(h + 1) * HEAD_LANES], preferred_element_type=F32)
        qt_out[h, :LANES, :] = (qa[:, :LANES] * qk_scale).T.astype(BF16)
        qt_out[h, LANES:, :] = (_rope_lanes(qa[:, LANES:], cs, sn) * qk_scale).T.astype(BF16)
        ka = jnp.dot(kvn, wk_ref[:, h * LANES:(h + 1) * LANES], preferred_element_type=F32)
        k_out[h, :, :LANES] = ka.astype(BF16)
        k_out[h, :, LANES:] = kr
        va = jnp.dot(kvn, wv_ref[:, h * LANES:(h + 1) * LANES], preferred_element_type=F32)
        vt_out[h, 0, :MLA_V_DIM, :] = va.T.astype(BF16)
        vt_out[h, 0, MLA_V_DIM:, :] = ones_row


def _qkv(pm, ps, cs_tab, sn_tab, qnw, kvnw, wq, wk, wv, *, t_prompt, dec_seq, q_rank, kv_rank, tkv):
    t = pm.shape[0]
    heads = MLA_HEADS
    tm = min(QKV_TM, tkv)
    assert q_rank == kv_rank and q_rank % LANES == 0 and tkv % tm == 0 and dec_seq % tkv == 0
    per_kv = tkv // tm
    np_blocks = t_prompt // tm
    dec_blocks = dec_seq // tm

    def pos_block(i):
        return (jnp.where(i < np_blocks, i, (i - np_blocks) % dec_blocks), 0)

    out_sds = jax.ShapeDtypeStruct((heads, t, HEAD_LANES), BF16)
    out_spec = pl.BlockSpec((heads, tm, HEAD_LANES), lambda i: (0, i, 0))
    kern = functools.partial(_qkv_kernel, heads=heads, qk_scale=MLA_QK_DIM ** -0.5 * LOG2_E)
    return pl.pallas_call(
        kern,
        grid=(t // tm,),
        in_specs=[
            pl.BlockSpec((tm, q_rank), lambda i: (i, 0)),
            pl.BlockSpec((tm, kv_rank), lambda i: (i, 1)),
            pl.BlockSpec((tm, LANES), lambda i: (i, 0)),
            pl.BlockSpec((tm, LANES), pos_block),
            pl.BlockSpec((tm, LANES), pos_block),
            pl.BlockSpec((1, q_rank), lambda i: (0, 0)),
            pl.BlockSpec((1, kv_rank), lambda i: (0, 0)),
            pl.BlockSpec(wq.shape, lambda i: (0, 0)),
            pl.BlockSpec(wk.shape, lambda i: (0, 0)),
            pl.BlockSpec(wv.shape, lambda i: (0, 0)),
        ],
        out_specs=[pl.BlockSpec((heads, HEAD_LANES, tm), lambda i: (0, 0, i)), out_spec,
                   pl.BlockSpec((heads, 1, VT_ROWS, tm), lambda i: (0, i // per_kv, 0, i % per_kv))],
        out_shape=[jax.ShapeDtypeStruct((heads, HEAD_LANES, t), BF16), out_sds,
                   jax.ShapeDtypeStruct((heads, t // tkv, VT_ROWS, tkv), BF16)],
        compiler_params=_cparams(1),
        name="qkv_up",
    )(pm, pm, ps, cs_tab, sn_tab, qnw, kvnw, wq, wk, wv)


def _attn_kernel(q_ref, k_ref, vt_ref, zb_ref, o_ref, s0_ref, s1_ref, p0_ref, p1_ref, a0_ref, a1_ref,
                 m_ref, acc_ref, *, tk, nk, cb, cb_pv):
    qt = q_ref[0]
    tq = qt.shape[1]

    def scores(j, s_ref):
        start = pl.multiple_of(j * tk, tk)
        s_ref[...] = jnp.dot(k_ref[0, pl.ds(start, tk), :], qt, preferred_element_type=F32)

    def softmax(s_ref, p_ref, a_ref):
        for c in range(tq // cb):
            cols = slice(c * cb, (c + 1) * cb)
            t = s_ref[:, cols]
            m_prev = m_ref[:, cols]
            m_new = jnp.maximum(m_prev, jnp.max(t, axis=0, keepdims=True))
            p_ref[:, cols] = jnp.exp2(t - m_new[0:1, :]).astype(BF16)
            a_ref[:, cols] = jnp.exp2(m_prev - m_new)
            m_ref[:, cols] = m_new

    def pv(j, p_ref, a_ref):
        vt = vt_ref[0, j]
        for c in range(tq // cb_pv):
            cols = slice(c * cb_pv, (c + 1) * cb_pv)
            upd = jnp.dot(vt, p_ref[:, cols], preferred_element_type=F32)
            acc_ref[:, cols] = acc_ref[:, cols] * a_ref[0:1, cols] + upd

    m_ref[...] = jnp.full(m_ref.shape, -jnp.inf, F32)
    acc_ref[...] = jnp.zeros(acc_ref.shape, F32)
    scores(0, s0_ref)
    scores(1, s1_ref)
    softmax(s0_ref, p0_ref, a0_ref)

    def body(jj, carry):
        j = 2 * jj
        scores(j + 2, s0_ref)
        softmax(s1_ref, p1_ref, a1_ref)
        pv(j, p0_ref, a0_ref)
        scores(j + 3, s1_ref)
        softmax(s0_ref, p0_ref, a0_ref)
        pv(j + 1, p1_ref, a1_ref)
        return carry

    lax.fori_loop(0, nk // 2 - 1, body, 0)
    softmax(s1_ref, p1_ref, a1_ref)
    pv(nk - 2, p0_ref, a0_ref)
    pv(nk - 1, p1_ref, a1_ref)
    o = (acc_ref[:MLA_V_DIM, :] / acc_ref[MLA_V_DIM:MLA_V_DIM + 1, :]).T
    o_ref[...] = (o * _silu(zb_ref[...].astype(F32))).astype(BF16)


def _with_carried_outputs(kern, n_in, prev):
    if prev is None:
        return kern, [], {}, ()

    def body(*refs):
        return kern(*refs[:n_in], *refs[n_in + len(prev):])

    specs = [pl.BlockSpec(memory_space=pl.ANY)] * len(prev)
    return body, specs, {n_in + k: k for k in range(len(prev))}, tuple(prev)


def _attention(qtf, kf, vtf, pm, prev, *, t_total, row0, n_seq, seq, zb_col0):
    heads = qtf.shape[0]
    tk = vtf.shape[3]
    tq = min(ATTN_TQ, seq)
    nq = seq // tq
    qb0 = row0 // tq
    sb0 = row0 // seq
    zc0 = zb_col0 // LANES
    nk = seq // tk
    assert nk % 2 == 0
    kern = functools.partial(_attn_kernel, tk=tk, nk=nk, cb=min(ATTN_CB, tq), cb_pv=min(ATTN_CB_PV, tq))
    kern, prev_specs, aliases, prev = _with_carried_outputs(kern, 4, prev)
    return pl.pallas_call(
        kern,
        grid=(n_seq, heads, nq),
        in_specs=[
            pl.BlockSpec((1, HEAD_LANES, tq), lambda n, h, i: (h, 0, qb0 + n * nq + i)),
            pl.BlockSpec((1, seq, HEAD_LANES), lambda n, h, i: (h, sb0 + n, 0)),
            pl.BlockSpec((1, nk, VT_ROWS, tk), lambda n, h, i: (h, sb0 + n, 0, 0)),
            pl.BlockSpec((tq, LANES), lambda n, h, i: (qb0 + n * nq + i, zc0 + h)),
        ] + prev_specs,
        out_specs=pl.BlockSpec((tq, LANES), lambda n, h, i: (qb0 + n * nq + i, h)),
        out_shape=jax.ShapeDtypeStruct((t_total, heads * LANES), BF16),
        input_output_aliases=aliases,
        scratch_shapes=[
            pltpu.VMEM((tk, tq), F32),
            pltpu.VMEM((tk, tq), F32),
            pltpu.VMEM((tk, tq), BF16),
            pltpu.VMEM((tk, tq), BF16),
            pltpu.VMEM((8, tq), F32),
            pltpu.VMEM((8, tq), F32),
            pltpu.VMEM((8, tq), F32),
            pltpu.VMEM((VT_ROWS, tq), F32),
        ],
        compiler_params=_cparams(3),
        name="mla_attention",
    )(qtf, kf, vtf, pm, *prev)


def _mlstm_kernel(qf_ref, kf_ref, vf_ref, gf_ref, qb_ref, kb_ref, vb_ref, gb_ref, bif_ref,
                  hf_ref, hb_ref, st_ref, m_ref, *, heads, k_scale):
    L = ML_CHUNK
    dh = ML_HEAD_DIM

    @pl.when(pl.program_id(1) == 0)
    def _():
        st_ref[...] = jnp.zeros_like(st_ref)
        m_ref[...] = jnp.zeros_like(m_ref)

    assert L == LANES
    row = lax.broadcasted_iota(jnp.int32, (L, L), 0)
    col = lax.broadcasted_iota(jnp.int32, (L, L), 1)
    ones_blk = jnp.ones((L, LANES), BF16)

    def rep(column):
        return jnp.broadcast_to(column, (L, LANES))

    dirs = ((qf_ref, kf_ref, vf_ref, gf_ref, hf_ref, col <= row, L - 1),
            (qb_ref, kb_ref, vb_ref, gb_ref, hb_ref, col >= row, 0))
    for d, (q_ref, k_ref, v_ref, g_ref, h_ref, mask, last) in enumerate(dirs):
        g = g_ref[...] + bif_ref[...]
        lf = _log_sigmoid(g) * LOG2_E
        g = g * LOG2_E
        b_all = jnp.dot(mask.astype(F32), lf, preferred_element_type=F32,
                        precision=lax.Precision.HIGHEST)
        g_t = g.T
        b_t = b_all.T
        for h in range(heads):
            ci = 2 * heads * d + h
            cf = ci + heads
            idx = d * heads + h
            b_r = rep(b_all[:, cf:cf + 1])
            i_r = rep(g[:, ci:ci + 1])
            brow = b_t[cf:cf + 1, :]
            irow = g_t[ci:ci + 1, :]
            m_prev = m_ref[idx]
            logd = jnp.where(mask, b_r - brow + irow, -jnp.inf)
            inter = b_r + m_prev
            m_t = jnp.maximum(inter, rep(jnp.max(logd, axis=1, keepdims=True)))
            dmat = jnp.exp2(logd - m_t)
            w_inter = jnp.exp2(inter - m_t)
            q = q_ref[:, h * dh:(h + 1) * dh]
            k = k_ref[:, h * dh:(h + 1) * dh]
            v = v_ref[:, h * dh:(h + 1) * dh]
            s = lax.dot_general(q, k, (((1,), (1,)), ((), ())), preferred_element_type=F32)
            s = (s * k_scale * dmat).astype(BF16)
            st_prev = st_ref[idx]
            intra = jnp.dot(s, jnp.concatenate([v, ones_blk], axis=1), preferred_element_type=F32)
            inter_p = jnp.dot(q, st_prev.astype(BF16), preferred_element_type=F32)
            num = intra[:, :dh] + w_inter * inter_p[:, :dh]
            den = intra[:, dh:] + w_inter * inter_p[:, dh:]
            hout = num / jnp.maximum(jnp.abs(den), jnp.exp2(-m_t))
            h_ref[:, h * dh:(h + 1) * dh] = hout.astype(h_ref.dtype)
            m_new = m_t[last:last + 1, :]
            b_last = b_r[last:last + 1, :]
            w_r = jnp.exp2(b_last - b_r + i_r - m_new) * k_scale
            decay = jnp.exp2(b_last + m_prev - m_new)
            wv = jnp.concatenate([v.astype(F32) * w_r, w_r], axis=1).astype(BF16)
            upd = lax.dot_general(k, wv, (((0,), (0,)), ((), ())), preferred_element_type=F32)
            st_ref[idx] = jnp.concatenate([decay, decay], axis=1) * st_prev + upd
            m_ref[idx] = m_new


def _mlstm(pm, ps, bif, prev, *, t_total, row0, n_seq, seq, q_col0):
    heads = ML_HEADS
    width = heads * ML_HEAD_DIM
    L = ML_CHUNK
    nc = seq // L
    rb0 = row0 // L
    qc = q_col0 // width
    fwd = lambda n, c: rb0 + n * nc + c
    bwd = lambda n, c: rb0 + n * nc + (nc - 1 - c)

    def col_spec(chunk_of, cb):
        return pl.BlockSpec((L, width), lambda n, c: (chunk_of(n, c), cb))

    gate_f = pl.BlockSpec((L, LANES), lambda n, c: (fwd(n, c), 1))
    gate_b = pl.BlockSpec((L, LANES), lambda n, c: (bwd(n, c), 1))
    out_sds = jax.ShapeDtypeStruct((t_total, width), BF16)
    kern = functools.partial(_mlstm_kernel, heads=heads, k_scale=ML_HEAD_DIM ** -0.5)
    kern, prev_specs, aliases, prev = _with_carried_outputs(kern, 9, prev)
    return pl.pallas_call(
        kern,
        grid=(n_seq, nc),
        in_specs=[
            col_spec(fwd, qc), col_spec(fwd, qc + 1), col_spec(fwd, qc + 2), gate_f,
            col_spec(bwd, qc), col_spec(bwd, qc + 1), col_spec(bwd, qc + 2), gate_b,
            pl.BlockSpec((1, LANES), lambda n, c: (0, 0)),
        ] + prev_specs,
        out_specs=[
            pl.BlockSpec((L, width), lambda n, c: (fwd(n, c), 0)),
            pl.BlockSpec((L, width), lambda n, c: (bwd(n, c), 0)),
        ],
        out_shape=[out_sds, out_sds],
        input_output_aliases=aliases,
        scratch_shapes=[
            pltpu.VMEM((2 * heads, ML_HEAD_DIM, HEAD_LANES), F32),
            pltpu.VMEM((2 * heads, 1, LANES), F32),
        ],
        compiler_params=_cparams(2),
        name="mlstm_bidir",
    )(pm, pm, pm, ps, pm, pm, pm, ps, bif, *prev)


def _tail_kernel(x_ref, attn_ref, hf_ref, hb_ref, o_ref, za_ref, ga_ref, gb_ref, gate_ref,
                 mlw_ref, wpa_ref, wpb_ref, wo_ref, lnw_ref, lnb_ref, y_ref, *, heads, alpha):
    dh = ML_HEAD_DIM
    hs = hf_ref[...].astype(F32) + hb_ref[...].astype(F32)
    hn = jnp.concatenate(
        [_layer_norm_rows(hs[:, h * dh:(h + 1) * dh]) for h in range(heads)], axis=1)
    a_in = hn * mlw_ref[...] * _sigmoid(o_ref[...].astype(F32)) * _silu(za_ref[...].astype(F32))
    y_a = jnp.dot(a_in.astype(BF16), wpa_ref[...], preferred_element_type=F32)
    y_b = jnp.dot(attn_ref[...], wpb_ref[...], preferred_element_type=F32)
    merged = (_sigmoid(ga_ref[...].astype(F32)) * y_a + _sigmoid(gb_ref[...].astype(F32)) * y_b)
    out = jnp.dot(merged.astype(BF16), wo_ref[...], preferred_element_type=F32)
    r = alpha * x_ref[...] + gate_ref[0] * out
    y_ref[...] = _layer_norm_rows(r) * lnw_ref[...] + lnb_ref[...]


def _tail(x, attn, hf, hb, pm, gate, mlw, wpa, wpb, wo, lnw, lnb, *, seg, alpha, o_col0, za_col0, g_col0):
    t, d = x.shape
    wa = hf.shape[1]
    tm = min(TAIL_TM, seg)
    const = lambda shape: pl.BlockSpec(shape, lambda i: (0,) * len(shape), pipeline_mode=pl.Buffered(1))
    kern = functools.partial(_tail_kernel, heads=ML_HEADS, alpha=alpha)
    return pl.pallas_call(
        kern,
        grid=(t // tm,),
        in_specs=[
            pl.BlockSpec((tm, d), lambda i: (i, 0)),
            pl.BlockSpec((tm, attn.shape[1]), lambda i: (i, 0)),
            pl.BlockSpec((tm, wa), lambda i: (i, 0)),
            pl.BlockSpec((tm, wa), lambda i: (i, 0)),
            pl.BlockSpec((tm, wa), lambda i: (i, o_col0 // wa)),
            pl.BlockSpec((tm, wa), lambda i: (i, za_col0 // wa)),
            pl.BlockSpec((tm, d), lambda i: (i, g_col0 // d)),
            pl.BlockSpec((tm, d), lambda i: (i, g_col0 // d + 1)),
            pl.BlockSpec((1, 1, d), lambda i: ((i * tm) // seg, 0, 0)),
            const((1, wa)),
            const(wpa.shape),
            const(wpb.shape),
            const(wo.shape),
            const((1, d)),
            const((1, d)),
        ],
        out_specs=pl.BlockSpec((tm, d), lambda i: (i, 0)),
        out_shape=jax.ShapeDtypeStruct((t, d), F32),
        compiler_params=_cparams(1),
        name="tail",
    )(x, attn, hf, hb, pm, pm, pm, pm, gate, mlw, wpa, wpb, wo, lnw, lnb)


def _rot_half(w):
    half = w.shape[-1] // 2
    return jnp.concatenate([-w[..., half:], w[..., :half]], axis=-1)


def _prep_weights(w_in, b_if, w_q_b, w_kv_b, w_proj_a, w_proj_b, w_out, d_model, q_rank, kv_rank):
    depth = w_in.shape[0]
    wa = ML_HEADS * ML_HEAD_DIM
    wb = MLA_HEADS * MLA_V_DIM
    sizes = [q_rank, kv_rank, MLA_ROPE_DIM, wb, wa, wa, wa, wa, wa, 4 * ML_HEADS, 2 * d_model]
    offs = [0]
    for s in sizes:
        offs.append(offs[-1] + s)
    part = lambda i: w_in[:, :, offs[i]:offs[i + 1]]
    (q_lat, kv_lat, k_rope, z_b, ml_q, ml_k, ml_v, ml_o, z_a, ml_g, merge_g) = [part(i) for i in range(11)]
    main_parts = [q_lat, kv_lat, z_b, ml_q, ml_k, ml_v, ml_o, z_a, merge_g]
    w_main = jnp.concatenate(main_parts, axis=-1).astype(BF16)
    names = ["q_lat", "kv_lat", "z_b", "ml_q", "ml_k", "ml_v", "ml_o", "z_a", "merge_g"]
    cols, o = {}, 0
    for nme, p in zip(names, main_parts):
        cols[nme] = o
        o += p.shape[-1]
    pad = jnp.zeros(w_in.shape[:2] + (LANES - 4 * ML_HEADS,), w_in.dtype)
    w_small = jnp.concatenate([k_rope, _rot_half(k_rope), ml_g, pad], axis=-1).astype(BF16)

    wq = w_q_b.reshape(depth, q_rank, MLA_HEADS, MLA_QK_DIM)
    wq_r = wq[..., MLA_NOPE_DIM:]
    wq = jnp.concatenate([wq[..., :MLA_NOPE_DIM], wq_r, _rot_half(wq_r)], axis=-1)
    wq = wq.reshape(depth, q_rank, MLA_HEADS * HEAD_LANES).astype(BF16)
    wkv = w_kv_b.reshape(depth, kv_rank, MLA_HEADS, MLA_NOPE_DIM + MLA_V_DIM)
    wk = wkv[..., :MLA_NOPE_DIM].reshape(depth, kv_rank, MLA_HEADS * MLA_NOPE_DIM).astype(BF16)
    wv = wkv[..., MLA_NOPE_DIM:].reshape(depth, kv_rank, MLA_HEADS * MLA_V_DIM).astype(BF16)
    bif = jnp.pad(b_if.reshape(depth, 1, 4 * ML_HEADS), ((0, 0), (0, 0), (0, LANES - 4 * ML_HEADS)))
    return dict(w_main=w_main, w_small=w_small, cols=cols, wq=wq, wk=wk, wv=wv, bif=bif,
                wpa=w_proj_a.astype(BF16), wpb=w_proj_b.astype(BF16), wo=w_out.astype(BF16))


def _rope_lane_tables(seq_len):
    inv_freq = ROPE_THETA ** (-jnp.arange(0, MLA_ROPE_DIM, 2, dtype=F32) / MLA_ROPE_DIM)
    ang = jnp.arange(seq_len, dtype=F32)[:, None] * inv_freq[None, :]
    zeros = jnp.zeros((seq_len, LANES - MLA_ROPE_DIM), F32)
    cos, sin = jnp.cos(ang), jnp.sin(ang)
    return (jnp.concatenate([cos, cos, zeros], axis=1), jnp.concatenate([sin, sin, zeros], axis=1))


def kernel(x_prompt, x_sample, c_prompt, c_sample, w_ada, b_ada, w_in, b_if, q_norm_w, kv_norm_w,
           w_q_b, w_kv_b, ml_norm_w, w_proj_a, w_proj_b, w_out, ln_w, ln_b):
    batch, seq, d = x_prompt.shape
    dec_batch, dec_seq, _ = x_sample.shape
    depth = w_ada.shape[0]
    q_rank = q_norm_w.shape[1]
    kv_rank = kv_norm_w.shape[1]
    alpha = (2 * depth) ** 0.25
    assert seq % dec_seq == 0 and dec_seq % ML_CHUNK == 0
    seg = dec_seq
    t_prompt = batch * seq
    t = t_prompt + dec_batch * dec_seq

    x = jnp.concatenate([x_prompt.reshape(t_prompt, d), x_sample.reshape(dec_batch * dec_seq, d)], axis=0)
    c_all = jnp.concatenate([c_prompt, c_sample], axis=0)
    n_cond = c_all.shape[0]
    rows = -(-n_cond // 16) * 16
    c_pad = jnp.pad(c_all, ((0, rows - n_cond), (0, 0)))
    seg_cond = jnp.concatenate([jnp.repeat(jnp.arange(batch), seq // seg),
                                batch + jnp.arange(dec_batch)])

    mod = _adaln_mod(c_pad, w_ada, b_ada)
    mod = mod[:, seg_cond, :].reshape(depth, t // seg, 1, 3, d)
    shift, scale, gate = mod[..., 0, :], mod[..., 1, :], mod[..., 2, :]

    w = _prep_weights(w_in, b_if, w_q_b, w_kv_b, w_proj_a, w_proj_b, w_out, d, q_rank, kv_rank)
    cols = w["cols"]
    cs_tab, sn_tab = _rope_lane_tables(max(seq, dec_seq))
    groups = ((0, batch, seq), (t_prompt, dec_batch, dec_seq))

    for l in range(depth):
        pm, ps = _inproj(x, shift[l], scale[l], w["w_main"][l], w["w_small"][l], seg)
        qf, kf, vf = _qkv(pm, ps, cs_tab, sn_tab, q_norm_w[l][None], kv_norm_w[l][None],
                          w["wq"][l], w["wk"][l], w["wv"][l],
                          t_prompt=t_prompt, dec_seq=dec_seq, q_rank=q_rank, kv_rank=kv_rank,
                          tkv=min(ATTN_TK, dec_seq))
        attn, hfb = None, None
        for row0, n_seq, s_len in groups:
            attn = [_attention(qf, kf, vf, pm, attn, t_total=t, row0=row0, n_seq=n_seq, seq=s_len,
                               zb_col0=cols["z_b"])]
            hfb = _mlstm(pm, ps, w["bif"][l], hfb, t_total=t, row0=row0, n_seq=n_seq, seq=s_len,
                         q_col0=cols["ml_q"])
        x = _tail(x, attn[0], hfb[0], hfb[1],
                  pm, gate[l], ml_norm_w[l][None], w["wpa"][l], w["wpb"][l], w["wo"][l],
                  ln_w[l][None], ln_b[l][None], seg=seg, alpha=alpha,
                  o_col0=cols["ml_o"], za_col0=cols["z_a"], g_col0=cols["merge_g"])

    y_prompt = x[:t_prompt].reshape(batch, seq, d)
    y_sample = x[t_prompt:].reshape(dec_batch, dec_seq, d)
    return (y_prompt, y_sample)
```

```python
import functools

import jax
import jax.numpy as jnp
from jax import lax
from jax.experimental import pallas as pl
from jax.experimental.pallas import tpu as pltpu

F32 = jnp.float32
BF16 = jnp.bfloat16

MLA_HEADS = 16
MLA_NOPE_DIM = 128
MLA_ROPE_DIM = 64
MLA_V_DIM = 128
MLA_QK_DIM = MLA_NOPE_DIM + MLA_ROPE_DIM
ROPE_THETA = 10000.0
ML_HEADS = 8
ML_HEAD_DIM = 128
ML_CHUNK = 128
LN_EPS = 1e-5
RMS_EPS = 1e-6
LOG2_E = 1.4426950408889634

LANES = 128
HEAD_LANES = 2 * LANES
VT_ROWS = MLA_V_DIM + 16
VMEM_LIMIT_BYTES = 56 * 1024 * 1024

MOD_TN = 1024
INPROJ_TM = 1024
INPROJ_TN = 1024
QKV_TM = 256
ATTN_TQ = 2048
ATTN_TK = 512
ATTN_CB = 128
ATTN_CB_PV = 256
TAIL_TM = 256


def _cparams(n_axes):
    return pltpu.CompilerParams(
        dimension_semantics=("arbitrary",) * n_axes,
        vmem_limit_bytes=VMEM_LIMIT_BYTES,
    )


def _sigmoid(x):
    return 1.0 / (1.0 + jnp.exp(-x))


def _silu(x):
    return x * _sigmoid(x)


def _log_sigmoid(x):
    return jnp.minimum(x, 0.0) - jnp.log(1.0 + jnp.exp(-jnp.abs(x)))


def _layer_norm_rows(x):
    mu = jnp.mean(x, axis=-1, keepdims=True)
    xc = x - mu
    var = jnp.mean(xc * xc, axis=-1, keepdims=True)
    return xc * lax.rsqrt(var + LN_EPS)


def _mod_kernel(c_ref, w_ref, b_ref, o_ref):
    c = c_ref[...]
    a = _silu(c).astype(BF16)
    o_ref[0] = jnp.dot(a, w_ref[0].astype(BF16), preferred_element_type=F32) + b_ref[0]


def _adaln_mod(c_pad, w_ada, b_ada):
    depth, d, n = w_ada.shape
    rows = c_pad.shape[0]
    tn = min(MOD_TN, n)
    return pl.pallas_call(
        _mod_kernel,
        grid=(depth, n // tn),
        in_specs=[
            pl.BlockSpec((rows, d), lambda l, j: (0, 0)),
            pl.BlockSpec((1, d, tn), lambda l, j: (l, 0, j)),
            pl.BlockSpec((1, 1, tn), lambda l, j: (l, 0, j)),
        ],
        out_specs=pl.BlockSpec((1, rows, tn), lambda l, j: (l, 0, j)),
        out_shape=jax.ShapeDtypeStruct((depth, rows, n), F32),
        compiler_params=_cparams(2),
        name="adaln_mod",
    )(c_pad, w_ada, b_ada.reshape(depth, 1, n))


def _inproj_kernel(x_ref, sh_ref, sc_ref, wm_ref, ws_ref, pm_ref, ps_ref, u_ref):
    @pl.when(pl.program_id(1) == 0)
    def _():
        u = _layer_norm_rows(x_ref[...]) * (1.0 + sc_ref[0]) + sh_ref[0]
        ub = u.astype(BF16)
        u_ref[...] = ub
        ps_ref[...] = jnp.dot(ub, ws_ref[...], preferred_element_type=F32)

    pm_ref[...] = jnp.dot(u_ref[...], wm_ref[...], preferred_element_type=F32).astype(BF16)


def _inproj(x, shift, scale, w_main, w_small, seg):
    t, d = x.shape
    n = w_main.shape[1]
    ns = w_small.shape[1]
    tm = min(INPROJ_TM, seg)
    tn = min(INPROJ_TN, n)
    seg_of = lambda i, j: ((i * tm) // seg, 0, 0)
    return pl.pallas_call(
        _inproj_kernel,
        grid=(t // tm, n // tn),
        in_specs=[
            pl.BlockSpec((tm, d), lambda i, j: (i, 0)),
            pl.BlockSpec((1, 1, d), seg_of),
            pl.BlockSpec((1, 1, d), seg_of),
            pl.BlockSpec((d, tn), lambda i, j: (0, j)),
            pl.BlockSpec((d, ns), lambda i, j: (0, 0)),
        ],
        out_specs=[
            pl.BlockSpec((tm, tn), lambda i, j: (i, j)),
            pl.BlockSpec((tm, ns), lambda i, j: (i, 0)),
        ],
        out_shape=[
            jax.ShapeDtypeStruct((t, n), BF16),
            jax.ShapeDtypeStruct((t, ns), F32),
        ],
        scratch_shapes=[pltpu.VMEM((tm, d), BF16)],
        compiler_params=_cparams(2),
        name="inproj",
    )(x, shift, scale, w_main, w_small)


def _rope_lanes(x, cs, sn):
    return x * cs + pltpu.roll(x, LANES // 2, axis=1) * sn


def _qkv_kernel(ql_ref, kvl_ref, ps_ref, cs_ref, sn_ref, qnw_ref, kvnw_ref, wq_ref, wk_ref, wv_ref,
                qt_out, k_out, vt_out, *, heads, qk_scale):
    def rms(v, w):
        return (v * lax.rsqrt(jnp.mean(v * v, axis=-1, keepdims=True) + RMS_EPS) * w).astype(BF16)

    qn = rms(ql_ref[...].astype(F32), qnw_ref[...])
    kvn = rms(kvl_ref[...].astype(F32), kvnw_ref[...])
    cs = cs_ref[...]
    sn = sn_ref[...]
    kr = _rope_lanes(ps_ref[...], cs, sn).astype(BF16)
    tm = kr.shape[0]
    sub = lax.broadcasted_iota(jnp.int32, (VT_ROWS - MLA_V_DIM, tm), 0)
    ones_row = jnp.where(sub == 0, 1.0, 0.0).astype(BF16)
    for h in range(heads):
        qa = jnp.dot(qn, wq_ref[:, h * HEAD_LANES:(h + 1) * HEAD_LANES], preferred_element_type=F32)
        qt_out[h, :LANES, :] = (qa[:, :LANES] * qk_scale).T.astype(BF16)
        qt_out[h, LANES:, :] = (_rope_lanes(qa[:, LANES:], cs, sn) * qk_scale).T.astype(BF16)
        ka = jnp.dot(kvn, wk_ref[:, h * LANES:(h + 1) * LANES], preferred_element_type=F32)
        k_out[h, :, :LANES] = ka.astype(BF16)
        k_out[h, :, LANES:] = kr
        va = jnp.dot(kvn, wv_ref[:, h * LANES:(h + 1) * LANES], preferred_element_type=F32)
        vt_out[h, 0, :MLA_V_DIM, :] = va.T.astype(BF16)
        vt_out[h, 0, MLA_V_DIM:, :] = ones_row


def _qkv(pm, ps, cs_tab, sn_tab, qnw, kvnw, wq, wk, wv, *, t_prompt, dec_seq, q_rank, kv_rank, tkv):
    t = pm.shape[0]
    heads = MLA_HEADS
    tm = min(QKV_TM, tkv)
    assert q_rank == kv_rank and q_rank % LANES == 0 and tkv % tm == 0 and dec_seq % tkv == 0
    per_kv = tkv // tm
    np_blocks = t_prompt // tm
    dec_blocks = dec_seq // tm

    def pos_block(i):
        return (jnp.where(i < np_blocks, i, (i - np_blocks) % dec_blocks), 0)

    out_sds = jax.ShapeDtypeStruct((heads, t, HEAD_LANES), BF16)
    out_spec = pl.BlockSpec((heads, tm, HEAD_LANES), lambda i: (0, i, 0))
    kern = functools.partial(_qkv_kernel, heads=heads, qk_scale=MLA_QK_DIM ** -0.5 * LOG2_E)
    return pl.pallas_call(
        kern,
        grid=(t // tm,),
        in_specs=[
            pl.BlockSpec((tm, q_rank), lambda i: (i, 0)),
            pl.BlockSpec((tm, kv_rank), lambda i: (i, 1)),
            pl.BlockSpec((tm, LANES), lambda i: (i, 0)),
            pl.BlockSpec((tm, LANES), pos_block),
            pl.BlockSpec((tm, LANES), pos_block),
            pl.BlockSpec((1, q_rank), lambda i: (0, 0)),
            pl.BlockSpec((1, kv_rank), lambda i: (0, 0)),
            pl.BlockSpec(wq.shape, lambda i: (0, 0)),
            pl.BlockSpec(wk.shape, lambda i: (0, 0)),
            pl.BlockSpec(wv.shape, lambda i: (0, 0)),
        ],
        out_specs=[pl.BlockSpec((heads, HEAD_LANES, tm), lambda i: (0, 0, i)), out_spec,
                   pl.BlockSpec((heads, 1, VT_ROWS, tm), lambda i: (0, i // per_kv, 0, i % per_kv))],
        out_shape=[jax.ShapeDtypeStruct((heads, HEAD_LANES, t), BF16), out_sds,
                   jax.ShapeDtypeStruct((heads, t // tkv, VT_ROWS, tkv), BF16)],
        compiler_params=_cparams(1),
        name="qkv_up",
    )(pm, pm, ps, cs_tab, sn_tab, qnw, kvnw, wq, wk, wv)


def _attn_kernel(q_ref, k_ref, vt_ref, zb_ref, o_ref, s0_ref, s1_ref, x0_ref, x1_ref, p0_ref, p1_ref,
                 a0_ref, a1_ref, m_ref, acc_ref, *, tk, nk, cb, cb_pv):
    qt = q_ref[0]
    tq = qt.shape[1]

    def scores(j, s_ref, x_ref):
        start = pl.multiple_of(j * tk, tk)
        st = jnp.dot(k_ref[0, pl.ds(start, tk), :], qt, preferred_element_type=F32)
        s_ref[...] = st
        x_ref[...] = jnp.broadcast_to(jnp.max(st, axis=0, keepdims=True), x_ref.shape)

    def softmax(s_ref, x_ref, p_ref, a_ref):
        for c in range(tq // cb):
            cols = slice(c * cb, (c + 1) * cb)
            m_prev = m_ref[:, cols]
            m_new = jnp.maximum(m_prev, x_ref[:, cols])
            p_ref[:, cols] = jnp.exp2(s_ref[:, cols] - m_new[0:1, :]).astype(BF16)
            a_ref[:, cols] = jnp.exp2(m_prev - m_new)
            m_ref[:, cols] = m_new

    def pv(j, p_ref, a_ref):
        vt = vt_ref[0, j]
        for c in range(tq // cb_pv):
            cols = slice(c * cb_pv, (c + 1) * cb_pv)
            upd = jnp.dot(vt, p_ref[:, cols], preferred_element_type=F32)
            acc_ref[:, cols] = acc_ref[:, cols] * a_ref[0:1, cols] + upd

    m_ref[...] = jnp.full(m_ref.shape, -jnp.inf, F32)
    acc_ref[...] = jnp.zeros(acc_ref.shape, F32)
    scores(0, s0_ref, x0_ref)
    scores(1, s1_ref, x1_ref)
    softmax(s0_ref, x0_ref, p0_ref, a0_ref)

    def body(jj, carry):
        j = 2 * jj
        scores(j + 2, s0_ref, x0_ref)
        softmax(s1_ref, x1_ref, p1_ref, a1_ref)
        pv(j, p0_ref, a0_ref)
        scores(j + 3, s1_ref, x1_ref)
        softmax(s0_ref, x0_ref, p0_ref, a0_ref)
        pv(j + 1, p1_ref, a1_ref)
        return carry

    lax.fori_loop(0, nk // 2 - 1, body, 0)
    softmax(s1_ref, x1_ref, p1_ref, a1_ref)
    pv(nk - 2, p0_ref, a0_ref)
    pv(nk - 1, p1_ref, a1_ref)
    o = (acc_ref[:MLA_V_DIM, :] / acc_ref[MLA_V_DIM:MLA_V_DIM + 1, :]).T
    o_ref[...] = (o * _silu(zb_ref[...].astype(F32))).astype(BF16)


def _with_carried_outputs(kern, n_in, prev):
    if prev is None:
        return kern, [], {}, ()

    def body(*refs):
        return kern(*refs[:n_in], *refs[n_in + len(prev):])

    specs = [pl.BlockSpec(memory_space=pl.ANY)] * len(prev)
    return body, specs, {n_in + k: k for k in range(len(prev))}, tuple(prev)


def _attention(qtf, kf, vtf, pm, prev, *, t_total, row0, n_seq, seq, zb_col0):
    heads = qtf.shape[0]
    tk = vtf.shape[3]
    tq = min(ATTN_TQ, seq)
    nq = seq // tq
    qb0 = row0 // tq
    sb0 = row0 // seq
    zc0 = zb_col0 // LANES
    nk = seq // tk
    assert nk % 2 == 0
    kern = functools.partial(_attn_kernel, tk=tk, nk=nk, cb=min(ATTN_CB, tq), cb_pv=min(ATTN_CB_PV, tq))
    kern, prev_specs, aliases, prev = _with_carried_outputs(kern, 4, prev)
    return pl.pallas_call(
        kern,
        grid=(n_seq, heads, nq),
        in_specs=[
            pl.BlockSpec((1, HEAD_LANES, tq), lambda n, h, i: (h, 0, qb0 + n * nq + i)),
            pl.BlockSpec((1, seq, HEAD_LANES), lambda n, h, i: (h, sb0 + n, 0)),
            pl.BlockSpec((1, nk, VT_ROWS, tk), lambda n, h, i: (h, sb0 + n, 0, 0)),
            pl.BlockSpec((tq, LANES), lambda n, h, i: (qb0 + n * nq + i, zc0 + h)),
        ] + prev_specs,
        out_specs=pl.BlockSpec((tq, LANES), lambda n, h, i: (qb0 + n * nq + i, h)),
        out_shape=jax.ShapeDtypeStruct((t_total, heads * LANES), BF16),
        input_output_aliases=aliases,
        scratch_shapes=[
            pltpu.VMEM((tk, tq), F32),
            pltpu.VMEM((tk, tq), F32),
            pltpu.VMEM((8, tq), F32),
            pltpu.VMEM((8, tq), F32),
            pltpu.VMEM((tk, tq), BF16),
            pltpu.VMEM((tk, tq), BF16),
            pltpu.VMEM((8, tq), F32),
            pltpu.VMEM((8, tq), F32),
            pltpu.VMEM((8, tq), F32),
            pltpu.VMEM((VT_ROWS, tq), F32),
        ],
        compiler_params=_cparams(3),
        name="mla_attention",
    )(qtf, kf, vtf, pm, *prev)


def _mlstm_kernel(qf_ref, kf_ref, vf_ref, gf_ref, qb_ref, kb_ref, vb_ref, gb_ref, bif_ref,
                  hf_ref, hb_ref, st_ref, m_ref, *, heads, k_scale):
    L = ML_CHUNK
    dh = ML_HEAD_DIM

    @pl.when(pl.program_id(1) == 0)
    def _():
        st_ref[...] = jnp.zeros_like(st_ref)
        m_ref[...] = jnp.zeros_like(m_ref)

    assert L == LANES
    row = lax.broadcasted_iota(jnp.int32, (L, L), 0)
    col = lax.broadcasted_iota(jnp.int32, (L, L), 1)
    ones_blk = jnp.ones((L, LANES), BF16)

    def rep(column):
        return jnp.broadcast_to(column, (L, LANES))

    dirs = ((qf_ref, kf_ref, vf_ref, gf_ref, hf_ref, col <= row, L - 1),
            (qb_ref, kb_ref, vb_ref, gb_ref, hb_ref, col >= row, 0))
    for d, (q_ref, k_ref, v_ref, g_ref, h_ref, mask, last) in enumerate(dirs):
        g = g_ref[...] + bif_ref[...]
        lf = _log_sigmoid(g) * LOG2_E
        g = g * LOG2_E
        b_all = jnp.dot(mask.astype(F32), lf, preferred_element_type=F32,
                        precision=lax.Precision.HIGHEST)
        g_t = g.T
        b_t = b_all.T
        for h in range(heads):
            ci = 2 * heads * d + h
            cf = ci + heads
            idx = d * heads + h
            b_r = rep(b_all[:, cf:cf + 1])
            i_r = rep(g[:, ci:ci + 1])
            brow = b_t[cf:cf + 1, :]
            irow = g_t[ci:ci + 1, :]
            m_prev = m_ref[idx]
            logd = jnp.where(mask, b_r - brow + irow, -jnp.inf)
            inter = b_r + m_prev
            m_t = jnp.maximum(inter, rep(jnp.max(logd, axis=1, keepdims=True)))
            dmat = jnp.exp2(logd - m_t)
            w_inter = jnp.exp2(inter - m_t)
            q = q_ref[:, h * dh:(h + 1) * dh]
            k = k_ref[:, h * dh:(h + 1) * dh]
            v = v_ref[:, h * dh:(h + 1) * dh]
            s = lax.dot_general(q, k, (((1,), (1,)), ((), ())), preferred_element_type=F32)
            s = (s * k_scale * dmat).astype(BF16)
            st_prev = st_ref[idx]
            intra = jnp.dot(s, jnp.concatenate([v, ones_blk], axis=1), preferred_element_type=F32)
            inter_p = jnp.dot(q, st_prev.astype(BF16), preferred_element_type=F32)
            num = intra[:, :dh] + w_inter * inter_p[:, :dh]
            den = intra[:, dh:] + w_inter * inter_p[:, dh:]
            hout = num / jnp.maximum(jnp.abs(den), jnp.exp2(-m_t))
            h_ref[:, h * dh:(h + 1) * dh] = hout.astype(h_ref.dtype)
            m_new = m_t[last:last + 1, :]
            b_last = b_r[last:last + 1, :]
            w_r = jnp.exp2(b_last - b_r + i_r - m_new) * k_scale
            decay = jnp.exp2(b_last + m_prev - m_new)
            wv = jnp.concatenate([v.astype(F32) * w_r, w_r], axis=1).astype(BF16)
            upd = lax.dot_general(k, wv, (((0,), (0,)), ((), ())), preferred_element_type=F32)
            st_ref[idx] = jnp.concatenate([decay, decay], axis=1) * st_prev + upd
            m_ref[idx] = m_new


def _mlstm(pm, ps, bif, prev, *, t_total, row0, n_seq, seq, q_col0):
    heads = ML_HEADS
    width = heads * ML_HEAD_DIM
    L = ML_CHUNK
    nc = seq // L
    rb0 = row0 // L
    qc = q_col0 // width
    fwd = lambda n, c: rb0 + n * nc + c
    bwd = lambda n, c: rb0 + n * nc + (nc - 1 - c)

    def col_spec(chunk_of, cb):
        return pl.BlockSpec((L, width), lambda n, c: (chunk_of(n, c), cb))

    gate_f = pl.BlockSpec((L, LANES), lambda n, c: (fwd(n, c), 1))
    gate_b = pl.BlockSpec((L, LANES), lambda n, c: (bwd(n, c), 1))
    out_sds = jax.ShapeDtypeStruct((t_total, width), BF16)
    kern = functools.partial(_mlstm_kernel, heads=heads, k_scale=ML_HEAD_DIM ** -0.5)
    kern, prev_specs, aliases, prev = _with_carried_outputs(kern, 9, prev)
    return pl.pallas_call(
        kern,
        grid=(n_seq, nc),
        in_specs=[
            col_spec(fwd, qc), col_spec(fwd, qc + 1), col_spec(fwd, qc + 2), gate_f,
            col_spec(bwd, qc), col_spec(bwd, qc + 1), col_spec(bwd, qc + 2), gate_b,
            pl.BlockSpec((1, LANES), lambda n, c: (0, 0)),
        ] + prev_specs,
        out_specs=[
            pl.BlockSpec((L, width), lambda n, c: (fwd(n, c), 0)),
            pl.BlockSpec((L, width), lambda n, c: (bwd(n, c), 0)),
        ],
        out_shape=[out_sds, out_sds],
        input_output_aliases=aliases,
        scratch_shapes=[
            pltpu.VMEM((2 * heads, ML_HEAD_DIM, HEAD_LANES), F32),
            pltpu.VMEM((2 * heads, 1, LANES), F32),
        ],
        compiler_params=_cparams(2),
        name="mlstm_bidir",
    )(pm, pm, pm, ps, pm, pm, pm, ps, bif, *prev)


def _tail_kernel(x_ref, attn_ref, hf_ref, hb_ref, o_ref, za_ref, ga_ref, gb_ref, gate_ref,
                 mlw_ref, wpa_ref, wpb_ref, wo_ref, lnw_ref, lnb_ref, y_ref, *, heads, alpha):
    dh = ML_HEAD_DIM
    hs = hf_ref[...].astype(F32) + hb_ref[...].astype(F32)
    hn = jnp.concatenate(
        [_layer_norm_rows(hs[:, h * dh:(h + 1) * dh]) for h in range(heads)], axis=1)
    a_in = hn * mlw_ref[...] * _sigmoid(o_ref[...].astype(F32)) * _silu(za_ref[...].astype(F32))
    y_a = jnp.dot(a_in.astype(BF16), wpa_ref[...], preferred_element_type=F32)
    y_b = jnp.dot(attn_ref[...], wpb_ref[...], preferred_element_type=F32)
    merged = (_sigmoid(ga_ref[...].astype(F32)) * y_a + _sigmoid(gb_ref[...].astype(F32)) * y_b)
    out = jnp.dot(merged.astype(BF16), wo_ref[...], preferred_element_type=F32)
    r = alpha * x_ref[...] + gate_ref[0] * out
    y_ref[...] = _layer_norm_rows(r) * lnw_ref[...] + lnb_ref[...]


def _tail(x, attn, hf, hb, pm, gate, mlw, wpa, wpb, wo, lnw, lnb, *, seg, alpha, o_col0, za_col0, g_col0):
    t, d = x.shape
    wa = hf.shape[1]
    tm = min(TAIL_TM, seg)
    const = lambda shape: pl.BlockSpec(shape, lambda i: (0,) * len(shape), pipeline_mode=pl.Buffered(1))
    kern = functools.partial(_tail_kernel, heads=ML_HEADS, alpha=alpha)
    return pl.pallas_call(
        kern,
        grid=(t // tm,),
        in_specs=[
            pl.BlockSpec((tm, d), lambda i: (i, 0)),
            pl.BlockSpec((tm, attn.shape[1]), lambda i: (i, 0)),
            pl.BlockSpec((tm, wa), lambda i: (i, 0)),
            pl.BlockSpec((tm, wa), lambda i: (i, 0)),
            pl.BlockSpec((tm, wa), lambda i: (i, o_col0 // wa)),
            pl.BlockSpec((tm, wa), lambda i: (i, za_col0 // wa)),
            pl.BlockSpec((tm, d), lambda i: (i, g_col0 // d)),
            pl.BlockSpec((tm, d), lambda i: (i, g_col0 // d + 1)),
            pl.BlockSpec((1, 1, d), lambda i: ((i * tm) // seg, 0, 0)),
            const((1, wa)),
            const(wpa.shape),
            const(wpb.shape),
            const(wo.shape),
            const((1, d)),
            const((1, d)),
        ],
        out_specs=pl.BlockSpec((tm, d), lambda i: (i, 0)),
        out_shape=jax.ShapeDtypeStruct((t, d), F32),
        compiler_params=_cparams(1),
        name="tail",
    )(x, attn, hf, hb, pm, pm, pm, pm, gate, mlw, wpa, wpb, wo, lnw, lnb)


def _rot_half(w):
    half = w.shape[-1] // 2
    return jnp.concatenate([-w[..., half:], w[..., :half]], axis=-1)


def _prep_weights(w_in, b_if, w_q_b, w_kv_b, w_proj_a, w_proj_b, w_out, d_model, q_rank, kv_rank):
    depth = w_in.shape[0]
    wa = ML_HEADS * ML_HEAD_DIM
    wb = MLA_HEADS * MLA_V_DIM
    sizes = [q_rank, kv_rank, MLA_ROPE_DIM, wb, wa, wa, wa, wa, wa, 4 * ML_HEADS, 2 * d_model]
    offs = [0]
    for s in sizes:
        offs.append(offs[-1] + s)
    part = lambda i: w_in[:, :, offs[i]:offs[i + 1]]
    (q_lat, kv_lat, k_rope, z_b, ml_q, ml_k, ml_v, ml_o, z_a, ml_g, merge_g) = [part(i) for i in range(11)]
    main_parts = [q_lat, kv_lat, z_b, ml_q, ml_k, ml_v, ml_o, z_a, merge_g]
    w_main = jnp.concatenate(main_parts, axis=-1).astype(BF16)
    names = ["q_lat", "kv_lat", "z_b", "ml_q", "ml_k", "ml_v", "ml_o", "z_a", "merge_g"]
    cols, o = {}, 0
    for nme, p in zip(names, main_parts):
        cols[nme] = o
        o += p.shape[-1]
    pad = jnp.zeros(w_in.shape[:2] + (LANES - 4 * ML_HEADS,), w_in.dtype)
    w_small = jnp.concatenate([k_rope, _rot_half(k_rope), ml_g, pad], axis=-1).astype(BF16)

    wq = w_q_b.reshape(depth, q_rank, MLA_HEADS, MLA_QK_DIM)
    wq_r = wq[..., MLA_NOPE_DIM:]
    wq = jnp.concatenate([wq[..., :MLA_NOPE_DIM], wq_r, _rot_half(wq_r)], axis=-1)
    wq = wq.reshape(depth, q_rank, MLA_HEADS * HEAD_LANES).astype(BF16)
    wkv = w_kv_b.reshape(depth, kv_rank, MLA_HEADS, MLA_NOPE_DIM + MLA_V_DIM)
    wk = wkv[..., :MLA_NOPE_DIM].reshape(depth, kv_rank, MLA_HEADS * MLA_NOPE_DIM).astype(BF16)
    wv = wkv[..., MLA_NOPE_DIM:].reshape(depth, kv_rank, MLA_HEADS * MLA_V_DIM).astype(BF16)
    bif = jnp.pad(b_if.reshape(depth, 1, 4 * ML_HEADS), ((0, 0), (0, 0), (0, LANES - 4 * ML_HEADS)))
    return dict(w_main=w_main, w_small=w_small, cols=cols, wq=wq, wk=wk, wv=wv, bif=bif,
                wpa=w_proj_a.astype(BF16), wpb=w_proj_b.astype(BF16), wo=w_out.astype(BF16))


def _rope_lane_tables(seq_len):
    inv_freq = ROPE_THETA ** (-jnp.arange(0, MLA_ROPE_DIM, 2, dtype=F32) / MLA_ROPE_DIM)
    ang = jnp.arange(seq_len, dtype=F32)[:, None] * inv_freq[None, :]
    zeros = jnp.zeros((seq_len, LANES - MLA_ROPE_DIM), F32)
    cos, sin = jnp.cos(ang), jnp.sin(ang)
    return (jnp.concatenate([cos, cos, zeros], axis=1), jnp.concatenate([sin, sin, zeros], axis=1))


def kernel(x_prompt, x_sample, c_prompt, c_sample, w_ada, b_ada, w_in, b_if, q_norm_w, kv_norm_w,
           w_q_b, w_kv_b, ml_norm_w, w_proj_a, w_proj_b, w_out, ln_w, ln_b):
    batch, seq, d = x_prompt.shape
    dec_batch, dec_seq, _ = x_sample.shape
    depth = w_ada.shape[0]
    q_rank = q_norm_w.shape[1]
    kv_rank = kv_norm_w.shape[1]
    alpha = (2 * depth) ** 0.25
    assert seq % dec_seq == 0 and dec_seq % ML_CHUNK == 0
    seg = dec_seq
    t_prompt = batch * seq
    t = t_prompt + dec_batch * dec_seq

    x = jnp.concatenate([x_prompt.reshape(t_prompt, d), x_sample.reshape(dec_batch * dec_seq, d)], axis=0)
    c_all = jnp.concatenate([c_prompt, c_sample], axis=0)
    n_cond = c_all.shape[0]
    rows = -(-n_cond // 16) * 16
    c_pad = jnp.pad(c_all, ((0, rows - n_cond), (0, 0)))
    seg_cond = jnp.concatenate([jnp.repeat(jnp.arange(batch), seq // seg),
                                batch + jnp.arange(dec_batch)])

    mod = _adaln_mod(c_pad, w_ada, b_ada)
    mod = mod[:, seg_cond, :].reshape(depth, t // seg, 1, 3, d)
    shift, scale, gate = mod[..., 0, :], mod[..., 1, :], mod[..., 2, :]

    w = _prep_weights(w_in, b_if, w_q_b, w_kv_b, w_proj_a, w_proj_b, w_out, d, q_rank, kv_rank)
    cols = w["cols"]
    cs_tab, sn_tab = _rope_lane_tables(max(seq, dec_seq))
    groups = ((0, batch, seq), (t_prompt, dec_batch, dec_seq))

    for l in range(depth):
        pm, ps = _inproj(x, shift[l], scale[l], w["w_main"][l], w["w_small"][l], seg)
        qf, kf, vf = _qkv(pm, ps, cs_tab, sn_tab, q_norm_w[l][None], kv_norm_w[l][None],
                          w["wq"][l], w["wk"][l], w["wv"][l],
                          t_prompt=t_prompt, dec_seq=dec_seq, q_rank=q_rank, kv_rank=kv_rank,
                          tkv=min(ATTN_TK, dec_seq))
        attn, hfb = None, None
        for row0, n_seq, s_len in groups:
            attn = [_attention(qf, kf, vf, pm, attn, t_total=t, row0=row0, n_seq=n_seq, seq=s_len,
                               zb_col0=cols["z_b"])]
            hfb = _mlstm(pm, ps, w["bif"][l], hfb, t_total=t, row0=row0, n_seq=n_seq, seq=s_len,
                         q_col0=cols["ml_q"])
        x = _tail(x, attn[0], hfb[0], hfb[1],
                  pm, gate[l], ml_norm_w[l][None], w["wpa"][l], w["wpb"][l], w["wo"][l],
                  ln_w[l][None], ln_b[l][None], seg=seg, alpha=alpha,
                  o_col0=cols["ml_o"], za_col0=cols["z_a"], g_col0=cols["merge_g"])

    y_prompt = x[:t_prompt].reshape(batch, seq, d)
    y_sample = x[t_prompt:].reshape(dec_batch, dec_seq, d)
    return (y_prompt, y_sample)
```

```python
import functools

import jax
import jax.numpy as jnp
from jax import lax
from jax.experimental import pallas as pl
from jax.experimental.pallas import tpu as pltpu

F32 = jnp.float32
BF16 = jnp.bfloat16

MLA_HEADS = 16
MLA_NOPE_DIM = 128
MLA_ROPE_DIM = 64
MLA_V_DIM = 128
MLA_QK_DIM = MLA_NOPE_DIM + MLA_ROPE_DIM
ROPE_THETA = 10000.0
ML_HEADS = 8
ML_HEAD_DIM = 128
ML_CHUNK = 128
LN_EPS = 1e-5
RMS_EPS = 1e-6
LOG2_E = 1.4426950408889634

LANES = 128
HEAD_LANES = 2 * LANES
VT_ROWS = MLA_V_DIM + 16
VMEM_LIMIT_BYTES = 56 * 1024 * 1024

MOD_TN = 1024
INPROJ_TM = 1024
INPROJ_TN = 1024
QKV_TM = 256
ATTN_TQ = 2048
ATTN_TK = 512
ATTN_CB_PV = 256
TAIL_TM = 256
TAIL_SUB = 128


def _cparams(n_axes):
    return pltpu.CompilerParams(
        dimension_semantics=("arbitrary",) * n_axes,
        vmem_limit_bytes=VMEM_LIMIT_BYTES,
    )


def _sigmoid(x):
    return 1.0 / (1.0 + jnp.exp(-x))


def _silu(x):
    return x * _sigmoid(x)


def _log_sigmoid(x):
    return jnp.minimum(x, 0.0) - jnp.log(1.0 + jnp.exp(-jnp.abs(x)))


def _layer_norm_rows(x):
    mu = jnp.mean(x, axis=-1, keepdims=True)
    xc = x - mu
    var = jnp.mean(xc * xc, axis=-1, keepdims=True)
    return xc * lax.rsqrt(var + LN_EPS)


def _mod_kernel(c_ref, w_ref, b_ref, o_ref):
    c = c_ref[...]
    a = _silu(c).astype(BF16)
    o_ref[0] = jnp.dot(a, w_ref[0].astype(BF16), preferred_element_type=F32) + b_ref[0]


def _adaln_mod(c_pad, w_ada, b_ada):
    depth, d, n = w_ada.shape
    rows = c_pad.shape[0]
    tn = min(MOD_TN, n)
    return pl.pallas_call(
        _mod_kernel,
        grid=(depth, n // tn),
        in_specs=[
            pl.BlockSpec((rows, d), lambda l, j: (0, 0)),
            pl.BlockSpec((1, d, tn), lambda l, j: (l, 0, j)),
            pl.BlockSpec((1, 1, tn), lambda l, j: (l, 0, j)),
        ],
        out_specs=pl.BlockSpec((1, rows, tn), lambda l, j: (l, 0, j)),
        out_shape=jax.ShapeDtypeStruct((depth, rows, n), F32),
        compiler_params=_cparams(2),
        name="adaln_mod",
    )(c_pad, w_ada, b_ada.reshape(depth, 1, n))


def _inproj_kernel(x_ref, sh_ref, sc_ref, wm_ref, ws_ref, pm_ref, ps_ref, u_ref):
    @pl.when(pl.program_id(1) == 0)
    def _():
        u = _layer_norm_rows(x_ref[...]) * (1.0 + sc_ref[0]) + sh_ref[0]
        ub = u.astype(BF16)
        u_ref[...] = ub
        ps_ref[...] = jnp.dot(ub, ws_ref[...], preferred_element_type=F32)

    pm_ref[...] = jnp.dot(u_ref[...], wm_ref[...], preferred_element_type=F32).astype(BF16)


def _inproj(x, shift, scale, w_main, w_small, seg):
    t, d = x.shape
    n = w_main.shape[1]
    ns = w_small.shape[1]
    tm = min(INPROJ_TM, seg)
    tn = min(INPROJ_TN, n)
    seg_of = lambda i, j: ((i * tm) // seg, 0, 0)
    return pl.pallas_call(
        _inproj_kernel,
        grid=(t // tm, n // tn),
        in_specs=[
            pl.BlockSpec((tm, d), lambda i, j: (i, 0)),
            pl.BlockSpec((1, 1, d), seg_of),
            pl.BlockSpec((1, 1, d), seg_of),
            pl.BlockSpec((d, tn), lambda i, j: (0, j)),
            pl.BlockSpec((d, ns), lambda i, j: (0, 0)),
        ],
        out_specs=[
            pl.BlockSpec((tm, tn), lambda i, j: (i, j)),
            pl.BlockSpec((tm, ns), lambda i, j: (i, 0)),
        ],
        out_shape=[
            jax.ShapeDtypeStruct((t, n), BF16),
            jax.ShapeDtypeStruct((t, ns), F32),
        ],
        scratch_shapes=[pltpu.VMEM((tm, d), BF16)],
        compiler_params=_cparams(2),
        name="inproj",
    )(x, shift, scale, w_main, w_small)


def _rope_lanes(x, cs, sn):
    return x * cs + pltpu.roll(x, LANES // 2, axis=1) * sn


def _qkv_kernel(ql_ref, kvl_ref, ps_ref, cs_ref, sn_ref, qnw_ref, kvnw_ref, wq_ref, wk_ref, wv_ref,
                qt_out, k_out, vt_out, *, heads, qk_scale):
    def rms(v, w):
        return (v * lax.rsqrt(jnp.mean(v * v, axis=-1, keepdims=True) + RMS_EPS) * w).astype(BF16)

    qn = rms(ql_ref[...].astype(F32), qnw_ref[...])
    kvn = rms(kvl_ref[...].astype(F32), kvnw_ref[...])
    cs = cs_ref[...]
    sn = sn_ref[...]
    kr = _rope_lanes(ps_ref[...], cs, sn).astype(BF16)
    tm = kr.shape[0]
    sub = lax.broadcasted_iota(jnp.int32, (VT_ROWS - MLA_V_DIM, tm), 0)
    ones_row = jnp.where(sub == 0, 1.0, 0.0).astype(BF16)
    for h in range(heads):
        qa = jnp.dot(qn, wq_ref[:, h * HEAD_LANES:(h + 1) * HEAD_LANES], preferred_element_type=F32)
        qt_out[h, :LANES, :] = (qa[:, :LANES] * qk_scale).T.astype(BF16)
        qt_out[h, LANES:, :] = (_rope_lanes(qa[:, LANES:], cs, sn) * qk_scale).T.astype(BF16)
        k_out[h, :, LANES:] = kr
        vt_out[h, 0, MLA_V_DIM:, :] = ones_row
    for h in range(0, heads, 2):
        ka = jnp.dot(kvn, wk_ref[:, h * LANES:(h + 2) * LANES], preferred_element_type=F32)
        va = jnp.dot(kvn, wv_ref[:, h * LANES:(h + 2) * LANES], preferred_element_type=F32)
        for e in range(2):
            k_out[h + e, :, :LANES] = ka[:, e * LANES:(e + 1) * LANES].astype(BF16)
            vt_out[h + e, 0, :MLA_V_DIM, :] = va[:, e * LANES:(e + 1) * LANES].T.astype(BF16)


def _qkv(pm, ps, cs_tab, sn_tab, qnw, kvnw, wq, wk, wv, *, t_prompt, dec_seq, q_rank, kv_rank, tkv):
    t = pm.shape[0]
    heads = MLA_HEADS
    tm = min(QKV_TM, tkv)
    assert q_rank == kv_rank and q_rank % LANES == 0 and tkv % tm == 0 and dec_seq % tkv == 0
    per_kv = tkv // tm
    np_blocks = t_prompt // tm
    dec_blocks = dec_seq // tm

    def pos_block(i):
        return (jnp.where(i < np_blocks, i, (i - np_blocks) % dec_blocks), 0)

    out_sds = jax.ShapeDtypeStruct((heads, t, HEAD_LANES), BF16)
    out_spec = pl.BlockSpec((heads, tm, HEAD_LANES), lambda i: (0, i, 0))
    kern = functools.partial(_qkv_kernel, heads=heads, qk_scale=MLA_QK_DIM ** -0.5 * LOG2_E)
    return pl.pallas_call(
        kern,
        grid=(t // tm,),
        in_specs=[
            pl.BlockSpec((tm, q_rank), lambda i: (i, 0)),
            pl.BlockSpec((tm, kv_rank), lambda i: (i, 1)),
            pl.BlockSpec((tm, LANES), lambda i: (i, 0)),
            pl.BlockSpec((tm, LANES), pos_block),
            pl.BlockSpec((tm, LANES), pos_block),
            pl.BlockSpec((1, q_rank), lambda i: (0, 0)),
            pl.BlockSpec((1, kv_rank), lambda i: (0, 0)),
            pl.BlockSpec(wq.shape, lambda i: (0, 0)),
            pl.BlockSpec(wk.shape, lambda i: (0, 0)),
            pl.BlockSpec(wv.shape, lambda i: (0, 0)),
        ],
        out_specs=[pl.BlockSpec((heads, HEAD_LANES, tm), lambda i: (0, 0, i)), out_spec,
                   pl.BlockSpec((heads, 1, VT_ROWS, tm), lambda i: (0, i // per_kv, 0, i % per_kv))],
        out_shape=[jax.ShapeDtypeStruct((heads, HEAD_LANES, t), BF16), out_sds,
                   jax.ShapeDtypeStruct((heads, t // tkv, VT_ROWS, tkv), BF16)],
        compiler_params=_cparams(1),
        name="qkv_up",
    )(pm, pm, ps, cs_tab, sn_tab, qnw, kvnw, wq, wk, wv)


def _attn_kernel(q_ref, k_ref, vt_ref, zb_ref, o_ref, s0_ref, s1_ref, x0_ref, x1_ref, p0_ref, p1_ref,
                 a0_ref, a1_ref, m_ref, acc_ref, *, tk, nk, cb_pv):
    qt = q_ref[0]
    tq = qt.shape[1]

    def scores(j, s_ref, x_ref):
        start = pl.multiple_of(j * tk, tk)
        st = jnp.dot(k_ref[0, pl.ds(start, tk), :], qt, preferred_element_type=F32)
        for c in range(tq // LANES):
            s_ref[c] = st[:, c * LANES:(c + 1) * LANES]
        x_ref[...] = jnp.broadcast_to(jnp.max(st, axis=0, keepdims=True), x_ref.shape)

    def softmax(s_ref, x_ref, p_ref, a_ref):
        for c in range(tq // LANES):
            cols = slice(c * LANES, (c + 1) * LANES)
            m_prev = m_ref[:, cols]
            m_new = jnp.maximum(m_prev, x_ref[:, cols])
            p_ref[c] = jnp.exp2(s_ref[c] - m_new[0:1, :]).astype(BF16)
            a_ref[:, cols] = jnp.exp2(m_prev - m_new)
            m_ref[:, cols] = m_new

    def pv(j, p_ref, a_ref):
        vt = vt_ref[0, j]
        per = cb_pv // LANES
        for c in range(tq // cb_pv):
            cols = slice(c * cb_pv, (c + 1) * cb_pv)
            pt = jnp.concatenate([p_ref[c * per + i] for i in range(per)], axis=1)
            upd = jnp.dot(vt, pt, preferred_element_type=F32)
            acc_ref[:, cols] = acc_ref[:, cols] * a_ref[0:1, cols] + upd

    m_ref[...] = jnp.full(m_ref.shape, -jnp.inf, F32)
    acc_ref[...] = jnp.zeros(acc_ref.shape, F32)
    scores(0, s0_ref, x0_ref)
    scores(1, s1_ref, x1_ref)
    softmax(s0_ref, x0_ref, p0_ref, a0_ref)

    def body(jj, carry):
        j = 2 * jj
        scores(j + 2, s0_ref, x0_ref)
        softmax(s1_ref, x1_ref, p1_ref, a1_ref)
        pv(j, p0_ref, a0_ref)
        scores(j + 3, s1_ref, x1_ref)
        softmax(s0_ref, x0_ref, p0_ref, a0_ref)
        pv(j + 1, p1_ref, a1_ref)
        return carry

    lax.fori_loop(0, nk // 2 - 1, body, 0)
    softmax(s1_ref, x1_ref, p1_ref, a1_ref)
    pv(nk - 2, p0_ref, a0_ref)
    pv(nk - 1, p1_ref, a1_ref)
    o = (acc_ref[:MLA_V_DIM, :] / acc_ref[MLA_V_DIM:MLA_V_DIM + 1, :]).T
    o_ref[...] = (o * _silu(zb_ref[...].astype(F32))).astype(BF16)


def _with_carried_outputs(kern, n_in, prev):
    if prev is None:
        return kern, [], {}, ()

    def body(*refs):
        return kern(*refs[:n_in], *refs[n_in + len(prev):])

    specs = [pl.BlockSpec(memory_space=pl.ANY)] * len(prev)
    return body, specs, {n_in + k: k for k in range(len(prev))}, tuple(prev)


def _attention(qtf, kf, vtf, pm, prev, *, t_total, row0, n_seq, seq, zb_col0):
    heads = qtf.shape[0]
    tk = vtf.shape[3]
    tq = min(ATTN_TQ, seq)
    nq = seq // tq
    qb0 = row0 // tq
    sb0 = row0 // seq
    zc0 = zb_col0 // LANES
    nk = seq // tk
    assert nk % 2 == 0
    kern = functools.partial(_attn_kernel, tk=tk, nk=nk, cb_pv=min(ATTN_CB_PV, tq))
    kern, prev_specs, aliases, prev = _with_carried_outputs(kern, 4, prev)
    return pl.pallas_call(
        kern,
        grid=(n_seq, heads, nq),
        in_specs=[
            pl.BlockSpec((1, HEAD_LANES, tq), lambda n, h, i: (h, 0, qb0 + n * nq + i)),
            pl.BlockSpec((1, seq, HEAD_LANES), lambda n, h, i: (h, sb0 + n, 0)),
            pl.BlockSpec((1, nk, VT_ROWS, tk), lambda n, h, i: (h, sb0 + n, 0, 0)),
            pl.BlockSpec((tq, LANES), lambda n, h, i: (qb0 + n * nq + i, zc0 + h)),
        ] + prev_specs,
        out_specs=pl.BlockSpec((tq, LANES), lambda n, h, i: (qb0 + n * nq + i, h)),
        out_shape=jax.ShapeDtypeStruct((t_total, heads * LANES), BF16),
        input_output_aliases=aliases,
        scratch_shapes=[
            pltpu.VMEM((tq // LANES, tk, LANES), F32),
            pltpu.VMEM((tq // LANES, tk, LANES), F32),
            pltpu.VMEM((8, tq), F32),
            pltpu.VMEM((8, tq), F32),
            pltpu.VMEM((tq // LANES, tk, LANES), BF16),
            pltpu.VMEM((tq // LANES, tk, LANES), BF16),
            pltpu.VMEM((8, tq), F32),
            pltpu.VMEM((8, tq), F32),
            pltpu.VMEM((8, tq), F32),
            pltpu.VMEM((VT_ROWS, tq), F32),
        ],
        compiler_params=_cparams(3),
        name="mla_attention",
    )(qtf, kf, vtf, pm, *prev)


def _mlstm_kernel(qf_ref, kf_ref, vf_ref, gf_ref, qb_ref, kb_ref, vb_ref, gb_ref, bif_ref,
                  hf_ref, hb_ref, st_ref, m_ref, *, heads, k_scale):
    L = ML_CHUNK
    dh = ML_HEAD_DIM

    @pl.when(pl.program_id(1) == 0)
    def _():
        st_ref[...] = jnp.zeros_like(st_ref)
        m_ref[...] = jnp.zeros_like(m_ref)

    assert L == LANES
    row = lax.broadcasted_iota(jnp.int32, (L, L), 0)
    col = lax.broadcasted_iota(jnp.int32, (L, L), 1)
    ones_blk = jnp.ones((L, LANES), BF16)

    def rep(column):
        return jnp.broadcast_to(column, (L, LANES))

    dirs = ((qf_ref, kf_ref, vf_ref, gf_ref, hf_ref, col <= row, L - 1),
            (qb_ref, kb_ref, vb_ref, gb_ref, hb_ref, col >= row, 0))
    for d, (q_ref, k_ref, v_ref, g_ref, h_ref, mask, last) in enumerate(dirs):
        g = g_ref[...] + bif_ref[...]
        lf = _log_sigmoid(g) * LOG2_E
        g = g * LOG2_E
        b_all = jnp.dot(mask.astype(F32), lf, preferred_element_type=F32,
                        precision=lax.Precision.HIGHEST)
        g_t = g.T
        b_t = b_all.T
        for h in range(heads):
            ci = 2 * heads * d + h
            cf = ci + heads
            idx = d * heads + h
            b_r = rep(b_all[:, cf:cf + 1])
            i_r = rep(g[:, ci:ci + 1])
            brow = b_t[cf:cf + 1, :]
            irow = g_t[ci:ci + 1, :]
            m_prev = m_ref[idx]
            logd = jnp.where(mask, b_r - brow + irow, -jnp.inf)
            inter = b_r + m_prev
            m_t = jnp.maximum(inter, rep(jnp.max(logd, axis=1, keepdims=True)))
            dmat = jnp.exp2(logd - m_t)
            w_inter = jnp.exp2(inter - m_t)
            q = q_ref[:, h * dh:(h + 1) * dh]
            k = k_ref[:, h * dh:(h + 1) * dh]
            v = v_ref[:, h * dh:(h + 1) * dh]
            s = lax.dot_general(q, k, (((1,), (1,)), ((), ())), preferred_element_type=F32)
            s = (s * k_scale * dmat).astype(BF16)
            st_prev = st_ref[idx]
            intra = jnp.dot(s, jnp.concatenate([v, ones_blk], axis=1), preferred_element_type=F32)
            inter_p = jnp.dot(q, st_prev.astype(BF16), preferred_element_type=F32)
            num = intra[:, :dh] + w_inter * inter_p[:, :dh]
            den = intra[:, dh:] + w_inter * inter_p[:, dh:]
            hout = num / jnp.maximum(jnp.abs(den), jnp.exp2(-m_t))
            h_ref[:, h * dh:(h + 1) * dh] = hout.astype(h_ref.dtype)
            m_new = m_t[last:last + 1, :]
            b_last = b_r[last:last + 1, :]
            w_r = jnp.exp2(b_last - b_r + i_r - m_new) * k_scale
            decay = jnp.exp2(b_last + m_prev - m_new)
            wv = jnp.concatenate([v.astype(F32) * w_r, w_r], axis=1).astype(BF16)
            upd = lax.dot_general(k, wv, (((0,), (0,)), ((), ())), preferred_element_type=F32)
            st_ref[idx] = jnp.concatenate([decay, decay], axis=1) * st_prev + upd
            m_ref[idx] = m_new


def _mlstm(pm, ps, bif, prev, *, t_total, row0, n_seq, seq, q_col0):
    heads = ML_HEADS
    width = heads * ML_HEAD_DIM
    L = ML_CHUNK
    nc = seq // L
    rb0 = row0 // L
    qc = q_col0 // width
    fwd = lambda n, c: rb0 + n * nc + c
    bwd = lambda n, c: rb0 + n * nc + (nc - 1 - c)

    def col_spec(chunk_of, cb):
        return pl.BlockSpec((L, width), lambda n, c: (chunk_of(n, c), cb))

    gate_f = pl.BlockSpec((L, LANES), lambda n, c: (fwd(n, c), 1))
    gate_b = pl.BlockSpec((L, LANES), lambda n, c: (bwd(n, c), 1))
    out_sds = jax.ShapeDtypeStruct((t_total, width), BF16)
    kern = functools.partial(_mlstm_kernel, heads=heads, k_scale=ML_HEAD_DIM ** -0.5)
    kern, prev_specs, aliases, prev = _with_carried_outputs(kern, 9, prev)
    return pl.pallas_call(
        kern,
        grid=(n_seq, nc),
        in_specs=[
            col_spec(fwd, qc), col_spec(fwd, qc + 1), col_spec(fwd, qc + 2), gate_f,
            col_spec(bwd, qc), col_spec(bwd, qc + 1), col_spec(bwd, qc + 2), gate_b,
            pl.BlockSpec((1, LANES), lambda n, c: (0, 0)),
        ] + prev_specs,
        out_specs=[
            pl.BlockSpec((L, width), lambda n, c: (fwd(n, c), 0)),
            pl.BlockSpec((L, width), lambda n, c: (bwd(n, c), 0)),
        ],
        out_shape=[out_sds, out_sds],
        input_output_aliases=aliases,
        scratch_shapes=[
            pltpu.VMEM((2 * heads, ML_HEAD_DIM, HEAD_LANES), F32),
            pltpu.VMEM((2 * heads, 1, LANES), F32),
        ],
        compiler_params=_cparams(2),
        name="mlstm_bidir",
    )(pm, pm, pm, ps, pm, pm, pm, ps, bif, *prev)


def _tail_kernel(x_ref, attn_ref, hf_ref, hb_ref, o_ref, za_ref, ga_ref, gb_ref, gate_ref,
                 mlw_ref, wpa_ref, wpb_ref, wo_ref, lnw_ref, lnb_ref, y_ref, *, heads, alpha):
    dh = ML_HEAD_DIM
    tm = x_ref.shape[0]
    sub = min(TAIL_SUB, tm)
    for r0 in range(0, tm, sub):
        rows = slice(r0, r0 + sub)
        hs = hf_ref[rows, :].astype(F32) + hb_ref[rows, :].astype(F32)
        hn = jnp.concatenate(
            [_layer_norm_rows(hs[:, h * dh:(h + 1) * dh]) for h in range(heads)], axis=1)
        a_in = (hn * mlw_ref[...] * _sigmoid(o_ref[rows, :].astype(F32))
                * _silu(za_ref[rows, :].astype(F32)))
        y_a = jnp.dot(a_in.astype(BF16), wpa_ref[...], preferred_element_type=F32)
        y_b = jnp.dot(attn_ref[rows, :], wpb_ref[...], preferred_element_type=F32)
        merged = (_sigmoid(ga_ref[rows, :].astype(F32)) * y_a
                  + _sigmoid(gb_ref[rows, :].astype(F32)) * y_b)
        out = jnp.dot(merged.astype(BF16), wo_ref[...], preferred_element_type=F32)
        r = alpha * x_ref[rows, :] + gate_ref[0] * out
        y_ref[rows, :] = _layer_norm_rows(r) * lnw_ref[...] + lnb_ref[...]


def _tail(x, attn, hf, hb, pm, gate, mlw, wpa, wpb, wo, lnw, lnb, *, seg, alpha, o_col0, za_col0, g_col0):
    t, d = x.shape
    wa = hf.shape[1]
    tm = min(TAIL_TM, seg)
    const = lambda shape: pl.BlockSpec(shape, lambda i: (0,) * len(shape), pipeline_mode=pl.Buffered(1))
    kern = functools.partial(_tail_kernel, heads=ML_HEADS, alpha=alpha)
    return pl.pallas_call(
        kern,
        grid=(t // tm,),
        in_specs=[
            pl.BlockSpec((tm, d), lambda i: (i, 0)),
            pl.BlockSpec((tm, attn.shape[1]), lambda i: (i, 0)),
            pl.BlockSpec((tm, wa), lambda i: (i, 0)),
            pl.BlockSpec((tm, wa), lambda i: (i, 0)),
            pl.BlockSpec((tm, wa), lambda i: (i, o_col0 // wa)),
            pl.BlockSpec((tm, wa), lambda i: (i, za_col0 // wa)),
            pl.BlockSpec((tm, d), lambda i: (i, g_col0 // d)),
            pl.BlockSpec((tm, d), lambda i: (i, g_col0 // d + 1)),
            pl.BlockSpec((1, 1, d), lambda i: ((i * tm) // seg, 0, 0)),
            const((1, wa)),
            const(wpa.shape),
            const(wpb.shape),
            const(wo.shape),
            const((1, d)),
            const((1, d)),
        ],
        out_specs=pl.BlockSpec((tm, d), lambda i: (i, 0)),
        out_shape=jax.ShapeDtypeStruct((t, d), F32),
        compiler_params=_cparams(1),
        name="tail",
    )(x, attn, hf, hb, pm, pm, pm, pm, gate, mlw, wpa, wpb, wo, lnw, lnb)


def _rot_half(w):
    half = w.shape[-1] // 2
    return jnp.concatenate([-w[..., half:], w[..., :half]], axis=-1)


def _prep_weights(w_in, b_if, w_q_b, w_kv_b, w_proj_a, w_proj_b, w_out, d_model, q_rank, kv_rank):
    depth = w_in.shape[0]
    wa = ML_HEADS * ML_HEAD_DIM
    wb = MLA_HEADS * MLA_V_DIM
    sizes = [q_rank, kv_rank, MLA_ROPE_DIM, wb, wa, wa, wa, wa, wa, 4 * ML_HEADS, 2 * d_model]
    offs = [0]
    for s in sizes:
        offs.append(offs[-1] + s)
    part = lambda i: w_in[:, :, offs[i]:offs[i + 1]]
    (q_lat, kv_lat, k_rope, z_b, ml_q, ml_k, ml_v, ml_o, z_a, ml_g, merge_g) = [part(i) for i in range(11)]
    main_parts = [q_lat, kv_lat, z_b, ml_q, ml_k, ml_v, ml_o, z_a, merge_g]
    w_main = jnp.concatenate(main_parts, axis=-1).astype(BF16)
    names = ["q_lat", "kv_lat", "z_b", "ml_q", "ml_k", "ml_v", "ml_o", "z_a", "merge_g"]
    cols, o = {}, 0
    for nme, p in zip(names, main_parts):
        cols[nme] = o
        o += p.shape[-1]
    pad = jnp.zeros(w_in.shape[:2] + (LANES - 4 * ML_HEADS,), w_in.dtype)
    w_small = jnp.concatenate([k_rope, _rot_half(k_rope), ml_g, pad], axis=-1).astype(BF16)

    wq = w_q_b.reshape(depth, q_rank, MLA_HEADS, MLA_QK_DIM)
    wq_r = wq[..., MLA_NOPE_DIM:]
    wq = jnp.concatenate([wq[..., :MLA_NOPE_DIM], wq_r, _rot_half(wq_r)], axis=-1)
    wq = wq.reshape(depth, q_rank, MLA_HEADS * HEAD_LANES).astype(BF16)
    wkv = w_kv_b.reshape(depth, kv_rank, MLA_HEADS, MLA_NOPE_DIM + MLA_V_DIM)
    wk = wkv[..., :MLA_NOPE_DIM].reshape(depth, kv_rank, MLA_HEADS * MLA_NOPE_DIM).astype(BF16)
    wv = wkv[..., MLA_NOPE_DIM:].reshape(depth, kv_rank, MLA_HEADS * MLA_V_DIM).astype(BF16)
    bif = jnp.pad(b_if.reshape(depth, 1, 4 * ML_HEADS), ((0, 0), (0, 0), (0, LANES - 4 * ML_HEADS)))
    return dict(w_main=w_main, w_small=w_small, cols=cols, wq=wq, wk=wk, wv=wv, bif=bif,
                wpa=w_proj_a.astype(BF16), wpb=w_proj_b.astype(BF16), wo=w_out.astype(BF16))


def _rope_lane_tables(seq_len):
    inv_freq = ROPE_THETA ** (-jnp.arange(0, MLA_ROPE_DIM, 2, dtype=F32) / MLA_ROPE_DIM)
    ang = jnp.arange(seq_len, dtype=F32)[:, None] * inv_freq[None, :]
    zeros = jnp.zeros((seq_len, LANES - MLA_ROPE_DIM), F32)
    cos, sin = jnp.cos(ang), jnp.sin(ang)
    return (jnp.concatenate([cos, cos, zeros], axis=1), jnp.concatenate([sin, sin, zeros], axis=1))


def kernel(x_prompt, x_sample, c_prompt, c_sample, w_ada, b_ada, w_in, b_if, q_norm_w, kv_norm_w,
           w_q_b, w_kv_b, ml_norm_w, w_proj_a, w_proj_b, w_out, ln_w, ln_b):
    batch, seq, d = x_prompt.shape
    dec_batch, dec_seq, _ = x_sample.shape
    depth = w_ada.shape[0]
    q_rank = q_norm_w.shape[1]
    kv_rank = kv_norm_w.shape[1]
    alpha = (2 * depth) ** 0.25
    assert seq % dec_seq == 0 and dec_seq % ML_CHUNK == 0
    seg = dec_seq
    t_prompt = batch * seq
    t = t_prompt + dec_batch * dec_seq

    x = jnp.concatenate([x_prompt.reshape(t_prompt, d), x_sample.reshape(dec_batch * dec_seq, d)], axis=0)
    c_all = jnp.concatenate([c_prompt, c_sample], axis=0)
    n_cond = c_all.shape[0]
    rows = -(-n_cond // 16) * 16
    c_pad = jnp.pad(c_all, ((0, rows - n_cond), (0, 0)))
    seg_cond = jnp.concatenate([jnp.repeat(jnp.arange(batch), seq // seg),
                                batch + jnp.arange(dec_batch)])

    mod = _adaln_mod(c_pad, w_ada, b_ada)
    mod = mod[:, seg_cond, :].reshape(depth, t // seg, 1, 3, d)
    shift, scale, gate = mod[..., 0, :], mod[..., 1, :], mod[..., 2, :]

    w = _prep_weights(w_in, b_if, w_q_b, w_kv_b, w_proj_a, w_proj_b, w_out, d, q_rank, kv_rank)
    cols = w["cols"]
    cs_tab, sn_tab = _rope_lane_tables(max(seq, dec_seq))
    groups = ((0, batch, seq), (t_prompt, dec_batch, dec_seq))

    for l in range(depth):
        pm, ps = _inproj(x, shift[l], scale[l], w["w_main"][l], w["w_small"][l], seg)
        qf, kf, vf = _qkv(pm, ps, cs_tab, sn_tab, q_norm_w[l][None], kv_norm_w[l][None],
                          w["wq"][l], w["wk"][l], w["wv"][l],
                          t_prompt=t_prompt, dec_seq=dec_seq, q_rank=q_rank, kv_rank=kv_rank,
                          tkv=min(ATTN_TK, dec_seq))
        attn, hfb = None, None
        for row0, n_seq, s_len in groups:
            attn = [_attention(qf, kf, vf, pm, attn, t_total=t, row0=row0, n_seq=n_seq, seq=s_len,
                               zb_col0=cols["z_b"])]
            hfb = _mlstm(pm, ps, w["bif"][l], hfb, t_total=t, row0=row0, n_seq=n_seq, seq=s_len,
                         q_col0=cols["ml_q"])
        x = _tail(x, attn[0], hfb[0], hfb[1],
                  pm, gate[l], ml_norm_w[l][None], w["wpa"][l], w["wpb"][l], w["wo"][l],
                  ln_w[l][None], ln_b[l][None], seg=seg, alpha=alpha,
                  o_col0=cols["ml_o"], za_col0=cols["z_a"], g_col0=cols["merge_g"])

    y_prompt = x[:t_prompt].reshape(batch, seq, d)
    y_sample = x[t_prompt:].reshape(dec_batch, dec_seq, d)
    return (y_prompt, y_sample)
```

```python
import functools
import math

import jax
import jax.numpy as jnp
from jax import lax
from jax.experimental import pallas as pl
from jax.experimental.pallas import tpu as pltpu

F32 = jnp.float32
BF16 = jnp.bfloat16

MLA_HEADS = 16
MLA_NOPE_DIM = 128
MLA_ROPE_DIM = 64
MLA_V_DIM = 128
MLA_QK_DIM = MLA_NOPE_DIM + MLA_ROPE_DIM
ROPE_THETA = 10000.0
ML_HEADS = 8
ML_HEAD_DIM = 128
ML_CHUNK = 128
LN_EPS = 1e-5
RMS_EPS = 1e-6
LOG2_E = 1.4426950408889634

LANES = 128
HEAD_LANES = 2 * LANES
VT_ROWS = MLA_V_DIM + 16
VMEM_LIMIT_BYTES = 56 * 1024 * 1024

MOD_TN = 1024
INPROJ_TM = 1024
INPROJ_TN = 1024
QKV_TM = 256
ATTN_TQ = 2048
ATTN_TK = 512
ATTN_CB_PV = 256
TAIL_TM = 256
TAIL_SUB = 256


def _cparams(n_axes):
    return pltpu.CompilerParams(
        dimension_semantics=("arbitrary",) * n_axes,
        vmem_limit_bytes=VMEM_LIMIT_BYTES,
    )


def _sigmoid(x):
    return 1.0 / (1.0 + jnp.exp(-x))


def _silu(x):
    return x * _sigmoid(x)


def _log_sigmoid(x):
    return jnp.minimum(x, 0.0) - jnp.log(1.0 + jnp.exp(-jnp.abs(x)))


def _layer_norm_rows(x):
    mu = jnp.mean(x, axis=-1, keepdims=True)
    xc = x - mu
    var = jnp.mean(xc * xc, axis=-1, keepdims=True)
    return xc * lax.rsqrt(var + LN_EPS)


def _mod_kernel(c_ref, w_ref, b_ref, o_ref):
    c = c_ref[...]
    a = _silu(c).astype(BF16)
    o_ref[0] = jnp.dot(a, w_ref[0].astype(BF16), preferred_element_type=F32) + b_ref[0]


def _adaln_mod(c_pad, w_ada, b_ada):
    depth, d, n = w_ada.shape
    rows = c_pad.shape[0]
    tn = min(MOD_TN, n)
    return pl.pallas_call(
        _mod_kernel,
        grid=(depth, n // tn),
        in_specs=[
            pl.BlockSpec((rows, d), lambda l, j: (0, 0)),
            pl.BlockSpec((1, d, tn), lambda l, j: (l, 0, j)),
            pl.BlockSpec((1, 1, tn), lambda l, j: (l, 0, j)),
        ],
        out_specs=pl.BlockSpec((1, rows, tn), lambda l, j: (l, 0, j)),
        out_shape=jax.ShapeDtypeStruct((depth, rows, n), F32),
        compiler_params=_cparams(2),
        name="adaln_mod",
    )(c_pad, w_ada, b_ada.reshape(depth, 1, n))


def _inproj_kernel(x_ref, sh_ref, sc_ref, wm_ref, ws_ref, pm_ref, ps_ref, u_ref):
    @pl.when(pl.program_id(1) == 0)
    def _():
        u = _layer_norm_rows(x_ref[...]) * (1.0 + sc_ref[0]) + sh_ref[0]
        ub = u.astype(BF16)
        u_ref[...] = ub
        ps_ref[...] = jnp.dot(ub, ws_ref[...], preferred_element_type=F32)

    pm_ref[...] = jnp.dot(u_ref[...], wm_ref[...], preferred_element_type=F32).astype(BF16)


def _inproj(x, shift, scale, w_main, w_small, seg):
    t, d = x.shape
    n = w_main.shape[1]
    ns = w_small.shape[1]
    tm = min(INPROJ_TM, seg)
    tn = min(INPROJ_TN, n)
    seg_of = lambda i, j: ((i * tm) // seg, 0, 0)
    return pl.pallas_call(
        _inproj_kernel,
        grid=(t // tm, n // tn),
        in_specs=[
            pl.BlockSpec((tm, d), lambda i, j: (i, 0)),
            pl.BlockSpec((1, 1, d), seg_of),
            pl.BlockSpec((1, 1, d), seg_of),
            pl.BlockSpec((d, tn), lambda i, j: (0, j)),
            pl.BlockSpec((d, ns), lambda i, j: (0, 0)),
        ],
        out_specs=[
            pl.BlockSpec((tm, tn), lambda i, j: (i, j)),
            pl.BlockSpec((tm, ns), lambda i, j: (i, 0)),
        ],
        out_shape=[
            jax.ShapeDtypeStruct((t, n), BF16),
            jax.ShapeDtypeStruct((t, ns), F32),
        ],
        scratch_shapes=[pltpu.VMEM((tm, d), BF16)],
        compiler_params=_cparams(2),
        name="inproj",
    )(x, shift, scale, w_main, w_small)


def _rope_lanes(x, cs, sn):
    return x * cs + pltpu.roll(x, LANES // 2, axis=1) * sn


def _qkv_kernel(ql_ref, kvl_ref, ps_ref, cs_ref, sn_ref, qnw_ref, kvnw_ref, wq_ref, wk_ref, wv_ref,
                qt_out, k_out, vt_out, *, heads, qk_scale):
    def rms(v, w):
        return (v * lax.rsqrt(jnp.mean(v * v, axis=-1, keepdims=True) + RMS_EPS) * w).astype(BF16)

    qn = rms(ql_ref[...].astype(F32), qnw_ref[...])
    kvn = rms(kvl_ref[...].astype(F32), kvnw_ref[...])
    cs = cs_ref[...]
    sn = sn_ref[...]
    kr = _rope_lanes(ps_ref[...], cs, sn).astype(BF16)
    tm = kr.shape[0]
    sub = lax.broadcasted_iota(jnp.int32, (VT_ROWS - MLA_V_DIM, tm), 0)
    ones_row = jnp.where(sub == 0, 1.0, 0.0).astype(BF16)
    for h in range(heads):
        qa = jnp.dot(qn, wq_ref[:, h * HEAD_LANES:(h + 1) * HEAD_LANES], preferred_element_type=F32)
        qt_out[h, 0, :LANES, :] = (qa[:, :LANES] * qk_scale).T.astype(BF16)
        qt_out[h, 0, LANES:, :] = (_rope_lanes(qa[:, LANES:], cs, sn) * qk_scale).T.astype(BF16)
        k_out[h, :, LANES:] = kr
        vt_out[h, 0, MLA_V_DIM:, :] = ones_row
    for h in range(0, heads, 2):
        ka = jnp.dot(kvn, wk_ref[:, h * LANES:(h + 2) * LANES], preferred_element_type=F32)
        va = jnp.dot(kvn, wv_ref[:, h * LANES:(h + 2) * LANES], preferred_element_type=F32)
        for e in range(2):
            k_out[h + e, :, :LANES] = ka[:, e * LANES:(e + 1) * LANES].astype(BF16)
            vt_out[h + e, 0, :MLA_V_DIM, :] = va[:, e * LANES:(e + 1) * LANES].T.astype(BF16)


def _qkv(pm, ps, cs_tab, sn_tab, qnw, kvnw, wq, wk, wv, *, t_prompt, dec_seq, q_rank, kv_rank, tkv):
    t = pm.shape[0]
    heads = MLA_HEADS
    tm = min(QKV_TM, tkv)
    assert q_rank == kv_rank and q_rank % LANES == 0 and tkv % tm == 0 and dec_seq % tkv == 0
    per_kv = tkv // tm
    np_blocks = t_prompt // tm
    dec_blocks = dec_seq // tm

    def pos_block(i):
        return (jnp.where(i < np_blocks, i, (i - np_blocks) % dec_blocks), 0)

    out_sds = jax.ShapeDtypeStruct((heads, t, HEAD_LANES), BF16)
    out_spec = pl.BlockSpec((heads, tm, HEAD_LANES), lambda i: (0, i, 0))
    kern = functools.partial(_qkv_kernel, heads=heads, qk_scale=MLA_QK_DIM ** -0.5 * LOG2_E)
    return pl.pallas_call(
        kern,
        grid=(t // tm,),
        in_specs=[
            pl.BlockSpec((tm, q_rank), lambda i: (i, 0)),
            pl.BlockSpec((tm, kv_rank), lambda i: (i, 1)),
            pl.BlockSpec((tm, LANES), lambda i: (i, 0)),
            pl.BlockSpec((tm, LANES), pos_block),
            pl.BlockSpec((tm, LANES), pos_block),
            pl.BlockSpec((1, q_rank), lambda i: (0, 0)),
            pl.BlockSpec((1, kv_rank), lambda i: (0, 0)),
            pl.BlockSpec(wq.shape, lambda i: (0, 0)),
            pl.BlockSpec(wk.shape, lambda i: (0, 0)),
            pl.BlockSpec(wv.shape, lambda i: (0, 0)),
        ],
        out_specs=[pl.BlockSpec((heads, 1, HEAD_LANES, tm), lambda i: (0, i, 0, 0)), out_spec,
                   pl.BlockSpec((heads, 1, VT_ROWS, tm), lambda i: (0, i // per_kv, 0, i % per_kv))],
        out_shape=[jax.ShapeDtypeStruct((heads, t // tm, HEAD_LANES, tm), BF16), out_sds,
                   jax.ShapeDtypeStruct((heads, t // tkv, VT_ROWS, tkv), BF16)],
        compiler_params=_cparams(1),
        name="qkv_up",
    )(pm, pm, ps, cs_tab, sn_tab, qnw, kvnw, wq, wk, wv)


def _attn_kernel(q_ref, k_ref, vt_ref, zb_ref, o_ref, s0_ref, s1_ref, x0_ref, x1_ref, p0_ref, p1_ref,
                 a0_ref, a1_ref, m_ref, acc_ref, *, tk, nk, cb_pv):
    n_slab, _, slab = q_ref.shape[1:]
    tq = n_slab * slab
    per_slab = slab // LANES

    def scores(j, s_ref, x_ref):
        start = pl.multiple_of(j * tk, tk)
        kt = k_ref[0, pl.ds(start, tk), :]
        for b in range(n_slab):
            st = jnp.dot(kt, q_ref[0, b], preferred_element_type=F32)
            for c in range(per_slab):
                s_ref[b * per_slab + c] = st[:, c * LANES:(c + 1) * LANES]
            x_ref[:, b * slab:(b + 1) * slab] = jnp.broadcast_to(
                jnp.max(st, axis=0, keepdims=True), (x_ref.shape[0], slab))

    def softmax(s_ref, x_ref, p_ref, a_ref):
        for c in range(tq // LANES):
            cols = slice(c * LANES, (c + 1) * LANES)
            m_prev = m_ref[:, cols]
            m_new = jnp.maximum(m_prev, x_ref[:, cols])
            p_ref[c] = jnp.exp2(s_ref[c] - m_new[0:1, :]).astype(BF16)
            a_ref[:, cols] = jnp.exp2(m_prev - m_new)
            m_ref[:, cols] = m_new

    def pv(j, p_ref, a_ref):
        vt = vt_ref[0, j]
        per = cb_pv // LANES
        for c in range(tq // cb_pv):
            cols = slice(c * cb_pv, (c + 1) * cb_pv)
            pt = jnp.concatenate([p_ref[c * per + i] for i in range(per)], axis=1)
            upd = jnp.dot(vt, pt, preferred_element_type=F32)
            acc_ref[c] = acc_ref[c] * a_ref[0:1, cols] + upd

    m_ref[...] = jnp.full(m_ref.shape, -jnp.inf, F32)
    acc_ref[...] = jnp.zeros(acc_ref.shape, F32)
    scores(0, s0_ref, x0_ref)
    scores(1, s1_ref, x1_ref)
    softmax(s0_ref, x0_ref, p0_ref, a0_ref)

    def body(jj, carry):
        j = 2 * jj
        scores(j + 2, s0_ref, x0_ref)
        softmax(s1_ref, x1_ref, p1_ref, a1_ref)
        pv(j, p0_ref, a0_ref)
        scores(j + 3, s1_ref, x1_ref)
        softmax(s0_ref, x0_ref, p0_ref, a0_ref)
        pv(j + 1, p1_ref, a1_ref)
        return carry

    lax.fori_loop(0, nk // 2 - 1, body, 0)
    softmax(s1_ref, x1_ref, p1_ref, a1_ref)
    pv(nk - 2, p0_ref, a0_ref)
    pv(nk - 1, p1_ref, a1_ref)
    for c in range(tq // cb_pv):
        rows = slice(c * cb_pv, (c + 1) * cb_pv)
        o = (acc_ref[c, :MLA_V_DIM, :] / acc_ref[c, MLA_V_DIM:MLA_V_DIM + 1, :]).T
        o_ref[rows, :] = (o * _silu(zb_ref[rows, :].astype(F32))).astype(BF16)


def _with_carried_outputs(kern, n_in, prev):
    if prev is None:
        return kern, [], {}, ()

    def body(*refs):
        return kern(*refs[:n_in], *refs[n_in + len(prev):])

    specs = [pl.BlockSpec(memory_space=pl.ANY)] * len(prev)
    return body, specs, {n_in + k: k for k in range(len(prev))}, tuple(prev)


def _attention(qtf, kf, vtf, pm, prev, *, t_total, row0, n_seq, seq, zb_col0):
    heads, _, _, slab = qtf.shape
    tk = vtf.shape[3]
    tq = min(ATTN_TQ, seq)
    cb_pv = min(ATTN_CB_PV, tq)
    assert tq % slab == 0 and slab % LANES == 0
    nq = seq // tq
    qb0 = row0 // tq
    sb0 = row0 // seq
    zc0 = zb_col0 // LANES
    nk = seq // tk
    assert nk % 2 == 0
    kern = functools.partial(_attn_kernel, tk=tk, nk=nk, cb_pv=cb_pv)
    kern, prev_specs, aliases, prev = _with_carried_outputs(kern, 4, prev)
    return pl.pallas_call(
        kern,
        grid=(n_seq, heads, nq),
        in_specs=[
            pl.BlockSpec((1, tq // slab, HEAD_LANES, slab), lambda n, h, i: (h, qb0 + n * nq + i, 0, 0)),
            pl.BlockSpec((1, seq, HEAD_LANES), lambda n, h, i: (h, sb0 + n, 0)),
            pl.BlockSpec((1, nk, VT_ROWS, tk), lambda n, h, i: (h, sb0 + n, 0, 0)),
            pl.BlockSpec((tq, LANES), lambda n, h, i: (qb0 + n * nq + i, zc0 + h)),
        ] + prev_specs,
        out_specs=pl.BlockSpec((tq, LANES), lambda n, h, i: (qb0 + n * nq + i, h)),
        out_shape=jax.ShapeDtypeStruct((t_total, heads * LANES), BF16),
        input_output_aliases=aliases,
        scratch_shapes=[
            pltpu.VMEM((tq // LANES, tk, LANES), F32),
            pltpu.VMEM((tq // LANES, tk, LANES), F32),
            pltpu.VMEM((8, tq), F32),
            pltpu.VMEM((8, tq), F32),
            pltpu.VMEM((tq // LANES, tk, LANES), BF16),
            pltpu.VMEM((tq // LANES, tk, LANES), BF16),
            pltpu.VMEM((8, tq), F32),
            pltpu.VMEM((8, tq), F32),
            pltpu.VMEM((8, tq), F32),
            pltpu.VMEM((tq // cb_pv, VT_ROWS, cb_pv), F32),
        ],
        compiler_params=_cparams(3),
        name="mla_attention",
    )(qtf, kf, vtf, pm, *prev)


def _mlstm_kernel(qf_ref, kf_ref, vf_ref, gf_ref, qb_ref, kb_ref, vb_ref, gb_ref, bif_ref,
                  hf_ref, hb_ref, st_ref, m_ref, *, heads, log2_k_scale):
    L = ML_CHUNK
    dh = ML_HEAD_DIM

    @pl.when(pl.program_id(1) == 0)
    def _():
        st_ref[...] = jnp.zeros_like(st_ref)
        m_ref[...] = jnp.zeros_like(m_ref)

    assert L == LANES
    row = lax.broadcasted_iota(jnp.int32, (L, L), 0)
    col = lax.broadcasted_iota(jnp.int32, (L, L), 1)
    ones_blk = jnp.ones((L, LANES), BF16)

    def rep(column):
        return jnp.broadcast_to(column, (L, LANES))

    dirs = ((qf_ref, kf_ref, vf_ref, gf_ref, hf_ref, col <= row, L - 1),
            (qb_ref, kb_ref, vb_ref, gb_ref, hb_ref, col >= row, 0))
    for d, (q_ref, k_ref, v_ref, g_ref, h_ref, mask, last) in enumerate(dirs):
        g = g_ref[...] + bif_ref[...]
        lf = _log_sigmoid(g) * LOG2_E
        g = g * LOG2_E
        b_all = jnp.dot(mask.astype(F32), lf, preferred_element_type=F32,
                        precision=lax.Precision.HIGHEST)
        g_t = g.T
        b_t = b_all.T
        for h in range(heads):
            ci = 2 * heads * d + h
            cf = ci + heads
            idx = d * heads + h
            b_r = rep(b_all[:, cf:cf + 1])
            i_r = rep(g[:, ci:ci + 1])
            arow = g_t[ci:ci + 1, :] - b_t[cf:cf + 1, :]
            m_prev = m_ref[idx]
            logd = jnp.where(mask, b_r + arow, -jnp.inf)
            inter = b_r + m_prev
            m_t = jnp.maximum(inter, rep(jnp.max(logd, axis=1, keepdims=True)))
            dmat = jnp.exp2(logd - m_t)
            w_inter = jnp.exp2(inter - m_t)
            q = q_ref[:, h * dh:(h + 1) * dh]
            k = k_ref[:, h * dh:(h + 1) * dh]
            v = v_ref[:, h * dh:(h + 1) * dh]
            s = lax.dot_general(q, k, (((1,), (1,)), ((), ())), preferred_element_type=F32)
            s = (s * dmat).astype(BF16)
            st_prev = st_ref[idx]
            intra = jnp.dot(s, jnp.concatenate([v, ones_blk], axis=1), preferred_element_type=F32)
            inter_p = jnp.dot(q, st_prev.astype(BF16), preferred_element_type=F32)
            num = intra[:, :dh] + w_inter * inter_p[:, :dh]
            den = intra[:, dh:] + w_inter * inter_p[:, dh:]
            hout = num / jnp.maximum(jnp.abs(den), jnp.exp2(-log2_k_scale - m_t))
            h_ref[:, h * dh:(h + 1) * dh] = hout.astype(h_ref.dtype)
            m_new = m_t[last:last + 1, :]
            b_last = b_r[last:last + 1, :]
            w_r = jnp.exp2(b_last - b_r + i_r - m_new)
            decay = jnp.exp2(b_last + m_prev - m_new)
            wv = jnp.concatenate([v.astype(F32) * w_r, w_r], axis=1).astype(BF16)
            upd = lax.dot_general(k, wv, (((0,), (0,)), ((), ())), preferred_element_type=F32)
            st_ref[idx] = jnp.concatenate([decay, decay], axis=1) * st_prev + upd
            m_ref[idx] = m_new


def _mlstm(pm, ps, bif, prev, *, t_total, row0, n_seq, seq, q_col0):
    heads = ML_HEADS
    width = heads * ML_HEAD_DIM
    L = ML_CHUNK
    nc = seq // L
    rb0 = row0 // L
    qc = q_col0 // width
    fwd = lambda n, c: rb0 + n * nc + c
    bwd = lambda n, c: rb0 + n * nc + (nc - 1 - c)

    def col_spec(chunk_of, cb):
        return pl.BlockSpec((L, width), lambda n, c: (chunk_of(n, c), cb))

    gate_f = pl.BlockSpec((L, LANES), lambda n, c: (fwd(n, c), 1))
    gate_b = pl.BlockSpec((L, LANES), lambda n, c: (bwd(n, c), 1))
    out_sds = jax.ShapeDtypeStruct((t_total, width), BF16)
    kern = functools.partial(_mlstm_kernel, heads=heads, log2_k_scale=-0.5 * math.log2(ML_HEAD_DIM))
    kern, prev_specs, aliases, prev = _with_carried_outputs(kern, 9, prev)
    return pl.pallas_call(
        kern,
        grid=(n_seq, nc),
        in_specs=[
            col_spec(fwd, qc), col_spec(fwd, qc + 1), col_spec(fwd, qc + 2), gate_f,
            col_spec(bwd, qc), col_spec(bwd, qc + 1), col_spec(bwd, qc + 2), gate_b,
            pl.BlockSpec((1, LANES), lambda n, c: (0, 0)),
        ] + prev_specs,
        out_specs=[
            pl.BlockSpec((L, width), lambda n, c: (fwd(n, c), 0)),
            pl.BlockSpec((L, width), lambda n, c: (bwd(n, c), 0)),
        ],
        out_shape=[out_sds, out_sds],
        input_output_aliases=aliases,
        scratch_shapes=[
            pltpu.VMEM((2 * heads, ML_HEAD_DIM, HEAD_LANES), F32),
            pltpu.VMEM((2 * heads, 1, LANES), F32),
        ],
        compiler_params=_cparams(2),
        name="mlstm_bidir",
    )(pm, pm, pm, ps, pm, pm, pm, ps, bif, *prev)


def _tail_kernel(x_ref, attn_ref, hf_ref, hb_ref, o_ref, za_ref, ga_ref, gb_ref, gate_ref,
                 mlw_ref, wpa_ref, wpb_ref, wo_ref, lnw_ref, lnb_ref, y_ref, *, heads, alpha):
    dh = ML_HEAD_DIM
    tm = x_ref.shape[0]
    sub = min(TAIL_SUB, tm)
    for r0 in range(0, tm, sub):
        rows = slice(r0, r0 + sub)
        hs = hf_ref[rows, :].astype(F32) + hb_ref[rows, :].astype(F32)
        hn = jnp.concatenate(
            [_layer_norm_rows(hs[:, h * dh:(h + 1) * dh]) for h in range(heads)], axis=1)
        a_in = (hn * mlw_ref[...] * _sigmoid(o_ref[rows, :].astype(F32))
                * _silu(za_ref[rows, :].astype(F32)))
        y_a = jnp.dot(a_in.astype(BF16), wpa_ref[...], preferred_element_type=F32)
        y_b = jnp.dot(attn_ref[rows, :], wpb_ref[...], preferred_element_type=F32)
        merged = (_sigmoid(ga_ref[rows, :].astype(F32)) * y_a
                  + _sigmoid(gb_ref[rows, :].astype(F32)) * y_b)
        out = jnp.dot(merged.astype(BF16), wo_ref[...], preferred_element_type=F32)
        r = alpha * x_ref[rows, :] + gate_ref[0] * out
        y_ref[rows, :] = _layer_norm_rows(r) * lnw_ref[...] + lnb_ref[...]


def _tail(x, attn, hf, hb, pm, gate, mlw, wpa, wpb, wo, lnw, lnb, *, seg, alpha, o_col0, za_col0, g_col0):
    t, d = x.shape
    wa = hf.shape[1]
    tm = min(TAIL_TM, seg)
    const = lambda shape: pl.BlockSpec(shape, lambda i: (0,) * len(shape), pipeline_mode=pl.Buffered(1))
    kern = functools.partial(_tail_kernel, heads=ML_HEADS, alpha=alpha)
    return pl.pallas_call(
        kern,
        grid=(t // tm,),
        in_specs=[
            pl.BlockSpec((tm, d), lambda i: (i, 0)),
            pl.BlockSpec((tm, attn.shape[1]), lambda i: (i, 0)),
            pl.BlockSpec((tm, wa), lambda i: (i, 0)),
            pl.BlockSpec((tm, wa), lambda i: (i, 0)),
            pl.BlockSpec((tm, wa), lambda i: (i, o_col0 // wa)),
            pl.BlockSpec((tm, wa), lambda i: (i, za_col0 // wa)),
            pl.BlockSpec((tm, d), lambda i: (i, g_col0 // d)),
            pl.BlockSpec((tm, d), lambda i: (i, g_col0 // d + 1)),
            pl.BlockSpec((1, 1, d), lambda i: ((i * tm) // seg, 0, 0)),
            const((1, wa)),
            const(wpa.shape),
            const(wpb.shape),
            const(wo.shape),
            const((1, d)),
            const((1, d)),
        ],
        out_specs=pl.BlockSpec((tm, d), lambda i: (i, 0)),
        out_shape=jax.ShapeDtypeStruct((t, d), F32),
        compiler_params=_cparams(1),
        name="tail",
    )(x, attn, hf, hb, pm, pm, pm, pm, gate, mlw, wpa, wpb, wo, lnw, lnb)


def _rot_half(w):
    half = w.shape[-1] // 2
    return jnp.concatenate([-w[..., half:], w[..., :half]], axis=-1)


def _prep_weights(w_in, b_if, w_q_b, w_kv_b, w_proj_a, w_proj_b, w_out, d_model, q_rank, kv_rank):
    depth = w_in.shape[0]
    wa = ML_HEADS * ML_HEAD_DIM
    wb = MLA_HEADS * MLA_V_DIM
    sizes = [q_rank, kv_rank, MLA_ROPE_DIM, wb, wa, wa, wa, wa, wa, 4 * ML_HEADS, 2 * d_model]
    offs = [0]
    for s in sizes:
        offs.append(offs[-1] + s)
    part = lambda i: w_in[:, :, offs[i]:offs[i + 1]]
    (q_lat, kv_lat, k_rope, z_b, ml_q, ml_k, ml_v, ml_o, z_a, ml_g, merge_g) = [part(i) for i in range(11)]
    main_parts = [q_lat, kv_lat, z_b, ml_q, ml_k, ml_v, ml_o, z_a, merge_g]
    w_main = jnp.concatenate(main_parts, axis=-1).astype(BF16)
    names = ["q_lat", "kv_lat", "z_b", "ml_q", "ml_k", "ml_v", "ml_o", "z_a", "merge_g"]
    cols, o = {}, 0
    for nme, p in zip(names, main_parts):
        cols[nme] = o
        o += p.shape[-1]
    pad = jnp.zeros(w_in.shape[:2] + (LANES - 4 * ML_HEADS,), w_in.dtype)
    w_small = jnp.concatenate([k_rope, _rot_half(k_rope), ml_g, pad], axis=-1).astype(BF16)

    wq = w_q_b.reshape(depth, q_rank, MLA_HEADS, MLA_QK_DIM)
    wq_r = wq[..., MLA_NOPE_DIM:]
    wq = jnp.concatenate([wq[..., :MLA_NOPE_DIM], wq_r, _rot_half(wq_r)], axis=-1)
    wq = wq.reshape(depth, q_rank, MLA_HEADS * HEAD_LANES).astype(BF16)
    wkv = w_kv_b.reshape(depth, kv_rank, MLA_HEADS, MLA_NOPE_DIM + MLA_V_DIM)
    wk = wkv[..., :MLA_NOPE_DIM].reshape(depth, kv_rank, MLA_HEADS * MLA_NOPE_DIM).astype(BF16)
    wv = wkv[..., MLA_NOPE_DIM:].reshape(depth, kv_rank, MLA_HEADS * MLA_V_DIM).astype(BF16)
    bif = jnp.pad(b_if.reshape(depth, 1, 4 * ML_HEADS), ((0, 0), (0, 0), (0, LANES - 4 * ML_HEADS)))
    return dict(w_main=w_main, w_small=w_small, cols=cols, wq=wq, wk=wk, wv=wv, bif=bif,
                wpa=w_proj_a.astype(BF16), wpb=w_proj_b.astype(BF16), wo=w_out.astype(BF16))


def _rope_lane_tables(seq_len):
    inv_freq = ROPE_THETA ** (-jnp.arange(0, MLA_ROPE_DIM, 2, dtype=F32) / MLA_ROPE_DIM)
    ang = jnp.arange(seq_len, dtype=F32)[:, None] * inv_freq[None, :]
    zeros = jnp.zeros((seq_len, LANES - MLA_ROPE_DIM), F32)
    cos, sin = jnp.cos(ang), jnp.sin(ang)
    return (jnp.concatenate([cos, cos, zeros], axis=1), jnp.concatenate([sin, sin, zeros], axis=1))


def kernel(x_prompt, x_sample, c_prompt, c_sample, w_ada, b_ada, w_in, b_if, q_norm_w, kv_norm_w,
           w_q_b, w_kv_b, ml_norm_w, w_proj_a, w_proj_b, w_out, ln_w, ln_b):
    batch, seq, d = x_prompt.shape
    dec_batch, dec_seq, _ = x_sample.shape
    depth = w_ada.shape[0]
    q_rank = q_norm_w.shape[1]
    kv_rank = kv_norm_w.shape[1]
    alpha = (2 * depth) ** 0.25
    assert seq % dec_seq == 0 and dec_seq % ML_CHUNK == 0
    seg = dec_seq
    t_prompt = batch * seq
    t = t_prompt + dec_batch * dec_seq

    x = jnp.concatenate([x_prompt.reshape(t_prompt, d), x_sample.reshape(dec_batch * dec_seq, d)], axis=0)
    c_all = jnp.concatenate([c_prompt, c_sample], axis=0)
    n_cond = c_all.shape[0]
    rows = -(-n_cond // 16) * 16
    c_pad = jnp.pad(c_all, ((0, rows - n_cond), (0, 0)))
    seg_cond = jnp.concatenate([jnp.repeat(jnp.arange(batch), seq // seg),
                                batch + jnp.arange(dec_batch)])

    mod = _adaln_mod(c_pad, w_ada, b_ada)
    mod = mod[:, seg_cond, :].reshape(depth, t // seg, 1, 3, d)
    shift, scale, gate = mod[..., 0, :], mod[..., 1, :], mod[..., 2, :]

    w = _prep_weights(w_in, b_if, w_q_b, w_kv_b, w_proj_a, w_proj_b, w_out, d, q_rank, kv_rank)
    cols = w["cols"]
    cs_tab, sn_tab = _rope_lane_tables(max(seq, dec_seq))
    groups = ((0, batch, seq), (t_prompt, dec_batch, dec_seq))

    for l in range(depth):
        pm, ps = _inproj(x, shift[l], scale[l], w["w_main"][l], w["w_small"][l], seg)
        qf, kf, vf = _qkv(pm, ps, cs_tab, sn_tab, q_norm_w[l][None], kv_norm_w[l][None],
                          w["wq"][l], w["wk"][l], w["wv"][l],
                          t_prompt=t_prompt, dec_seq=dec_seq, q_rank=q_rank, kv_rank=kv_rank,
                          tkv=min(ATTN_TK, dec_seq))
        attn, hfb = None, None
        for row0, n_seq, s_len in groups:
            attn = [_attention(qf, kf, vf, pm, attn, t_total=t, row0=row0, n_seq=n_seq, seq=s_len,
                               zb_col0=cols["z_b"])]
            hfb = _mlstm(pm, ps, w["bif"][l], hfb, t_total=t, row0=row0, n_seq=n_seq, seq=s_len,
                         q_col0=cols["ml_q"])
        x = _tail(x, attn[0], hfb[0], hfb[1],
                  pm, gate[l], ml_norm_w[l][None], w["wpa"][l], w["wpb"][l], w["wo"][l],
                  ln_w[l][None], ln_b[l][None], seg=seg, alpha=alpha,
                  o_col0=cols["ml_o"], za_col0=cols["z_a"], g_col0=cols["merge_g"])

    y_prompt = x[:t_prompt].reshape(batch, seq, d)
    y_sample = x[t_prompt:].reshape(dec_batch, dec_seq, d)
    return (y_prompt, y_sample)
```

```python
import functools
import math

import jax
import jax.numpy as jnp
from jax import lax
from jax.experimental import pallas as pl
from jax.experimental.pallas import tpu as pltpu

F32 = jnp.float32
BF16 = jnp.bfloat16

MLA_HEADS = 16
MLA_NOPE_DIM = 128
MLA_ROPE_DIM = 64
MLA_V_DIM = 128
MLA_QK_DIM = MLA_NOPE_DIM + MLA_ROPE_DIM
ROPE_THETA = 10000.0
ML_HEADS = 8
ML_HEAD_DIM = 128
ML_CHUNK = 128
LN_EPS = 1e-5
RMS_EPS = 1e-6
LOG2_E = 1.4426950408889634

LANES = 128
HEAD_LANES = 2 * LANES
VT_ROWS = MLA_V_DIM + 16
VMEM_LIMIT_BYTES = 56 * 1024 * 1024

MOD_TN = 1024
INPROJ_TM = 1024
INPROJ_TN = 1024
QKV_TM = 256
ATTN_TQ = 2048
ATTN_TK = 512
ATTN_CB_PV = 256
TAIL_TM = 256
TAIL_SUB = 256


def _cparams(n_axes):
    return pltpu.CompilerParams(
        dimension_semantics=("arbitrary",) * n_axes,
        vmem_limit_bytes=VMEM_LIMIT_BYTES,
    )


def _sigmoid(x):
    return 1.0 / (1.0 + jnp.exp(-x))


def _silu(x):
    return x * _sigmoid(x)


def _log_sigmoid(x):
    return jnp.minimum(x, 0.0) - jnp.log(1.0 + jnp.exp(-jnp.abs(x)))


def _layer_norm_rows(x):
    mu = jnp.mean(x, axis=-1, keepdims=True)
    xc = x - mu
    var = jnp.mean(xc * xc, axis=-1, keepdims=True)
    return xc * lax.rsqrt(var + LN_EPS)


def _mod_kernel(c_ref, w_ref, b_ref, o_ref):
    c = c_ref[...]
    a = _silu(c).astype(BF16)
    o_ref[0] = jnp.dot(a, w_ref[0].astype(BF16), preferred_element_type=F32) + b_ref[0]


def _adaln_mod(c_pad, w_ada, b_ada):
    depth, d, n = w_ada.shape
    rows = c_pad.shape[0]
    tn = min(MOD_TN, n)
    return pl.pallas_call(
        _mod_kernel,
        grid=(depth, n // tn),
        in_specs=[
            pl.BlockSpec((rows, d), lambda l, j: (0, 0)),
            pl.BlockSpec((1, d, tn), lambda l, j: (l, 0, j)),
            pl.BlockSpec((1, 1, tn), lambda l, j: (l, 0, j)),
        ],
        out_specs=pl.BlockSpec((1, rows, tn), lambda l, j: (l, 0, j)),
        out_shape=jax.ShapeDtypeStruct((depth, rows, n), F32),
        compiler_params=_cparams(2),
        name="adaln_mod",
    )(c_pad, w_ada, b_ada.reshape(depth, 1, n))


def _inproj_kernel(x_ref, sh_ref, sc_ref, wm_ref, ws_ref, pm_ref, ps_ref, u_ref):
    @pl.when(pl.program_id(1) == 0)
    def _():
        u = _layer_norm_rows(x_ref[...]) * (1.0 + sc_ref[0]) + sh_ref[0]
        ub = u.astype(BF16)
        u_ref[...] = ub
        ps_ref[...] = jnp.dot(ub, ws_ref[...], preferred_element_type=F32)

    pm_ref[...] = jnp.dot(u_ref[...], wm_ref[...], preferred_element_type=F32).astype(BF16)


def _inproj(x, shift, scale, w_main, w_small, prev, *, seg, t_total, row0):
    t, d = x.shape
    n = w_main.shape[1]
    ns = w_small.shape[1]
    tm = min(INPROJ_TM, seg)
    tn = min(INPROJ_TN, n)
    rb0 = row0 // tm
    seg_of = lambda i, j: ((row0 + i * tm) // seg, 0, 0)
    kern, prev_specs, aliases, prev = _with_carried_outputs(_inproj_kernel, 5, prev)
    return pl.pallas_call(
        kern,
        grid=(t // tm, n // tn),
        in_specs=[
            pl.BlockSpec((tm, d), lambda i, j: (i, 0)),
            pl.BlockSpec((1, 1, d), seg_of),
            pl.BlockSpec((1, 1, d), seg_of),
            pl.BlockSpec((d, tn), lambda i, j: (0, j)),
            pl.BlockSpec((d, ns), lambda i, j: (0, 0)),
        ] + prev_specs,
        out_specs=[
            pl.BlockSpec((tm, tn), lambda i, j: (rb0 + i, j)),
            pl.BlockSpec((tm, ns), lambda i, j: (rb0 + i, 0)),
        ],
        out_shape=[
            jax.ShapeDtypeStruct((t_total, n), BF16),
            jax.ShapeDtypeStruct((t_total, ns), F32),
        ],
        input_output_aliases=aliases,
        scratch_shapes=[pltpu.VMEM((tm, d), BF16)],
        compiler_params=_cparams(2),
        name="inproj",
    )(x, shift, scale, w_main, w_small, *prev)


def _rope_lanes(x, cs, sn):
    return x * cs + pltpu.roll(x, LANES // 2, axis=1) * sn


def _qkv_kernel(ql_ref, kvl_ref, ps_ref, cs_ref, sn_ref, qnw_ref, kvnw_ref, wq_ref, wk_ref, wv_ref,
                qt_out, k_out, vt_out, *, heads, qk_scale):
    def rms(v, w):
        return (v * lax.rsqrt(jnp.mean(v * v, axis=-1, keepdims=True) + RMS_EPS) * w).astype(BF16)

    qn = rms(ql_ref[...].astype(F32), qnw_ref[...])
    kvn = rms(kvl_ref[...].astype(F32), kvnw_ref[...])
    cs = cs_ref[...]
    sn = sn_ref[...]
    kr = _rope_lanes(ps_ref[...], cs, sn).astype(BF16)
    tm = kr.shape[0]
    sub = lax.broadcasted_iota(jnp.int32, (VT_ROWS - MLA_V_DIM, tm), 0)
    ones_row = jnp.where(sub == 0, 1.0, 0.0).astype(BF16)
    for h in range(heads):
        qa = jnp.dot(qn, wq_ref[:, h * HEAD_LANES:(h + 1) * HEAD_LANES], preferred_element_type=F32)
        qt_out[h, 0, :LANES, :] = (qa[:, :LANES] * qk_scale).T.astype(BF16)
        qt_out[h, 0, LANES:, :] = (_rope_lanes(qa[:, LANES:], cs, sn) * qk_scale).T.astype(BF16)
        k_out[h, :, LANES:] = kr
        vt_out[h, 0, MLA_V_DIM:, :] = ones_row
    for h in range(0, heads, 2):
        ka = jnp.dot(kvn, wk_ref[:, h * LANES:(h + 2) * LANES], preferred_element_type=F32)
        va = jnp.dot(kvn, wv_ref[:, h * LANES:(h + 2) * LANES], preferred_element_type=F32)
        for e in range(2):
            k_out[h + e, :, :LANES] = ka[:, e * LANES:(e + 1) * LANES].astype(BF16)
            vt_out[h + e, 0, :MLA_V_DIM, :] = va[:, e * LANES:(e + 1) * LANES].T.astype(BF16)


def _qkv(pm, ps, cs_tab, sn_tab, qnw, kvnw, wq, wk, wv, *, t_prompt, dec_seq, q_rank, kv_rank, tkv):
    t = pm.shape[0]
    heads = MLA_HEADS
    tm = min(QKV_TM, tkv)
    assert q_rank == kv_rank and q_rank % LANES == 0 and tkv % tm == 0 and dec_seq % tkv == 0
    per_kv = tkv // tm
    np_blocks = t_prompt // tm
    dec_blocks = dec_seq // tm

    def pos_block(i):
        return (jnp.where(i < np_blocks, i, (i - np_blocks) % dec_blocks), 0)

    out_sds = jax.ShapeDtypeStruct((heads, t, HEAD_LANES), BF16)
    out_spec = pl.BlockSpec((heads, tm, HEAD_LANES), lambda i: (0, i, 0))
    kern = functools.partial(_qkv_kernel, heads=heads, qk_scale=MLA_QK_DIM ** -0.5 * LOG2_E)
    return pl.pallas_call(
        kern,
        grid=(t // tm,),
        in_specs=[
            pl.BlockSpec((tm, q_rank), lambda i: (i, 0)),
            pl.BlockSpec((tm, kv_rank), lambda i: (i, 1)),
            pl.BlockSpec((tm, LANES), lambda i: (i, 0)),
            pl.BlockSpec((tm, LANES), pos_block),
            pl.BlockSpec((tm, LANES), pos_block),
            pl.BlockSpec((1, q_rank), lambda i: (0, 0)),
            pl.BlockSpec((1, kv_rank), lambda i: (0, 0)),
            pl.BlockSpec(wq.shape, lambda i: (0, 0)),
            pl.BlockSpec(wk.shape, lambda i: (0, 0)),
            pl.BlockSpec(wv.shape, lambda i: (0, 0)),
        ],
        out_specs=[pl.BlockSpec((heads, 1, HEAD_LANES, tm), lambda i: (0, i, 0, 0)), out_spec,
                   pl.BlockSpec((heads, 1, VT_ROWS, tm), lambda i: (0, i // per_kv, 0, i % per_kv))],
        out_shape=[jax.ShapeDtypeStruct((heads, t // tm, HEAD_LANES, tm), BF16), out_sds,
                   jax.ShapeDtypeStruct((heads, t // tkv, VT_ROWS, tkv), BF16)],
        compiler_params=_cparams(1),
        name="qkv_up",
    )(pm, pm, ps, cs_tab, sn_tab, qnw, kvnw, wq, wk, wv)


def _attn_kernel(q_ref, k_ref, vt_ref, zb_ref, o_ref, s0_ref, s1_ref, x0_ref, x1_ref, p0_ref, p1_ref,
                 a0_ref, a1_ref, m_ref, acc_ref, *, tk, nk, cb_pv):
    n_slab, _, slab = q_ref.shape[1:]
    tq = n_slab * slab
    per_slab = slab // LANES

    def scores(j, s_ref, x_ref):
        start = pl.multiple_of(j * tk, tk)
        kt = k_ref[0, pl.ds(start, tk), :]
        for b in range(n_slab):
            st = jnp.dot(kt, q_ref[0, b], preferred_element_type=F32)
            for c in range(per_slab):
                s_ref[b * per_slab + c] = st[:, c * LANES:(c + 1) * LANES]
            x_ref[:, b * slab:(b + 1) * slab] = jnp.broadcast_to(
                jnp.max(st, axis=0, keepdims=True), (x_ref.shape[0], slab))

    def softmax(s_ref, x_ref, p_ref, a_ref):
        for c in range(tq // LANES):
            cols = slice(c * LANES, (c + 1) * LANES)
            m_prev = m_ref[:, cols]
            m_new = jnp.maximum(m_prev, x_ref[:, cols])
            p_ref[c] = jnp.exp2(s_ref[c] - m_new[0:1, :]).astype(BF16)
            a_ref[:, cols] = jnp.exp2(m_prev - m_new)
            m_ref[:, cols] = m_new

    def pv(j, p_ref, a_ref):
        vt = vt_ref[0, j]
        per = cb_pv // LANES
        for c in range(tq // cb_pv):
            cols = slice(c * cb_pv, (c + 1) * cb_pv)
            pt = jnp.concatenate([p_ref[c * per + i] for i in range(per)], axis=1)
            upd = jnp.dot(vt, pt, preferred_element_type=F32)
            acc_ref[c] = acc_ref[c] * a_ref[0:1, cols] + upd

    m_ref[...] = jnp.full(m_ref.shape, -jnp.inf, F32)
    acc_ref[...] = jnp.zeros(acc_ref.shape, F32)
    scores(0, s0_ref, x0_ref)
    scores(1, s1_ref, x1_ref)
    softmax(s0_ref, x0_ref, p0_ref, a0_ref)

    def body(jj, carry):
        j = 2 * jj
        scores(j + 2, s0_ref, x0_ref)
        softmax(s1_ref, x1_ref, p1_ref, a1_ref)
        pv(j, p0_ref, a0_ref)
        scores(j + 3, s1_ref, x1_ref)
        softmax(s0_ref, x0_ref, p0_ref, a0_ref)
        pv(j + 1, p1_ref, a1_ref)
        return carry

    lax.fori_loop(0, nk // 2 - 1, body, 0)
    softmax(s1_ref, x1_ref, p1_ref, a1_ref)
    pv(nk - 2, p0_ref, a0_ref)
    pv(nk - 1, p1_ref, a1_ref)
    for c in range(tq // cb_pv):
        rows = slice(c * cb_pv, (c + 1) * cb_pv)
        o = (acc_ref[c, :MLA_V_DIM, :] / acc_ref[c, MLA_V_DIM:MLA_V_DIM + 1, :]).T
        o_ref[rows, :] = (o * _silu(zb_ref[rows, :].astype(F32))).astype(BF16)


def _with_carried_outputs(kern, n_in, prev):
    if prev is None:
        return kern, [], {}, ()

    def body(*refs):
        return kern(*refs[:n_in], *refs[n_in + len(prev):])

    specs = [pl.BlockSpec(memory_space=pl.ANY)] * len(prev)
    return body, specs, {n_in + k: k for k in range(len(prev))}, tuple(prev)


def _attention(qtf, kf, vtf, pm, prev, *, t_total, row0, n_seq, seq, zb_col0):
    heads, _, _, slab = qtf.shape
    tk = vtf.shape[3]
    tq = min(ATTN_TQ, seq)
    cb_pv = min(ATTN_CB_PV, tq)
    assert tq % slab == 0 and slab % LANES == 0
    nq = seq // tq
    qb0 = row0 // tq
    sb0 = row0 // seq
    zc0 = zb_col0 // LANES
    nk = seq // tk
    assert nk % 2 == 0
    kern = functools.partial(_attn_kernel, tk=tk, nk=nk, cb_pv=cb_pv)
    kern, prev_specs, aliases, prev = _with_carried_outputs(kern, 4, prev)
    return pl.pallas_call(
        kern,
        grid=(n_seq, heads, nq),
        in_specs=[
            pl.BlockSpec((1, tq // slab, HEAD_LANES, slab), lambda n, h, i: (h, qb0 + n * nq + i, 0, 0)),
            pl.BlockSpec((1, seq, HEAD_LANES), lambda n, h, i: (h, sb0 + n, 0)),
            pl.BlockSpec((1, nk, VT_ROWS, tk), lambda n, h, i: (h, sb0 + n, 0, 0)),
            pl.BlockSpec((tq, LANES), lambda n, h, i: (qb0 + n * nq + i, zc0 + h)),
        ] + prev_specs,
        out_specs=pl.BlockSpec((tq, LANES), lambda n, h, i: (qb0 + n * nq + i, h)),
        out_shape=jax.ShapeDtypeStruct((t_total, heads * LANES), BF16),
        input_output_aliases=aliases,
        scratch_shapes=[
            pltpu.VMEM((tq // LANES, tk, LANES), F32),
            pltpu.VMEM((tq // LANES, tk, LANES), F32),
            pltpu.VMEM((8, tq), F32),
            pltpu.VMEM((8, tq), F32),
            pltpu.VMEM((tq // LANES, tk, LANES), BF16),
            pltpu.VMEM((tq // LANES, tk, LANES), BF16),
            pltpu.VMEM((8, tq), F32),
            pltpu.VMEM((8, tq), F32),
            pltpu.VMEM((8, tq), F32),
            pltpu.VMEM((tq // cb_pv, VT_ROWS, cb_pv), F32),
        ],
        compiler_params=_cparams(3),
        name="mla_attention",
    )(qtf, kf, vtf, pm, *prev)


def _mlstm_kernel(qf_ref, kf_ref, vf_ref, gf_ref, qb_ref, kb_ref, vb_ref, gb_ref, bif_ref,
                  hf_ref, hb_ref, st_ref, m_ref, *, heads, log2_k_scale):
    L = ML_CHUNK
    dh = ML_HEAD_DIM

    @pl.when(pl.program_id(1) == 0)
    def _():
        st_ref[...] = jnp.zeros_like(st_ref)
        m_ref[...] = jnp.zeros_like(m_ref)

    assert L == LANES
    row = lax.broadcasted_iota(jnp.int32, (L, L), 0)
    col = lax.broadcasted_iota(jnp.int32, (L, L), 1)
    ones_blk = jnp.ones((L, LANES), BF16)

    def rep(column):
        return jnp.broadcast_to(column, (L, LANES))

    dirs = ((qf_ref, kf_ref, vf_ref, gf_ref, hf_ref, col <= row, L - 1),
            (qb_ref, kb_ref, vb_ref, gb_ref, hb_ref, col >= row, 0))
    for d, (q_ref, k_ref, v_ref, g_ref, h_ref, mask, last) in enumerate(dirs):
        g = g_ref[...] + bif_ref[...]
        lf = _log_sigmoid(g) * LOG2_E
        g = g * LOG2_E
        b_all = jnp.dot(mask.astype(F32), lf, preferred_element_type=F32,
                        precision=lax.Precision.HIGHEST)
        g_t = g.T
        b_t = b_all.T
        for h in range(heads):
            ci = 2 * heads * d + h
            cf = ci + heads
            idx = d * heads + h
            b_r = rep(b_all[:, cf:cf + 1])
            i_r = rep(g[:, ci:ci + 1])
            arow = g_t[ci:ci + 1, :] - b_t[cf:cf + 1, :]
            m_prev = m_ref[idx]
            logd = jnp.where(mask, b_r + arow, -jnp.inf)
            inter = b_r + m_prev
            m_t = jnp.maximum(inter, rep(jnp.max(logd, axis=1, keepdims=True)))
            dmat = jnp.exp2(logd - m_t)
            w_inter = jnp.exp2(inter - m_t)
            q = q_ref[:, h * dh:(h + 1) * dh]
            k = k_ref[:, h * dh:(h + 1) * dh]
            v = v_ref[:, h * dh:(h + 1) * dh]
            s = lax.dot_general(q, k, (((1,), (1,)), ((), ())), preferred_element_type=F32)
            s = (s * dmat).astype(BF16)
            st_prev = st_ref[idx]
            intra = jnp.dot(s, jnp.concatenate([v, ones_blk], axis=1), preferred_element_type=F32)
            inter_p = jnp.dot(q, st_prev.astype(BF16), preferred_element_type=F32)
            num = intra[:, :dh] + w_inter * inter_p[:, :dh]
            den = intra[:, dh:] + w_inter * inter_p[:, dh:]
            hout = num / jnp.maximum(jnp.abs(den), jnp.exp2(-log2_k_scale - m_t))
            h_ref[:, h * dh:(h + 1) * dh] = hout.astype(h_ref.dtype)
            m_new = m_t[last:last + 1, :]
            b_last = b_r[last:last + 1, :]
            w_r = jnp.exp2(b_last - b_r + i_r - m_new)
            decay = jnp.exp2(b_last + m_prev - m_new)
            wv = jnp.concatenate([v.astype(F32) * w_r, w_r], axis=1).astype(BF16)
            upd = lax.dot_general(k, wv, (((0,), (0,)), ((), ())), preferred_element_type=F32)
            st_ref[idx] = jnp.concatenate([decay, decay], axis=1) * st_prev + upd
            m_ref[idx] = m_new


def _mlstm(pm, ps, bif, prev, *, t_total, row0, n_seq, seq, q_col0):
    heads = ML_HEADS
    width = heads * ML_HEAD_DIM
    L = ML_CHUNK
    nc = seq // L
    rb0 = row0 // L
    qc = q_col0 // width
    fwd = lambda n, c: rb0 + n * nc + c
    bwd = lambda n, c: rb0 + n * nc + (nc - 1 - c)

    def col_spec(chunk_of, cb):
        return pl.BlockSpec((L, width), lambda n, c: (chunk_of(n, c), cb))

    gate_f = pl.BlockSpec((L, LANES), lambda n, c: (fwd(n, c), 1))
    gate_b = pl.BlockSpec((L, LANES), lambda n, c: (bwd(n, c), 1))
    out_sds = jax.ShapeDtypeStruct((t_total, width), BF16)
    kern = functools.partial(_mlstm_kernel, heads=heads, log2_k_scale=-0.5 * math.log2(ML_HEAD_DIM))
    kern, prev_specs, aliases, prev = _with_carried_outputs(kern, 9, prev)
    return pl.pallas_call(
        kern,
        grid=(n_seq, nc),
        in_specs=[
            col_spec(fwd, qc), col_spec(fwd, qc + 1), col_spec(fwd, qc + 2), gate_f,
            col_spec(bwd, qc), col_spec(bwd, qc + 1), col_spec(bwd, qc + 2), gate_b,
            pl.BlockSpec((1, LANES), lambda n, c: (0, 0)),
        ] + prev_specs,
        out_specs=[
            pl.BlockSpec((L, width), lambda n, c: (fwd(n, c), 0)),
            pl.BlockSpec((L, width), lambda n, c: (bwd(n, c), 0)),
        ],
        out_shape=[out_sds, out_sds],
        input_output_aliases=aliases,
        scratch_shapes=[
            pltpu.VMEM((2 * heads, ML_HEAD_DIM, HEAD_LANES), F32),
            pltpu.VMEM((2 * heads, 1, LANES), F32),
        ],
        compiler_params=_cparams(2),
        name="mlstm_bidir",
    )(pm, pm, pm, ps, pm, pm, pm, ps, bif, *prev)


def _tail_kernel(x_ref, attn_ref, hf_ref, hb_ref, o_ref, za_ref, ga_ref, gb_ref, gate_ref,
                 mlw_ref, wpa_ref, wpb_ref, wo_ref, lnw_ref, lnb_ref, y_ref, *, heads, alpha):
    dh = ML_HEAD_DIM
    tm = x_ref.shape[0]
    sub = min(TAIL_SUB, tm)
    for r0 in range(0, tm, sub):
        rows = slice(r0, r0 + sub)
        hs = hf_ref[rows, :].astype(F32) + hb_ref[rows, :].astype(F32)
        hn = jnp.concatenate(
            [_layer_norm_rows(hs[:, h * dh:(h + 1) * dh]) for h in range(heads)], axis=1)
        a_in = (hn * mlw_ref[...] * _sigmoid(o_ref[rows, :].astype(F32))
                * _silu(za_ref[rows, :].astype(F32)))
        y_a = jnp.dot(a_in.astype(BF16), wpa_ref[...], preferred_element_type=F32)
        y_b = jnp.dot(attn_ref[rows, :], wpb_ref[...], preferred_element_type=F32)
        merged = (_sigmoid(ga_ref[rows, :].astype(F32)) * y_a
                  + _sigmoid(gb_ref[rows, :].astype(F32)) * y_b)
        out = jnp.dot(merged.astype(BF16), wo_ref[...], preferred_element_type=F32)
        r = alpha * x_ref[rows, :] + gate_ref[0] * out
        y_ref[rows, :] = _layer_norm_rows(r) * lnw_ref[...] + lnb_ref[...]


def _tail(x, attn, hf, hb, pm, gate, mlw, wpa, wpb, wo, lnw, lnb, prev, *, n_rows, x_row0, row0, out_rows,
          out_row0, seg, alpha, o_col0, za_col0, g_col0):
    d = x.shape[1]
    wa = hf.shape[1]
    tm = min(TAIL_TM, seg)
    xb0, gb0, ob0 = x_row0 // tm, row0 // tm, out_row0 // tm
    const = lambda shape: pl.BlockSpec(shape, lambda i: (0,) * len(shape), pipeline_mode=pl.Buffered(1))
    kern = functools.partial(_tail_kernel, heads=ML_HEADS, alpha=alpha)
    kern, prev_specs, aliases, prev = _with_carried_outputs(kern, 15, prev)
    return pl.pallas_call(
        kern,
        grid=(n_rows // tm,),
        in_specs=[
            pl.BlockSpec((tm, d), lambda i: (xb0 + i, 0)),
            pl.BlockSpec((tm, attn.shape[1]), lambda i: (gb0 + i, 0)),
            pl.BlockSpec((tm, wa), lambda i: (gb0 + i, 0)),
            pl.BlockSpec((tm, wa), lambda i: (gb0 + i, 0)),
            pl.BlockSpec((tm, wa), lambda i: (gb0 + i, o_col0 // wa)),
            pl.BlockSpec((tm, wa), lambda i: (gb0 + i, za_col0 // wa)),
            pl.BlockSpec((tm, d), lambda i: (gb0 + i, g_col0 // d)),
            pl.BlockSpec((tm, d), lambda i: (gb0 + i, g_col0 // d + 1)),
            pl.BlockSpec((1, 1, d), lambda i: ((row0 + i * tm) // seg, 0, 0)),
            const((1, wa)),
            const(wpa.shape),
            const(wpb.shape),
            const(wo.shape),
            const((1, d)),
            const((1, d)),
        ] + prev_specs,
        out_specs=pl.BlockSpec((tm, d), lambda i: (ob0 + i, 0)),
        out_shape=jax.ShapeDtypeStruct((out_rows, d), F32),
        input_output_aliases=aliases,
        compiler_params=_cparams(1),
        name="tail",
    )(x, attn, hf, hb, pm, pm, pm, pm, gate, mlw, wpa, wpb, wo, lnw, lnb, *prev)


def _rot_half(w):
    half = w.shape[-1] // 2
    return jnp.concatenate([-w[..., half:], w[..., :half]], axis=-1)


def _prep_weights(w_in, b_if, w_q_b, w_kv_b, w_proj_a, w_proj_b, w_out, d_model, q_rank, kv_rank):
    depth = w_in.shape[0]
    wa = ML_HEADS * ML_HEAD_DIM
    wb = MLA_HEADS * MLA_V_DIM
    sizes = [q_rank, kv_rank, MLA_ROPE_DIM, wb, wa, wa, wa, wa, wa, 4 * ML_HEADS, 2 * d_model]
    offs = [0]
    for s in sizes:
        offs.append(offs[-1] + s)
    part = lambda i: w_in[:, :, offs[i]:offs[i + 1]]
    (q_lat, kv_lat, k_rope, z_b, ml_q, ml_k, ml_v, ml_o, z_a, ml_g, merge_g) = [part(i) for i in range(11)]
    main_parts = [q_lat, kv_lat, z_b, ml_q, ml_k, ml_v, ml_o, z_a, merge_g]
    w_main = jnp.concatenate(main_parts, axis=-1).astype(BF16)
    names = ["q_lat", "kv_lat", "z_b", "ml_q", "ml_k", "ml_v", "ml_o", "z_a", "merge_g"]
    cols, o = {}, 0
    for nme, p in zip(names, main_parts):
        cols[nme] = o
        o += p.shape[-1]
    pad = jnp.zeros(w_in.shape[:2] + (LANES - 4 * ML_HEADS,), w_in.dtype)
    w_small = jnp.concatenate([k_rope, _rot_half(k_rope), ml_g, pad], axis=-1).astype(BF16)

    wq = w_q_b.reshape(depth, q_rank, MLA_HEADS, MLA_QK_DIM)
    wq_r = wq[..., MLA_NOPE_DIM:]
    wq = jnp.concatenate([wq[..., :MLA_NOPE_DIM], wq_r, _rot_half(wq_r)], axis=-1)
    wq = wq.reshape(depth, q_rank, MLA_HEADS * HEAD_LANES).astype(BF16)
    wkv = w_kv_b.reshape(depth, kv_rank, MLA_HEADS, MLA_NOPE_DIM + MLA_V_DIM)
    wk = wkv[..., :MLA_NOPE_DIM].reshape(depth, kv_rank, MLA_HEADS * MLA_NOPE_DIM).astype(BF16)
    wv = wkv[..., MLA_NOPE_DIM:].reshape(depth, kv_rank, MLA_HEADS * MLA_V_DIM).astype(BF16)
    bif = jnp.pad(b_if.reshape(depth, 1, 4 * ML_HEADS), ((0, 0), (0, 0), (0, LANES - 4 * ML_HEADS)))
    return dict(w_main=w_main, w_small=w_small, cols=cols, wq=wq, wk=wk, wv=wv, bif=bif,
                wpa=w_proj_a.astype(BF16), wpb=w_proj_b.astype(BF16), wo=w_out.astype(BF16))


def _rope_lane_tables(seq_len):
    inv_freq = ROPE_THETA ** (-jnp.arange(0, MLA_ROPE_DIM, 2, dtype=F32) / MLA_ROPE_DIM)
    ang = jnp.arange(seq_len, dtype=F32)[:, None] * inv_freq[None, :]
    zeros = jnp.zeros((seq_len, LANES - MLA_ROPE_DIM), F32)
    cos, sin = jnp.cos(ang), jnp.sin(ang)
    return (jnp.concatenate([cos, cos, zeros], axis=1), jnp.concatenate([sin, sin, zeros], axis=1))


def kernel(x_prompt, x_sample, c_prompt, c_sample, w_ada, b_ada, w_in, b_if, q_norm_w, kv_norm_w,
           w_q_b, w_kv_b, ml_norm_w, w_proj_a, w_proj_b, w_out, ln_w, ln_b):
    batch, seq, d = x_prompt.shape
    dec_batch, dec_seq, _ = x_sample.shape
    depth = w_ada.shape[0]
    q_rank = q_norm_w.shape[1]
    kv_rank = kv_norm_w.shape[1]
    alpha = (2 * depth) ** 0.25
    assert seq % dec_seq == 0 and dec_seq % ML_CHUNK == 0
    seg = dec_seq
    t_prompt = batch * seq
    t = t_prompt + dec_batch * dec_seq

    x_groups = (x_prompt.reshape(t_prompt, d), x_sample.reshape(dec_batch * dec_seq, d))
    c_all = jnp.concatenate([c_prompt, c_sample], axis=0)
    n_cond = c_all.shape[0]
    rows = -(-n_cond // 16) * 16
    c_pad = jnp.pad(c_all, ((0, rows - n_cond), (0, 0)))
    seg_cond = jnp.concatenate([jnp.repeat(jnp.arange(batch), seq // seg),
                                batch + jnp.arange(dec_batch)])

    mod = _adaln_mod(c_pad, w_ada, b_ada)
    mod = mod[:, seg_cond, :].reshape(depth, t // seg, 1, 3, d)
    shift, scale, gate = mod[..., 0, :], mod[..., 1, :], mod[..., 2, :]

    w = _prep_weights(w_in, b_if, w_q_b, w_kv_b, w_proj_a, w_proj_b, w_out, d, q_rank, kv_rank)
    cols = w["cols"]
    cs_tab, sn_tab = _rope_lane_tables(max(seq, dec_seq))
    groups = ((0, batch, seq), (t_prompt, dec_batch, dec_seq))

    x = None
    for l in range(depth):
        if l == 0:
            pmps = None
            for (row0, _, _), xg in zip(groups, x_groups):
                pmps = _inproj(xg, shift[l], scale[l], w["w_main"][l], w["w_small"][l], pmps,
                               seg=seg, t_total=t, row0=row0)
            pm, ps = pmps
        else:
            pm, ps = _inproj(x, shift[l], scale[l], w["w_main"][l], w["w_small"][l], None,
                             seg=seg, t_total=t, row0=0)
        qf, kf, vf = _qkv(pm, ps, cs_tab, sn_tab, q_norm_w[l][None], kv_norm_w[l][None],
                          w["wq"][l], w["wk"][l], w["wv"][l],
                          t_prompt=t_prompt, dec_seq=dec_seq, q_rank=q_rank, kv_rank=kv_rank,
                          tkv=min(ATTN_TK, dec_seq))
        attn, hfb = None, None
        for row0, n_seq, s_len in groups:
            attn = [_attention(qf, kf, vf, pm, attn, t_total=t, row0=row0, n_seq=n_seq, seq=s_len,
                               zb_col0=cols["z_b"])]
            hfb = _mlstm(pm, ps, w["bif"][l], hfb, t_total=t, row0=row0, n_seq=n_seq, seq=s_len,
                         q_col0=cols["ml_q"])
        tail = functools.partial(
            _tail, attn=attn[0], hf=hfb[0], hb=hfb[1], pm=pm, gate=gate[l], mlw=ml_norm_w[l][None],
            wpa=w["wpa"][l], wpb=w["wpb"][l], wo=w["wo"][l], lnw=ln_w[l][None], lnb=ln_b[l][None],
            seg=seg, alpha=alpha, o_col0=cols["ml_o"], za_col0=cols["z_a"], g_col0=cols["merge_g"])
        first, last = l == 0, l == depth - 1
        if not (first or last):
            x = tail(x=x, prev=None, n_rows=t, x_row0=0, row0=0, out_rows=t, out_row0=0)
            continue
        outs, carried = [], None
        for (row0, n_seq, s_len), xg in zip(groups, x_groups):
            n_rows = n_seq * s_len
            y = tail(x=xg if first else x, prev=None if last else carried, n_rows=n_rows,
                     x_row0=0 if first else row0, row0=row0,
                     out_rows=n_rows if last else t, out_row0=0 if last else row0)
            outs.append(y)
            carried = [y]
        x = None if last else outs[-1]

    y_prompt, y_sample = outs
    return (y_prompt.reshape(batch, seq, d), y_sample.reshape(dec_batch, dec_seq, d))
```

```python
import functools
import math

import jax
import jax.numpy as jnp
from jax import lax
from jax.experimental import pallas as pl
from jax.experimental.pallas import tpu as pltpu

F32 = jnp.float32
BF16 = jnp.bfloat16

MLA_HEADS = 16
MLA_NOPE_DIM = 128
MLA_ROPE_DIM = 64
MLA_V_DIM = 128
MLA_QK_DIM = MLA_NOPE_DIM + MLA_ROPE_DIM
ROPE_THETA = 10000.0
ML_HEADS = 8
ML_HEAD_DIM = 128
ML_CHUNK = 128
LN_EPS = 1e-5
RMS_EPS = 1e-6
LOG2_E = 1.4426950408889634

LANES = 128
HEAD_LANES = 2 * LANES
VT_ROWS = MLA_V_DIM + 16
VMEM_LIMIT_BYTES = 56 * 1024 * 1024

MOD_TN = 1024
INPROJ_TM = 1024
INPROJ_TN = 1024
QKV_TM = 256
ATTN_TQ = 2048
ATTN_TK = 512
ATTN_CB_PV = 256
TAIL_TM = 256
TAIL_SUB = 256


def _cparams(n_axes):
    return pltpu.CompilerParams(
        dimension_semantics=("arbitrary",) * n_axes,
        vmem_limit_bytes=VMEM_LIMIT_BYTES,
    )


def _sigmoid(x):
    return 1.0 / (1.0 + jnp.exp(-x))


def _silu(x):
    return x * _sigmoid(x)


def _log_sigmoid(x):
    return jnp.minimum(x, 0.0) - jnp.log(1.0 + jnp.exp(-jnp.abs(x)))


def _layer_norm_rows(x):
    mu = jnp.mean(x, axis=-1, keepdims=True)
    xc = x - mu
    var = jnp.mean(xc * xc, axis=-1, keepdims=True)
    return xc * lax.rsqrt(var + LN_EPS)


def _mod_kernel(c_ref, w_ref, b_ref, o_ref):
    c = c_ref[...]
    a = _silu(c).astype(BF16)
    o_ref[0] = jnp.dot(a, w_ref[0].astype(BF16), preferred_element_type=F32) + b_ref[0]


def _adaln_mod(c_pad, w_ada, b_ada):
    depth, d, n = w_ada.shape
    rows = c_pad.shape[0]
    tn = min(MOD_TN, n)
    return pl.pallas_call(
        _mod_kernel,
        grid=(depth, n // tn),
        in_specs=[
            pl.BlockSpec((rows, d), lambda l, j: (0, 0)),
            pl.BlockSpec((1, d, tn), lambda l, j: (l, 0, j)),
            pl.BlockSpec((1, 1, tn), lambda l, j: (l, 0, j)),
        ],
        out_specs=pl.BlockSpec((1, rows, tn), lambda l, j: (l, 0, j)),
        out_shape=jax.ShapeDtypeStruct((depth, rows, n), F32),
        compiler_params=_cparams(2),
        name="adaln_mod",
    )(c_pad, w_ada, b_ada.reshape(depth, 1, n))


def _inproj_kernel(x_ref, sh_ref, sc_ref, wm_ref, ws_ref, pm_ref, ps_ref, u_ref):
    @pl.when(pl.program_id(1) == 0)
    def _():
        u = _layer_norm_rows(x_ref[...]) * (1.0 + sc_ref[0]) + sh_ref[0]
        ub = u.astype(BF16)
        u_ref[...] = ub
        ps_ref[...] = jnp.dot(ub, ws_ref[...], preferred_element_type=F32)

    pm_ref[...] = jnp.dot(u_ref[...], wm_ref[...], preferred_element_type=F32).astype(BF16)


def _inproj(x, shift, scale, w_main, w_small, prev, *, seg, t_total, row0):
    t, d = x.shape
    n = w_main.shape[1]
    ns = w_small.shape[1]
    tm = min(INPROJ_TM, seg)
    tn = min(INPROJ_TN, n)
    rb0 = row0 // tm
    seg_of = lambda i, j: ((row0 + i * tm) // seg, 0, 0)
    kern, prev_specs, aliases, prev = _with_carried_outputs(_inproj_kernel, 5, prev)
    return pl.pallas_call(
        kern,
        grid=(t // tm, n // tn),
        in_specs=[
            pl.BlockSpec((tm, d), lambda i, j: (i, 0)),
            pl.BlockSpec((1, 1, d), seg_of),
            pl.BlockSpec((1, 1, d), seg_of),
            pl.BlockSpec((d, tn), lambda i, j: (0, j)),
            pl.BlockSpec((d, ns), lambda i, j: (0, 0)),
        ] + prev_specs,
        out_specs=[
            pl.BlockSpec((tm, tn), lambda i, j: (rb0 + i, j)),
            pl.BlockSpec((tm, ns), lambda i, j: (rb0 + i, 0)),
        ],
        out_shape=[
            jax.ShapeDtypeStruct((t_total, n), BF16),
            jax.ShapeDtypeStruct((t_total, ns), F32),
        ],
        input_output_aliases=aliases,
        scratch_shapes=[pltpu.VMEM((tm, d), BF16)],
        compiler_params=_cparams(2),
        name="inproj",
    )(x, shift, scale, w_main, w_small, *prev)


def _rope_lanes(x, cs, sn):
    return x * cs + pltpu.roll(x, LANES // 2, axis=1) * sn


def _qkv_kernel(ql_ref, kvl_ref, ps_ref, cs_ref, sn_ref, qnw_ref, kvnw_ref, wq_ref, wk_ref, wv_ref,
                qt_out, k_out, vt_out, *, heads, qk_scale):
    def rms(v, w):
        return (v * lax.rsqrt(jnp.mean(v * v, axis=-1, keepdims=True) + RMS_EPS) * w).astype(BF16)

    qn = rms(ql_ref[...].astype(F32), qnw_ref[...])
    kvn = rms(kvl_ref[...].astype(F32), kvnw_ref[...])
    cs = cs_ref[...]
    sn = sn_ref[...]
    kr = _rope_lanes(ps_ref[...], cs, sn).astype(BF16)
    tm = kr.shape[0]
    sub = lax.broadcasted_iota(jnp.int32, (VT_ROWS - MLA_V_DIM, tm), 0)
    ones_row = jnp.where(sub == 0, 1.0, 0.0).astype(BF16)
    for h in range(heads):
        qa = jnp.dot(qn, wq_ref[:, h * HEAD_LANES:(h + 1) * HEAD_LANES], preferred_element_type=F32)
        qt_out[h, 0, :LANES, :] = (qa[:, :LANES] * qk_scale).T.astype(BF16)
        qt_out[h, 0, LANES:, :] = (_rope_lanes(qa[:, LANES:], cs, sn) * qk_scale).T.astype(BF16)
        k_out[h, :, LANES:] = kr
        vt_out[h, 0, MLA_V_DIM:, :] = ones_row
    for h in range(0, heads, 2):
        ka = jnp.dot(kvn, wk_ref[:, h * LANES:(h + 2) * LANES], preferred_element_type=F32)
        va = jnp.dot(kvn, wv_ref[:, h * LANES:(h + 2) * LANES], preferred_element_type=F32)
        for e in range(2):
            k_out[h + e, :, :LANES] = ka[:, e * LANES:(e + 1) * LANES].astype(BF16)
            vt_out[h + e, 0, :MLA_V_DIM, :] = va[:, e * LANES:(e + 1) * LANES].T.astype(BF16)


def _qkv(pm, ps, cs_tab, sn_tab, qnw, kvnw, wq, wk, wv, *, t_prompt, dec_seq, q_rank, kv_rank, tkv):
    t = pm.shape[0]
    heads = MLA_HEADS
    tm = min(QKV_TM, tkv)
    assert q_rank == kv_rank and q_rank % LANES == 0 and tkv % tm == 0 and dec_seq % tkv == 0
    per_kv = tkv // tm
    np_blocks = t_prompt // tm
    dec_blocks = dec_seq // tm

    def pos_block(i):
        return (jnp.where(i < np_blocks, i, (i - np_blocks) % dec_blocks), 0)

    out_sds = jax.ShapeDtypeStruct((heads, t, HEAD_LANES), BF16)
    out_spec = pl.BlockSpec((heads, tm, HEAD_LANES), lambda i: (0, i, 0))
    kern = functools.partial(_qkv_kernel, heads=heads, qk_scale=MLA_QK_DIM ** -0.5 * LOG2_E)
    return pl.pallas_call(
        kern,
        grid=(t // tm,),
        in_specs=[
            pl.BlockSpec((tm, q_rank), lambda i: (i, 0)),
            pl.BlockSpec((tm, kv_rank), lambda i: (i, 1)),
            pl.BlockSpec((tm, LANES), lambda i: (i, 0)),
            pl.BlockSpec((tm, LANES), pos_block),
            pl.BlockSpec((tm, LANES), pos_block),
            pl.BlockSpec((1, q_rank), lambda i: (0, 0)),
            pl.BlockSpec((1, kv_rank), lambda i: (0, 0)),
            pl.BlockSpec(wq.shape, lambda i: (0, 0)),
            pl.BlockSpec(wk.shape, lambda i: (0, 0)),
            pl.BlockSpec(wv.shape, lambda i: (0, 0)),
        ],
        out_specs=[pl.BlockSpec((heads, 1, HEAD_LANES, tm), lambda i: (0, i, 0, 0)), out_spec,
                   pl.BlockSpec((heads, 1, VT_ROWS, tm), lambda i: (0, i // per_kv, 0, i % per_kv))],
        out_shape=[jax.ShapeDtypeStruct((heads, t // tm, HEAD_LANES, tm), BF16), out_sds,
                   jax.ShapeDtypeStruct((heads, t // tkv, VT_ROWS, tkv), BF16)],
        compiler_params=_cparams(1),
        name="qkv_up",
    )(pm, pm, ps, cs_tab, sn_tab, qnw, kvnw, wq, wk, wv)


def _attn_kernel(q_ref, k_ref, vt_ref, zb_ref, o_ref, s0_ref, s1_ref, x0_ref, x1_ref, p0_ref, p1_ref,
                 a0_ref, a1_ref, m_ref, acc_ref, *, tk, nk, cb_pv):
    n_slab, _, slab = q_ref.shape[1:]
    tq = n_slab * slab
    per_slab = slab // LANES

    def scores(j, s_ref, x_ref):
        start = pl.multiple_of(j * tk, tk)
        kt = k_ref[0, pl.ds(start, tk), :]
        for b in range(n_slab):
            st = jnp.dot(kt, q_ref[0, b], preferred_element_type=F32)
            for c in range(per_slab):
                s_ref[b * per_slab + c] = st[:, c * LANES:(c + 1) * LANES]
            x_ref[:, b * slab:(b + 1) * slab] = jnp.broadcast_to(
                jnp.max(st, axis=0, keepdims=True), (x_ref.shape[0], slab))

    def softmax(s_ref, x_ref, p_ref, a_ref):
        for c in range(tq // LANES):
            cols = slice(c * LANES, (c + 1) * LANES)
            m_prev = m_ref[:, cols]
            m_new = jnp.maximum(m_prev, x_ref[:, cols])
            p_ref[c] = jnp.exp2(s_ref[c] - m_new[0:1, :]).astype(BF16)
            a_ref[:, cols] = jnp.exp2(m_prev - m_new)
            m_ref[:, cols] = m_new

    def pv(j, p_ref, a_ref):
        vt = vt_ref[0, j]
        per = cb_pv // LANES
        for c in range(tq // cb_pv):
            cols = slice(c * cb_pv, (c + 1) * cb_pv)
            pt = jnp.concatenate([p_ref[c * per + i] for i in range(per)], axis=1)
            upd = jnp.dot(vt, pt, preferred_element_type=F32)
            acc_ref[c] = acc_ref[c] * a_ref[0:1, cols] + upd

    m_ref[...] = jnp.full(m_ref.shape, -jnp.inf, F32)
    acc_ref[...] = jnp.zeros(acc_ref.shape, F32)
    scores(0, s0_ref, x0_ref)
    scores(1, s1_ref, x1_ref)
    softmax(s0_ref, x0_ref, p0_ref, a0_ref)

    def body(jj, carry):
        j = 2 * jj
        scores(j + 2, s0_ref, x0_ref)
        softmax(s1_ref, x1_ref, p1_ref, a1_ref)
        pv(j, p0_ref, a0_ref)
        scores(j + 3, s1_ref, x1_ref)
        softmax(s0_ref, x0_ref, p0_ref, a0_ref)
        pv(j + 1, p1_ref, a1_ref)
        return carry

    lax.fori_loop(0, nk // 2 - 1, body, 0)
    softmax(s1_ref, x1_ref, p1_ref, a1_ref)
    pv(nk - 2, p0_ref, a0_ref)
    pv(nk - 1, p1_ref, a1_ref)
    for c in range(tq // cb_pv):
        rows = slice(c * cb_pv, (c + 1) * cb_pv)
        o = (acc_ref[c, :MLA_V_DIM, :] / acc_ref[c, MLA_V_DIM:MLA_V_DIM + 1, :]).T
        o_ref[rows, :] = (o * _silu(zb_ref[rows, :].astype(F32))).astype(BF16)


def _with_carried_outputs(kern, n_in, prev):
    if prev is None:
        return kern, [], {}, ()

    def body(*refs):
        return kern(*refs[:n_in], *refs[n_in + len(prev):])

    specs = [pl.BlockSpec(memory_space=pl.ANY)] * len(prev)
    return body, specs, {n_in + k: k for k in range(len(prev))}, tuple(prev)


def _attention(qtf, kf, vtf, pm, prev, *, t_total, row0, n_seq, seq, zb_col0):
    heads, _, _, slab = qtf.shape
    tk = vtf.shape[3]
    tq = min(ATTN_TQ, seq)
    cb_pv = min(ATTN_CB_PV, tq)
    assert tq % slab == 0 and slab % LANES == 0
    nq = seq // tq
    qb0 = row0 // tq
    sb0 = row0 // seq
    zc0 = zb_col0 // LANES
    nk = seq // tk
    assert nk % 2 == 0
    kern = functools.partial(_attn_kernel, tk=tk, nk=nk, cb_pv=cb_pv)
    kern, prev_specs, aliases, prev = _with_carried_outputs(kern, 4, prev)
    return pl.pallas_call(
        kern,
        grid=(n_seq, heads, nq),
        in_specs=[
            pl.BlockSpec((1, tq // slab, HEAD_LANES, slab), lambda n, h, i: (h, qb0 + n * nq + i, 0, 0)),
            pl.BlockSpec((1, seq, HEAD_LANES), lambda n, h, i: (h, sb0 + n, 0)),
            pl.BlockSpec((1, nk, VT_ROWS, tk), lambda n, h, i: (h, sb0 + n, 0, 0)),
            pl.BlockSpec((tq, LANES), lambda n, h, i: (qb0 + n * nq + i, zc0 + h)),
        ] + prev_specs,
        out_specs=pl.BlockSpec((tq, LANES), lambda n, h, i: (qb0 + n * nq + i, h)),
        out_shape=jax.ShapeDtypeStruct((t_total, heads * LANES), BF16),
        input_output_aliases=aliases,
        scratch_shapes=[
            pltpu.VMEM((tq // LANES, tk, LANES), F32),
            pltpu.VMEM((tq // LANES, tk, LANES), F32),
            pltpu.VMEM((8, tq), F32),
            pltpu.VMEM((8, tq), F32),
            pltpu.VMEM((tq // LANES, tk, LANES), BF16),
            pltpu.VMEM((tq // LANES, tk, LANES), BF16),
            pltpu.VMEM((8, tq), F32),
            pltpu.VMEM((8, tq), F32),
            pltpu.VMEM((8, tq), F32),
            pltpu.VMEM((tq // cb_pv, VT_ROWS, cb_pv), F32),
        ],
        compiler_params=_cparams(3),
        name="mla_attention",
    )(qtf, kf, vtf, pm, *prev)


def _mlstm_kernel(qf_ref, kf_ref, vf_ref, gf_ref, qb_ref, kb_ref, vb_ref, gb_ref, bif_ref,
                  hf_ref, hb_ref, st_ref, m_ref, dm_ref, wi_ref, fl_ref, wr_ref, dc_ref, *, heads, log2_k_scale):
    L = ML_CHUNK
    dh = ML_HEAD_DIM

    @pl.when(pl.program_id(1) == 0)
    def _():
        st_ref[...] = jnp.zeros_like(st_ref)
        m_ref[...] = jnp.zeros_like(m_ref)

    assert L == LANES
    row = lax.broadcasted_iota(jnp.int32, (L, L), 0)
    col = lax.broadcasted_iota(jnp.int32, (L, L), 1)
    ones_blk = jnp.ones((L, LANES), BF16)

    def rep(column):
        return jnp.broadcast_to(column, (L, LANES))

    dirs = ((qf_ref, kf_ref, vf_ref, gf_ref, hf_ref, col <= row, L - 1),
            (qb_ref, kb_ref, vb_ref, gb_ref, hb_ref, col >= row, 0))
    for d, (q_ref, k_ref, v_ref, g_ref, h_ref, mask, last) in enumerate(dirs):
        g = g_ref[...] + bif_ref[...]
        lf = _log_sigmoid(g) * LOG2_E
        g = g * LOG2_E
        b_all = jnp.dot(mask.astype(F32), lf, preferred_element_type=F32,
                        precision=lax.Precision.HIGHEST)
        g_t = g.T
        b_t = b_all.T
        for h in range(heads):
            ci = 2 * heads * d + h
            cf = ci + heads
            idx = d * heads + h
            b_r = rep(b_all[:, cf:cf + 1])
            i_r = rep(g[:, ci:ci + 1])
            arow = g_t[ci:ci + 1, :] - b_t[cf:cf + 1, :]
            m_prev = m_ref[idx]
            logd = jnp.where(mask, b_r + arow, -jnp.inf)
            inter = b_r + m_prev
            m_t = jnp.maximum(inter, rep(jnp.max(logd, axis=1, keepdims=True)))
            dm_ref[idx] = jnp.exp2(logd - m_t)
            wi_ref[idx] = jnp.exp2(inter - m_t)
            fl_ref[idx] = jnp.exp2(-log2_k_scale - m_t)
            m_new = m_t[last:last + 1, :]
            b_last = b_r[last:last + 1, :]
            wr_ref[idx] = jnp.exp2(b_last - b_r + i_r - m_new)
            dc_ref[idx] = jnp.exp2(b_last + m_prev - m_new)
            m_ref[idx] = m_new

    for d, (q_ref, k_ref, v_ref, g_ref, h_ref, mask, last) in enumerate(dirs):
        for h in range(heads):
            idx = d * heads + h
            q = q_ref[:, h * dh:(h + 1) * dh]
            k = k_ref[:, h * dh:(h + 1) * dh]
            v = v_ref[:, h * dh:(h + 1) * dh]
            s = lax.dot_general(q, k, (((1,), (1,)), ((), ())), preferred_element_type=F32)
            s = (s * dm_ref[idx]).astype(BF16)
            st_prev = st_ref[idx]
            intra = jnp.dot(s, jnp.concatenate([v, ones_blk], axis=1), preferred_element_type=F32)
            inter_p = jnp.dot(q, st_prev.astype(BF16), preferred_element_type=F32)
            w_inter = wi_ref[idx]
            num = intra[:, :dh] + w_inter * inter_p[:, :dh]
            den = intra[:, dh:] + w_inter * inter_p[:, dh:]
            hout = num / jnp.maximum(jnp.abs(den), fl_ref[idx])
            h_ref[:, h * dh:(h + 1) * dh] = hout.astype(h_ref.dtype)
            w_r = wr_ref[idx]
            decay = dc_ref[idx]
            wv = jnp.concatenate([v.astype(F32) * w_r, w_r], axis=1).astype(BF16)
            upd = lax.dot_general(k, wv, (((0,), (0,)), ((), ())), preferred_element_type=F32)
            st_ref[idx] = jnp.concatenate([decay, decay], axis=1) * st_prev + upd


def _mlstm(pm, ps, bif, prev, *, t_total, row0, n_seq, seq, q_col0):
    heads = ML_HEADS
    width = heads * ML_HEAD_DIM
    L = ML_CHUNK
    nc = seq // L
    rb0 = row0 // L
    qc = q_col0 // width
    fwd = lambda n, c: rb0 + n * nc + c
    bwd = lambda n, c: rb0 + n * nc + (nc - 1 - c)

    def col_spec(chunk_of, cb):
        return pl.BlockSpec((L, width), lambda n, c: (chunk_of(n, c), cb))

    gate_f = pl.BlockSpec((L, LANES), lambda n, c: (fwd(n, c), 1))
    gate_b = pl.BlockSpec((L, LANES), lambda n, c: (bwd(n, c), 1))
    out_sds = jax.ShapeDtypeStruct((t_total, width), BF16)
    kern = functools.partial(_mlstm_kernel, heads=heads, log2_k_scale=-0.5 * math.log2(ML_HEAD_DIM))
    kern, prev_specs, aliases, prev = _with_carried_outputs(kern, 9, prev)
    return pl.pallas_call(
        kern,
        grid=(n_seq, nc),
        in_specs=[
            col_spec(fwd, qc), col_spec(fwd, qc + 1), col_spec(fwd, qc + 2), gate_f,
            col_spec(bwd, qc), col_spec(bwd, qc + 1), col_spec(bwd, qc + 2), gate_b,
            pl.BlockSpec((1, LANES), lambda n, c: (0, 0)),
        ] + prev_specs,
        out_specs=[
            pl.BlockSpec((L, width), lambda n, c: (fwd(n, c), 0)),
            pl.BlockSpec((L, width), lambda n, c: (bwd(n, c), 0)),
        ],
        out_shape=[out_sds, out_sds],
        input_output_aliases=aliases,
        scratch_shapes=[
            pltpu.VMEM((2 * heads, ML_HEAD_DIM, HEAD_LANES), F32),
            pltpu.VMEM((2 * heads, 1, LANES), F32),
            pltpu.VMEM((2 * heads, L, L), F32),
            pltpu.VMEM((2 * heads, L, LANES), F32),
            pltpu.VMEM((2 * heads, L, LANES), F32),
            pltpu.VMEM((2 * heads, L, LANES), F32),
            pltpu.VMEM((2 * heads, 1, LANES), F32),
        ],
        compiler_params=_cparams(2),
        name="mlstm_bidir",
    )(pm, pm, pm, ps, pm, pm, pm, ps, bif, *prev)


def _tail_kernel(x_ref, attn_ref, hf_ref, hb_ref, o_ref, za_ref, ga_ref, gb_ref, gate_ref,
                 mlw_ref, wpa_ref, wpb_ref, wo_ref, lnw_ref, lnb_ref, y_ref, *, heads, alpha):
    dh = ML_HEAD_DIM
    tm = x_ref.shape[0]
    sub = min(TAIL_SUB, tm)
    for r0 in range(0, tm, sub):
        rows = slice(r0, r0 + sub)
        y_b = jnp.dot(attn_ref[rows, :], wpb_ref[...], preferred_element_type=F32)
        hs = hf_ref[rows, :].astype(F32) + hb_ref[rows, :].astype(F32)
        hn = jnp.concatenate(
            [_layer_norm_rows(hs[:, h * dh:(h + 1) * dh]) for h in range(heads)], axis=1)
        a_in = (hn * mlw_ref[...] * _sigmoid(o_ref[rows, :].astype(F32))
                * _silu(za_ref[rows, :].astype(F32)))
        y_a = jnp.dot(a_in.astype(BF16), wpa_ref[...], preferred_element_type=F32)
        merged = (_sigmoid(ga_ref[rows, :].astype(F32)) * y_a
                  + _sigmoid(gb_ref[rows, :].astype(F32)) * y_b)
        out = jnp.dot(merged.astype(BF16), wo_ref[...], preferred_element_type=F32)
        r = alpha * x_ref[rows, :] + gate_ref[0] * out
        y_ref[rows, :] = _layer_norm_rows(r) * lnw_ref[...] + lnb_ref[...]


def _tail(x, attn, hf, hb, pm, gate, mlw, wpa, wpb, wo, lnw, lnb, prev, *, n_rows, x_row0, row0, out_rows,
          out_row0, seg, alpha, o_col0, za_col0, g_col0):
    d = x.shape[1]
    wa = hf.shape[1]
    tm = min(TAIL_TM, seg)
    xb0, gb0, ob0 = x_row0 // tm, row0 // tm, out_row0 // tm
    const = lambda shape: pl.BlockSpec(shape, lambda i: (0,) * len(shape), pipeline_mode=pl.Buffered(1))
    kern = functools.partial(_tail_kernel, heads=ML_HEADS, alpha=alpha)
    kern, prev_specs, aliases, prev = _with_carried_outputs(kern, 15, prev)
    return pl.pallas_call(
        kern,
        grid=(n_rows // tm,),
        in_specs=[
            pl.BlockSpec((tm, d), lambda i: (xb0 + i, 0)),
            pl.BlockSpec((tm, attn.shape[1]), lambda i: (gb0 + i, 0)),
            pl.BlockSpec((tm, wa), lambda i: (gb0 + i, 0)),
            pl.BlockSpec((tm, wa), lambda i: (gb0 + i, 0)),
            pl.BlockSpec((tm, wa), lambda i: (gb0 + i, o_col0 // wa)),
            pl.BlockSpec((tm, wa), lambda i: (gb0 + i, za_col0 // wa)),
            pl.BlockSpec((tm, d), lambda i: (gb0 + i, g_col0 // d)),
            pl.BlockSpec((tm, d), lambda i: (gb0 + i, g_col0 // d + 1)),
            pl.BlockSpec((1, 1, d), lambda i: ((row0 + i * tm) // seg, 0, 0)),
            const((1, wa)),
            const(wpa.shape),
            const(wpb.shape),
            const(wo.shape),
            const((1, d)),
            const((1, d)),
        ] + prev_specs,
        out_specs=pl.BlockSpec((tm, d), lambda i: (ob0 + i, 0)),
        out_shape=jax.ShapeDtypeStruct((out_rows, d), F32),
        input_output_aliases=aliases,
        compiler_params=_cparams(1),
        name="tail",
    )(x, attn, hf, hb, pm, pm, pm, pm, gate, mlw, wpa, wpb, wo, lnw, lnb, *prev)


def _rot_half(w):
    half = w.shape[-1] // 2
    return jnp.concatenate([-w[..., half:], w[..., :half]], axis=-1)


def _prep_weights(w_in, b_if, w_q_b, w_kv_b, w_proj_a, w_proj_b, w_out, d_model, q_rank, kv_rank):
    depth = w_in.shape[0]
    wa = ML_HEADS * ML_HEAD_DIM
    wb = MLA_HEADS * MLA_V_DIM
    sizes = [q_rank, kv_rank, MLA_ROPE_DIM, wb, wa, wa, wa, wa, wa, 4 * ML_HEADS, 2 * d_model]
    offs = [0]
    for s in sizes:
        offs.append(offs[-1] + s)
    part = lambda i: w_in[:, :, offs[i]:offs[i + 1]]
    (q_lat, kv_lat, k_rope, z_b, ml_q, ml_k, ml_v, ml_o, z_a, ml_g, merge_g) = [part(i) for i in range(11)]
    main_parts = [q_lat, kv_lat, z_b, ml_q, ml_k, ml_v, ml_o, z_a, merge_g]
    w_main = jnp.concatenate(main_parts, axis=-1).astype(BF16)
    names = ["q_lat", "kv_lat", "z_b", "ml_q", "ml_k", "ml_v", "ml_o", "z_a", "merge_g"]
    cols, o = {}, 0
    for nme, p in zip(names, main_parts):
        cols[nme] = o
        o += p.shape[-1]
    pad = jnp.zeros(w_in.shape[:2] + (LANES - 4 * ML_HEADS,), w_in.dtype)
    w_small = jnp.concatenate([k_rope, _rot_half(k_rope), ml_g, pad], axis=-1).astype(BF16)

    wq = w_q_b.reshape(depth, q_rank, MLA_HEADS, MLA_QK_DIM)
    wq_r = wq[..., MLA_NOPE_DIM:]
    wq = jnp.concatenate([wq[..., :MLA_NOPE_DIM], wq_r, _rot_half(wq_r)], axis=-1)
    wq = wq.reshape(depth, q_rank, MLA_HEADS * HEAD_LANES).astype(BF16)
    wkv = w_kv_b.reshape(depth, kv_rank, MLA_HEADS, MLA_NOPE_DIM + MLA_V_DIM)
    wk = wkv[..., :MLA_NOPE_DIM].reshape(depth, kv_rank, MLA_HEADS * MLA_NOPE_DIM).astype(BF16)
    wv = wkv[..., MLA_NOPE_DIM:].reshape(depth, kv_rank, MLA_HEADS * MLA_V_DIM).astype(BF16)
    bif = jnp.pad(b_if.reshape(depth, 1, 4 * ML_HEADS), ((0, 0), (0, 0), (0, LANES - 4 * ML_HEADS)))
    return dict(w_main=w_main, w_small=w_small, cols=cols, wq=wq, wk=wk, wv=wv, bif=bif,
                wpa=w_proj_a.astype(BF16), wpb=w_proj_b.astype(BF16), wo=w_out.astype(BF16))


def _rope_lane_tables(seq_len):
    inv_freq = ROPE_THETA ** (-jnp.arange(0, MLA_ROPE_DIM, 2, dtype=F32) / MLA_ROPE_DIM)
    ang = jnp.arange(seq_len, dtype=F32)[:, None] * inv_freq[None, :]
    zeros = jnp.zeros((seq_len, LANES - MLA_ROPE_DIM), F32)
    cos, sin = jnp.cos(ang), jnp.sin(ang)
    return (jnp.concatenate([cos, cos, zeros], axis=1), jnp.concatenate([sin, sin, zeros], axis=1))


def kernel(x_prompt, x_sample, c_prompt, c_sample, w_ada, b_ada, w_in, b_if, q_norm_w, kv_norm_w,
           w_q_b, w_kv_b, ml_norm_w, w_proj_a, w_proj_b, w_out, ln_w, ln_b):
    batch, seq, d = x_prompt.shape
    dec_batch, dec_seq, _ = x_sample.shape
    depth = w_ada.shape[0]
    q_rank = q_norm_w.shape[1]
    kv_rank = kv_norm_w.shape[1]
    alpha = (2 * depth) ** 0.25
    assert seq % dec_seq == 0 and dec_seq % ML_CHUNK == 0
    seg = dec_seq
    t_prompt = batch * seq
    t = t_prompt + dec_batch * dec_seq

    x_groups = (x_prompt.reshape(t_prompt, d), x_sample.reshape(dec_batch * dec_seq, d))
    c_all = jnp.concatenate([c_prompt, c_sample], axis=0)
    n_cond = c_all.shape[0]
    rows = -(-n_cond // 16) * 16
    c_pad = jnp.pad(c_all, ((0, rows - n_cond), (0, 0)))
    seg_cond = jnp.concatenate([jnp.repeat(jnp.arange(batch), seq // seg),
                                batch + jnp.arange(dec_batch)])

    mod = _adaln_mod(c_pad, w_ada, b_ada)
    mod = mod[:, seg_cond, :].reshape(depth, t // seg, 1, 3, d)
    shift, scale, gate = mod[..., 0, :], mod[..., 1, :], mod[..., 2, :]

    w = _prep_weights(w_in, b_if, w_q_b, w_kv_b, w_proj_a, w_proj_b, w_out, d, q_rank, kv_rank)
    cols = w["cols"]
    cs_tab, sn_tab = _rope_lane_tables(max(seq, dec_seq))
    groups = ((0, batch, seq), (t_prompt, dec_batch, dec_seq))

    x = None
    for l in range(depth):
        if l == 0:
            pmps = None
            for (row0, _, _), xg in zip(groups, x_groups):
                pmps = _inproj(xg, shift[l], scale[l], w["w_main"][l], w["w_small"][l], pmps,
                               seg=seg, t_total=t, row0=row0)
            pm, ps = pmps
        else:
            pm, ps = _inproj(x, shift[l], scale[l], w["w_main"][l], w["w_small"][l], None,
                             seg=seg, t_total=t, row0=0)
        qf, kf, vf = _qkv(pm, ps, cs_tab, sn_tab, q_norm_w[l][None], kv_norm_w[l][None],
                          w["wq"][l], w["wk"][l], w["wv"][l],
                          t_prompt=t_prompt, dec_seq=dec_seq, q_rank=q_rank, kv_rank=kv_rank,
                          tkv=min(ATTN_TK, dec_seq))
        attn, hfb = None, None
        for row0, n_seq, s_len in groups:
            attn = [_attention(qf, kf, vf, pm, attn, t_total=t, row0=row0, n_seq=n_seq, seq=s_len,
                               zb_col0=cols["z_b"])]
            hfb = _mlstm(pm, ps, w["bif"][l], hfb, t_total=t, row0=row0, n_seq=n_seq, seq=s_len,
                         q_col0=cols["ml_q"])
        tail = functools.partial(
            _tail, attn=attn[0], hf=hfb[0], hb=hfb[1], pm=pm, gate=gate[l], mlw=ml_norm_w[l][None],
            wpa=w["wpa"][l], wpb=w["wpb"][l], wo=w["wo"][l], lnw=ln_w[l][None], lnb=ln_b[l][None],
            seg=seg, alpha=alpha, o_col0=cols["ml_o"], za_col0=cols["z_a"], g_col0=cols["merge_g"])
        first, last = l == 0, l == depth - 1
        if not (first or last):
            x = tail(x=x, prev=None, n_rows=t, x_row0=0, row0=0, out_rows=t, out_row0=0)
            continue
        outs, carried = [], None
        for (row0, n_seq, s_len), xg in zip(groups, x_groups):
            n_rows = n_seq * s_len
            y = tail(x=xg if first else x, prev=None if last else carried, n_rows=n_rows,
                     x_row0=0 if first else row0, row0=row0,
                     out_rows=n_rows if last else t, out_row0=0 if last else row0)
            outs.append(y)
            carried = [y]
        x = None if last else outs[-1]

    y_prompt, y_sample = outs
    return (y_prompt.reshape(batch, seq, d), y_sample.reshape(dec_batch, dec_seq, d))
```

```python
import functools
import math

import jax
import jax.numpy as jnp
from jax import lax
from jax.experimental import pallas as pl
from jax.experimental.pallas import tpu as pltpu

F32 = jnp.float32
BF16 = jnp.bfloat16

MLA_HEADS = 16
MLA_NOPE_DIM = 128
MLA_ROPE_DIM = 64
MLA_V_DIM = 128
MLA_QK_DIM = MLA_NOPE_DIM + MLA_ROPE_DIM
ROPE_THETA = 10000.0
ML_HEADS = 8
ML_HEAD_DIM = 128
ML_CHUNK = 128
LN_EPS = 1e-5
RMS_EPS = 1e-6
LOG2_E = 1.4426950408889634

LANES = 128
HEAD_LANES = 2 * LANES
VT_ROWS = MLA_V_DIM + 16
VMEM_LIMIT_BYTES = 56 * 1024 * 1024

MOD_TN = 1024
INPROJ_TM = 1024
INPROJ_TN = 1024
QKV_TM = 256
ATTN_TQ = 2048
ATTN_TK = 512
ATTN_CB_PV = 256
TAIL_TM = 256
TAIL_SUB = 256


def _cparams(n_axes):
    return pltpu.CompilerParams(
        dimension_semantics=("arbitrary",) * n_axes,
        vmem_limit_bytes=VMEM_LIMIT_BYTES,
    )


def _sigmoid(x):
    return 1.0 / (1.0 + jnp.exp(-x))


def _silu(x):
    return x * _sigmoid(x)


def _log_sigmoid(x):
    return jnp.minimum(x, 0.0) - jnp.log(1.0 + jnp.exp(-jnp.abs(x)))


def _layer_norm_rows(x):
    mu = jnp.mean(x, axis=-1, keepdims=True)
    xc = x - mu
    var = jnp.mean(xc * xc, axis=-1, keepdims=True)
    return xc * lax.rsqrt(var + LN_EPS)


def _mod_kernel(c_ref, w_ref, b_ref, o_ref):
    c = c_ref[...]
    a = _silu(c).astype(BF16)
    o_ref[0] = jnp.dot(a, w_ref[0].astype(BF16), preferred_element_type=F32) + b_ref[0]


def _adaln_mod(c_pad, w_ada, b_ada):
    depth, d, n = w_ada.shape
    rows = c_pad.shape[0]
    tn = min(MOD_TN, n)
    return pl.pallas_call(
        _mod_kernel,
        grid=(depth, n // tn),
        in_specs=[
            pl.BlockSpec((rows, d), lambda l, j: (0, 0)),
            pl.BlockSpec((1, d, tn), lambda l, j: (l, 0, j)),
            pl.BlockSpec((1, 1, tn), lambda l, j: (l, 0, j)),
        ],
        out_specs=pl.BlockSpec((1, rows, tn), lambda l, j: (l, 0, j)),
        out_shape=jax.ShapeDtypeStruct((depth, rows, n), F32),
        compiler_params=_cparams(2),
        name="adaln_mod",
    )(c_pad, w_ada, b_ada.reshape(depth, 1, n))


def _inproj_kernel(x_ref, sh_ref, sc_ref, wm_ref, ws_ref, pm_ref, ps_ref, u_ref):
    @pl.when(pl.program_id(1) == 0)
    def _():
        u = _layer_norm_rows(x_ref[...]) * (1.0 + sc_ref[0]) + sh_ref[0]
        ub = u.astype(BF16)
        u_ref[...] = ub
        ps_ref[...] = jnp.dot(ub, ws_ref[...], preferred_element_type=F32)

    ub = u_ref[...]
    w_cols = wm_ref.shape[2]
    for c in range(wm_ref.shape[0]):
        pm_ref[:, c * w_cols:(c + 1) * w_cols] = jnp.dot(
            ub, wm_ref[c], preferred_element_type=F32).astype(BF16)


def _inproj(x, shift, scale, w_main, w_small, prev, *, seg, t_total, row0):
    t, d = x.shape
    n_slabs, _, w_cols = w_main.shape
    n = n_slabs * w_cols
    ns = w_small.shape[1]
    tm = min(INPROJ_TM, seg)
    tn = min(INPROJ_TN, n)
    assert tn % w_cols == 0
    rb0 = row0 // tm
    seg_of = lambda i, j: ((row0 + i * tm) // seg, 0, 0)
    kern, prev_specs, aliases, prev = _with_carried_outputs(_inproj_kernel, 5, prev)
    return pl.pallas_call(
        kern,
        grid=(t // tm, n // tn),
        in_specs=[
            pl.BlockSpec((tm, d), lambda i, j: (i, 0)),
            pl.BlockSpec((1, 1, d), seg_of),
            pl.BlockSpec((1, 1, d), seg_of),
            pl.BlockSpec((tn // w_cols, d, w_cols), lambda i, j: (j, 0, 0)),
            pl.BlockSpec((d, ns), lambda i, j: (0, 0)),
        ] + prev_specs,
        out_specs=[
            pl.BlockSpec((tm, tn), lambda i, j: (rb0 + i, j)),
            pl.BlockSpec((tm, ns), lambda i, j: (rb0 + i, 0)),
        ],
        out_shape=[
            jax.ShapeDtypeStruct((t_total, n), BF16),
            jax.ShapeDtypeStruct((t_total, ns), F32),
        ],
        input_output_aliases=aliases,
        scratch_shapes=[pltpu.VMEM((tm, d), BF16)],
        compiler_params=_cparams(2),
        name="inproj",
    )(x, shift, scale, w_main, w_small, *prev)


def _rope_lanes(x, cs, sn):
    return x * cs + pltpu.roll(x, LANES // 2, axis=1) * sn


def _qkv_kernel(ql_ref, kvl_ref, ps_ref, cs_ref, sn_ref, qnw_ref, kvnw_ref, wq_ref, wk_ref, wv_ref,
                qt_out, k_out, vt_out, *, heads, qk_scale):
    def rms(v, w):
        return (v * lax.rsqrt(jnp.mean(v * v, axis=-1, keepdims=True) + RMS_EPS) * w).astype(BF16)

    qn = rms(ql_ref[...].astype(F32), qnw_ref[...])
    kvn = rms(kvl_ref[...].astype(F32), kvnw_ref[...])
    cs = cs_ref[...]
    sn = sn_ref[...]
    kr = _rope_lanes(ps_ref[...], cs, sn).astype(BF16)
    tm = kr.shape[0]
    sub = lax.broadcasted_iota(jnp.int32, (VT_ROWS - MLA_V_DIM, tm), 0)
    ones_row = jnp.where(sub == 0, 1.0, 0.0).astype(BF16)
    for h in range(heads):
        qa = jnp.dot(qn, wq_ref[:, h * HEAD_LANES:(h + 1) * HEAD_LANES], preferred_element_type=F32)
        qt_out[h, 0, :LANES, :] = (qa[:, :LANES] * qk_scale).T.astype(BF16)
        qt_out[h, 0, LANES:, :] = (_rope_lanes(qa[:, LANES:], cs, sn) * qk_scale).T.astype(BF16)
        k_out[h, :, LANES:] = kr
        vt_out[h, 0, MLA_V_DIM:, :] = ones_row
    for h in range(0, heads, 2):
        ka = jnp.dot(kvn, wk_ref[:, h * LANES:(h + 2) * LANES], preferred_element_type=F32)
        va = jnp.dot(kvn, wv_ref[:, h * LANES:(h + 2) * LANES], preferred_element_type=F32)
        for e in range(2):
            k_out[h + e, :, :LANES] = ka[:, e * LANES:(e + 1) * LANES].astype(BF16)
            vt_out[h + e, 0, :MLA_V_DIM, :] = va[:, e * LANES:(e + 1) * LANES].T.astype(BF16)


def _qkv(pm, ps, cs_tab, sn_tab, qnw, kvnw, wq, wk, wv, *, t_prompt, dec_seq, q_rank, kv_rank, tkv):
    t = pm.shape[0]
    heads = MLA_HEADS
    tm = min(QKV_TM, tkv)
    assert q_rank == kv_rank and q_rank % LANES == 0 and tkv % tm == 0 and dec_seq % tkv == 0
    per_kv = tkv // tm
    np_blocks = t_prompt // tm
    dec_blocks = dec_seq // tm

    def pos_block(i):
        return (jnp.where(i < np_blocks, i, (i - np_blocks) % dec_blocks), 0)

    out_sds = jax.ShapeDtypeStruct((heads, t, HEAD_LANES), BF16)
    out_spec = pl.BlockSpec((heads, tm, HEAD_LANES), lambda i: (0, i, 0))
    kern = functools.partial(_qkv_kernel, heads=heads, qk_scale=MLA_QK_DIM ** -0.5 * LOG2_E)
    return pl.pallas_call(
        kern,
        grid=(t // tm,),
        in_specs=[
            pl.BlockSpec((tm, q_rank), lambda i: (i, 0)),
            pl.BlockSpec((tm, kv_rank), lambda i: (i, 1)),
            pl.BlockSpec((tm, LANES), lambda i: (i, 0)),
            pl.BlockSpec((tm, LANES), pos_block),
            pl.BlockSpec((tm, LANES), pos_block),
            pl.BlockSpec((1, q_rank), lambda i: (0, 0)),
            pl.BlockSpec((1, kv_rank), lambda i: (0, 0)),
            pl.BlockSpec(wq.shape, lambda i: (0, 0)),
            pl.BlockSpec(wk.shape, lambda i: (0, 0)),
            pl.BlockSpec(wv.shape, lambda i: (0, 0)),
        ],
        out_specs=[pl.BlockSpec((heads, 1, HEAD_LANES, tm), lambda i: (0, i, 0, 0)), out_spec,
                   pl.BlockSpec((heads, 1, VT_ROWS, tm), lambda i: (0, i // per_kv, 0, i % per_kv))],
        out_shape=[jax.ShapeDtypeStruct((heads, t // tm, HEAD_LANES, tm), BF16), out_sds,
                   jax.ShapeDtypeStruct((heads, t // tkv, VT_ROWS, tkv), BF16)],
        compiler_params=_cparams(1),
        name="qkv_up",
    )(pm, pm, ps, cs_tab, sn_tab, qnw, kvnw, wq, wk, wv)


def _attn_kernel(q_ref, k_ref, vt_ref, zb_ref, o_ref, s0_ref, s1_ref, x0_ref, x1_ref, p0_ref, p1_ref,
                 a0_ref, a1_ref, m_ref, acc_ref, *, tk, nk, cb_pv):
    n_slab, _, slab = q_ref.shape[1:]
    tq = n_slab * slab
    per_slab = slab // LANES

    def scores(j, s_ref, x_ref):
        start = pl.multiple_of(j * tk, tk)
        kt = k_ref[0, pl.ds(start, tk), :]
        for b in range(n_slab):
            st = jnp.dot(kt, q_ref[0, b], preferred_element_type=F32)
            for c in range(per_slab):
                s_ref[b * per_slab + c] = st[:, c * LANES:(c + 1) * LANES]
            x_ref[:, b * slab:(b + 1) * slab] = jnp.broadcast_to(
                jnp.max(st, axis=0, keepdims=True), (x_ref.shape[0], slab))

    def softmax(s_ref, x_ref, p_ref, a_ref):
        for c in range(tq // LANES):
            cols = slice(c * LANES, (c + 1) * LANES)
            m_prev = m_ref[:, cols]
            m_new = jnp.maximum(m_prev, x_ref[:, cols])
            p_ref[c] = jnp.exp2(s_ref[c] - m_new[0:1, :]).astype(BF16)
            a_ref[:, cols] = jnp.exp2(m_prev - m_new)
            m_ref[:, cols] = m_new

    def pv(j, p_ref, a_ref):
        vt = vt_ref[0, j]
        per = cb_pv // LANES
        for c in range(tq // cb_pv):
            cols = slice(c * cb_pv, (c + 1) * cb_pv)
            pt = jnp.concatenate([p_ref[c * per + i] for i in range(per)], axis=1)
            upd = jnp.dot(vt, pt, preferred_element_type=F32)
            acc_ref[c] = acc_ref[c] * a_ref[0:1, cols] + upd

    m_ref[...] = jnp.full(m_ref.shape, -jnp.inf, F32)
    acc_ref[...] = jnp.zeros(acc_ref.shape, F32)
    scores(0, s0_ref, x0_ref)
    scores(1, s1_ref, x1_ref)
    softmax(s0_ref, x0_ref, p0_ref, a0_ref)

    def body(jj, carry):
        j = 2 * jj
        scores(j + 2, s0_ref, x0_ref)
        softmax(s1_ref, x1_ref, p1_ref, a1_ref)
        pv(j, p0_ref, a0_ref)
        scores(j + 3, s1_ref, x1_ref)
        softmax(s0_ref, x0_ref, p0_ref, a0_ref)
        pv(j + 1, p1_ref, a1_ref)
        return carry

    lax.fori_loop(0, nk // 2 - 1, body, 0)
    softmax(s1_ref, x1_ref, p1_ref, a1_ref)
    pv(nk - 2, p0_ref, a0_ref)
    pv(nk - 1, p1_ref, a1_ref)
    for c in range(tq // cb_pv):
        rows = slice(c * cb_pv, (c + 1) * cb_pv)
        o = (acc_ref[c, :MLA_V_DIM, :] / acc_ref[c, MLA_V_DIM:MLA_V_DIM + 1, :]).T
        o_ref[rows, :] = (o * _silu(zb_ref[rows, :].astype(F32))).astype(BF16)


def _with_carried_outputs(kern, n_in, prev):
    if prev is None:
        return kern, [], {}, ()

    def body(*refs):
        return kern(*refs[:n_in], *refs[n_in + len(prev):])

    specs = [pl.BlockSpec(memory_space=pl.ANY)] * len(prev)
    return body, specs, {n_in + k: k for k in range(len(prev))}, tuple(prev)


def _attention(qtf, kf, vtf, pm, prev, *, t_total, row0, n_seq, seq, zb_col0):
    heads, _, _, slab = qtf.shape
    tk = vtf.shape[3]
    tq = min(ATTN_TQ, seq)
    cb_pv = min(ATTN_CB_PV, tq)
    assert tq % slab == 0 and slab % LANES == 0
    nq = seq // tq
    qb0 = row0 // tq
    sb0 = row0 // seq
    zc0 = zb_col0 // LANES
    nk = seq // tk
    assert nk % 2 == 0
    kern = functools.partial(_attn_kernel, tk=tk, nk=nk, cb_pv=cb_pv)
    kern, prev_specs, aliases, prev = _with_carried_outputs(kern, 4, prev)
    return pl.pallas_call(
        kern,
        grid=(n_seq, heads, nq),
        in_specs=[
            pl.BlockSpec((1, tq // slab, HEAD_LANES, slab), lambda n, h, i: (h, qb0 + n * nq + i, 0, 0)),
            pl.BlockSpec((1, seq, HEAD_LANES), lambda n, h, i: (h, sb0 + n, 0)),
            pl.BlockSpec((1, nk, VT_ROWS, tk), lambda n, h, i: (h, sb0 + n, 0, 0)),
            pl.BlockSpec((tq, LANES), lambda n, h, i: (qb0 + n * nq + i, zc0 + h)),
        ] + prev_specs,
        out_specs=pl.BlockSpec((tq, LANES), lambda n, h, i: (qb0 + n * nq + i, h)),
        out_shape=jax.ShapeDtypeStruct((t_total, heads * LANES), BF16),
        input_output_aliases=aliases,
        scratch_shapes=[
            pltpu.VMEM((tq // LANES, tk, LANES), F32),
            pltpu.VMEM((tq // LANES, tk, LANES), F32),
            pltpu.VMEM((8, tq), F32),
            pltpu.VMEM((8, tq), F32),
            pltpu.VMEM((tq // LANES, tk, LANES), BF16),
            pltpu.VMEM((tq // LANES, tk, LANES), BF16),
            pltpu.VMEM((8, tq), F32),
            pltpu.VMEM((8, tq), F32),
            pltpu.VMEM((8, tq), F32),
            pltpu.VMEM((tq // cb_pv, VT_ROWS, cb_pv), F32),
        ],
        compiler_params=_cparams(3),
        name="mla_attention",
    )(qtf, kf, vtf, pm, *prev)


def _mlstm_kernel(qf_ref, kf_ref, vf_ref, gf_ref, qb_ref, kb_ref, vb_ref, gb_ref, bif_ref,
                  hf_ref, hb_ref, st_ref, m_ref, dm_ref, wi_ref, fl_ref, wr_ref, dc_ref, *, heads, log2_k_scale):
    L = ML_CHUNK
    dh = ML_HEAD_DIM

    @pl.when(pl.program_id(1) == 0)
    def _():
        st_ref[...] = jnp.zeros_like(st_ref)
        m_ref[...] = jnp.zeros_like(m_ref)

    assert L == LANES
    row = lax.broadcasted_iota(jnp.int32, (L, L), 0)
    col = lax.broadcasted_iota(jnp.int32, (L, L), 1)
    ones_blk = jnp.ones((L, LANES), BF16)

    def rep(column):
        return jnp.broadcast_to(column, (L, LANES))

    dirs = ((qf_ref, kf_ref, vf_ref, gf_ref, hf_ref, col <= row, L - 1),
            (qb_ref, kb_ref, vb_ref, gb_ref, hb_ref, col >= row, 0))
    for d, (q_ref, k_ref, v_ref, g_ref, h_ref, mask, last) in enumerate(dirs):
        g = g_ref[...] + bif_ref[...]
        lf = _log_sigmoid(g) * LOG2_E
        g = g * LOG2_E
        b_all = jnp.dot(mask.astype(F32), lf, preferred_element_type=F32,
                        precision=lax.Precision.HIGHEST)
        g_t = g.T
        b_t = b_all.T
        for h in range(heads):
            ci = 2 * heads * d + h
            cf = ci + heads
            idx = d * heads + h
            b_r = rep(b_all[:, cf:cf + 1])
            i_r = rep(g[:, ci:ci + 1])
            arow = g_t[ci:ci + 1, :] - b_t[cf:cf + 1, :]
            m_prev = m_ref[idx]
            logd = jnp.where(mask, b_r + arow, -jnp.inf)
            inter = b_r + m_prev
            m_t = jnp.maximum(inter, rep(jnp.max(logd, axis=1, keepdims=True)))
            dm_ref[idx] = jnp.exp2(logd - m_t)
            wi_ref[idx] = jnp.exp2(inter - m_t)
            fl_ref[idx] = jnp.exp2(-log2_k_scale - m_t)
            m_new = m_t[last:last + 1, :]
            b_last = b_r[last:last + 1, :]
            wr_ref[idx] = jnp.exp2(b_last - b_r + i_r - m_new)
            dc_ref[idx] = jnp.exp2(b_last + m_prev - m_new)
            m_ref[idx] = m_new

    for d, (q_ref, k_ref, v_ref, g_ref, h_ref, mask, last) in enumerate(dirs):
        for h in range(heads):
            idx = d * heads + h
            q = q_ref[:, h * dh:(h + 1) * dh]
            k = k_ref[:, h * dh:(h + 1) * dh]
            v = v_ref[:, h * dh:(h + 1) * dh]
            s = lax.dot_general(q, k, (((1,), (1,)), ((), ())), preferred_element_type=F32)
            s = (s * dm_ref[idx]).astype(BF16)
            st_prev = st_ref[idx]
            intra = jnp.dot(s, jnp.concatenate([v, ones_blk], axis=1), preferred_element_type=F32)
            inter_p = jnp.dot(q, st_prev.astype(BF16), preferred_element_type=F32)
            w_inter = wi_ref[idx]
            num = intra[:, :dh] + w_inter * inter_p[:, :dh]
            den = intra[:, dh:] + w_inter * inter_p[:, dh:]
            hout = num / jnp.maximum(jnp.abs(den), fl_ref[idx])
            h_ref[:, h * dh:(h + 1) * dh] = hout.astype(h_ref.dtype)
            w_r = wr_ref[idx]
            decay = dc_ref[idx]
            wv = jnp.concatenate([v.astype(F32) * w_r, w_r], axis=1).astype(BF16)
            upd = lax.dot_general(k, wv, (((0,), (0,)), ((), ())), preferred_element_type=F32)
            st_ref[idx] = jnp.concatenate([decay, decay], axis=1) * st_prev + upd


def _mlstm(pm, ps, bif, prev, *, t_total, row0, n_seq, seq, q_col0):
    heads = ML_HEADS
    width = heads * ML_HEAD_DIM
    L = ML_CHUNK
    nc = seq // L
    rb0 = row0 // L
    qc = q_col0 // width
    fwd = lambda n, c: rb0 + n * nc + c
    bwd = lambda n, c: rb0 + n * nc + (nc - 1 - c)

    def col_spec(chunk_of, cb):
        return pl.BlockSpec((L, width), lambda n, c: (chunk_of(n, c), cb))

    gate_f = pl.BlockSpec((L, LANES), lambda n, c: (fwd(n, c), 1))
    gate_b = pl.BlockSpec((L, LANES), lambda n, c: (bwd(n, c), 1))
    out_sds = jax.ShapeDtypeStruct((t_total, width), BF16)
    kern = functools.partial(_mlstm_kernel, heads=heads, log2_k_scale=-0.5 * math.log2(ML_HEAD_DIM))
    kern, prev_specs, aliases, prev = _with_carried_outputs(kern, 9, prev)
    return pl.pallas_call(
        kern,
        grid=(n_seq, nc),
        in_specs=[
            col_spec(fwd, qc), col_spec(fwd, qc + 1), col_spec(fwd, qc + 2), gate_f,
            col_spec(bwd, qc), col_spec(bwd, qc + 1), col_spec(bwd, qc + 2), gate_b,
            pl.BlockSpec((1, LANES), lambda n, c: (0, 0)),
        ] + prev_specs,
        out_specs=[
            pl.BlockSpec((L, width), lambda n, c: (fwd(n, c), 0)),
            pl.BlockSpec((L, width), lambda n, c: (bwd(n, c), 0)),
        ],
        out_shape=[out_sds, out_sds],
        input_output_aliases=aliases,
        scratch_shapes=[
            pltpu.VMEM((2 * heads, ML_HEAD_DIM, HEAD_LANES), F32),
            pltpu.VMEM((2 * heads, 1, LANES), F32),
            pltpu.VMEM((2 * heads, L, L), F32),
            pltpu.VMEM((2 * heads, L, LANES), F32),
            pltpu.VMEM((2 * heads, L, LANES), F32),
            pltpu.VMEM((2 * heads, L, LANES), F32),
            pltpu.VMEM((2 * heads, 1, LANES), F32),
        ],
        compiler_params=_cparams(2),
        name="mlstm_bidir",
    )(pm, pm, pm, ps, pm, pm, pm, ps, bif, *prev)


def _tail_kernel(x_ref, attn_ref, hf_ref, hb_ref, o_ref, za_ref, ga_ref, gb_ref, gate_ref,
                 mlw_ref, wpa_ref, wpb_ref, wo_ref, lnw_ref, lnb_ref, y_ref, *, heads, alpha):
    dh = ML_HEAD_DIM
    tm = x_ref.shape[0]
    sub = min(TAIL_SUB, tm)
    for r0 in range(0, tm, sub):
        rows = slice(r0, r0 + sub)
        hs = hf_ref[rows, :].astype(F32) + hb_ref[rows, :].astype(F32)
        hn = jnp.concatenate(
            [_layer_norm_rows(hs[:, h * dh:(h + 1) * dh]) for h in range(heads)], axis=1)
        a_in = (hn * mlw_ref[...] * _sigmoid(o_ref[rows, :].astype(F32))
                * _silu(za_ref[rows, :].astype(F32)))
        y_a = jnp.dot(a_in.astype(BF16), wpa_ref[...], preferred_element_type=F32)
        y_b = jnp.dot(attn_ref[rows, :], wpb_ref[...], preferred_element_type=F32)
        merged = (_sigmoid(ga_ref[rows, :].astype(F32)) * y_a
                  + _sigmoid(gb_ref[rows, :].astype(F32)) * y_b)
        out = jnp.dot(merged.astype(BF16), wo_ref[...], preferred_element_type=F32)
        r = alpha * x_ref[rows, :] + gate_ref[0] * out
        y_ref[rows, :] = _layer_norm_rows(r) * lnw_ref[...] + lnb_ref[...]


def _tail(x, attn, hf, hb, pm, gate, mlw, wpa, wpb, wo, lnw, lnb, prev, *, n_rows, x_row0, row0, out_rows,
          out_row0, seg, alpha, o_col0, za_col0, g_col0):
    d = x.shape[1]
    wa = hf.shape[1]
    tm = min(TAIL_TM, seg)
    xb0, gb0, ob0 = x_row0 // tm, row0 // tm, out_row0 // tm
    const = lambda shape: pl.BlockSpec(shape, lambda i: (0,) * len(shape), pipeline_mode=pl.Buffered(1))
    kern = functools.partial(_tail_kernel, heads=ML_HEADS, alpha=alpha)
    kern, prev_specs, aliases, prev = _with_carried_outputs(kern, 15, prev)
    return pl.pallas_call(
        kern,
        grid=(n_rows // tm,),
        in_specs=[
            pl.BlockSpec((tm, d), lambda i: (xb0 + i, 0)),
            pl.BlockSpec((tm, attn.shape[1]), lambda i: (gb0 + i, 0)),
            pl.BlockSpec((tm, wa), lambda i: (gb0 + i, 0)),
            pl.BlockSpec((tm, wa), lambda i: (gb0 + i, 0)),
            pl.BlockSpec((tm, wa), lambda i: (gb0 + i, o_col0 // wa)),
            pl.BlockSpec((tm, wa), lambda i: (gb0 + i, za_col0 // wa)),
            pl.BlockSpec((tm, d), lambda i: (gb0 + i, g_col0 // d)),
            pl.BlockSpec((tm, d), lambda i: (gb0 + i, g_col0 // d + 1)),
            pl.BlockSpec((1, 1, d), lambda i: ((row0 + i * tm) // seg, 0, 0)),
            const((1, wa)),
            const(wpa.shape),
            const(wpb.shape),
            const(wo.shape),
            const((1, d)),
            const((1, d)),
        ] + prev_specs,
        out_specs=pl.BlockSpec((tm, d), lambda i: (ob0 + i, 0)),
        out_shape=jax.ShapeDtypeStruct((out_rows, d), F32),
        input_output_aliases=aliases,
        compiler_params=_cparams(1),
        name="tail",
    )(x, attn, hf, hb, pm, pm, pm, pm, gate, mlw, wpa, wpb, wo, lnw, lnb, *prev)


def _rot_half(w):
    half = w.shape[-1] // 2
    return jnp.concatenate([-w[..., half:], w[..., :half]], axis=-1)


def _prep_weights(w_in, b_if, w_q_b, w_kv_b, w_proj_a, w_proj_b, w_out, d_model, q_rank, kv_rank):
    depth = w_in.shape[0]
    wa = ML_HEADS * ML_HEAD_DIM
    wb = MLA_HEADS * MLA_V_DIM
    sizes = [q_rank, kv_rank, MLA_ROPE_DIM, wb, wa, wa, wa, wa, wa, 4 * ML_HEADS, 2 * d_model]
    offs = [0]
    for s in sizes:
        offs.append(offs[-1] + s)
    part = lambda i: w_in[:, :, offs[i]:offs[i + 1]]
    (q_lat, kv_lat, k_rope, z_b, ml_q, ml_k, ml_v, ml_o, z_a, ml_g, merge_g) = [part(i) for i in range(11)]
    main_parts = [q_lat, kv_lat, z_b, ml_q, ml_k, ml_v, ml_o, z_a, merge_g]
    w_main = jnp.concatenate(main_parts, axis=-1).astype(BF16)
    w_main = w_main.reshape(depth, d_model, -1, HEAD_LANES).transpose(0, 2, 1, 3)
    names = ["q_lat", "kv_lat", "z_b", "ml_q", "ml_k", "ml_v", "ml_o", "z_a", "merge_g"]
    cols, o = {}, 0
    for nme, p in zip(names, main_parts):
        cols[nme] = o
        o += p.shape[-1]
    pad = jnp.zeros(w_in.shape[:2] + (LANES - 4 * ML_HEADS,), w_in.dtype)
    w_small = jnp.concatenate([k_rope, _rot_half(k_rope), ml_g, pad], axis=-1).astype(BF16)

    wq = w_q_b.reshape(depth, q_rank, MLA_HEADS, MLA_QK_DIM)
    wq_r = wq[..., MLA_NOPE_DIM:]
    wq = jnp.concatenate([wq[..., :MLA_NOPE_DIM], wq_r, _rot_half(wq_r)], axis=-1)
    wq = wq.reshape(depth, q_rank, MLA_HEADS * HEAD_LANES).astype(BF16)
    wkv = w_kv_b.reshape(depth, kv_rank, MLA_HEADS, MLA_NOPE_DIM + MLA_V_DIM)
    wk = wkv[..., :MLA_NOPE_DIM].reshape(depth, kv_rank, MLA_HEADS * MLA_NOPE_DIM).astype(BF16)
    wv = wkv[..., MLA_NOPE_DIM:].reshape(depth, kv_rank, MLA_HEADS * MLA_V_DIM).astype(BF16)
    bif = jnp.pad(b_if.reshape(depth, 1, 4 * ML_HEADS), ((0, 0), (0, 0), (0, LANES - 4 * ML_HEADS)))
    return dict(w_main=w_main, w_small=w_small, cols=cols, wq=wq, wk=wk, wv=wv, bif=bif,
                wpa=w_proj_a.astype(BF16), wpb=w_proj_b.astype(BF16), wo=w_out.astype(BF16))


def _rope_lane_tables(seq_len):
    inv_freq = ROPE_THETA ** (-jnp.arange(0, MLA_ROPE_DIM, 2, dtype=F32) / MLA_ROPE_DIM)
    ang = jnp.arange(seq_len, dtype=F32)[:, None] * inv_freq[None, :]
    zeros = jnp.zeros((seq_len, LANES - MLA_ROPE_DIM), F32)
    cos, sin = jnp.cos(ang), jnp.sin(ang)
    return (jnp.concatenate([cos, cos, zeros], axis=1), jnp.concatenate([sin, sin, zeros], axis=1))


def kernel(x_prompt, x_sample, c_prompt, c_sample, w_ada, b_ada, w_in, b_if, q_norm_w, kv_norm_w,
           w_q_b, w_kv_b, ml_norm_w, w_proj_a, w_proj_b, w_out, ln_w, ln_b):
    batch, seq, d = x_prompt.shape
    dec_batch, dec_seq, _ = x_sample.shape
    depth = w_ada.shape[0]
    q_rank = q_norm_w.shape[1]
    kv_rank = kv_norm_w.shape[1]
    alpha = (2 * depth) ** 0.25
    assert seq % dec_seq == 0 and dec_seq % ML_CHUNK == 0
    seg = dec_seq
    t_prompt = batch * seq
    t = t_prompt + dec_batch * dec_seq

    x_groups = (x_prompt.reshape(t_prompt, d), x_sample.reshape(dec_batch * dec_seq, d))
    c_all = jnp.concatenate([c_prompt, c_sample], axis=0)
    n_cond = c_all.shape[0]
    rows = -(-n_cond // 16) * 16
    c_pad = jnp.pad(c_all, ((0, rows - n_cond), (0, 0)))
    seg_cond = jnp.concatenate([jnp.repeat(jnp.arange(batch), seq // seg),
                                batch + jnp.arange(dec_batch)])

    mod = _adaln_mod(c_pad, w_ada, b_ada)
    mod = mod[:, seg_cond, :].reshape(depth, t // seg, 1, 3, d)
    shift, scale, gate = mod[..., 0, :], mod[..., 1, :], mod[..., 2, :]

    w = _prep_weights(w_in, b_if, w_q_b, w_kv_b, w_proj_a, w_proj_b, w_out, d, q_rank, kv_rank)
    cols = w["cols"]
    cs_tab, sn_tab = _rope_lane_tables(max(seq, dec_seq))
    groups = ((0, batch, seq), (t_prompt, dec_batch, dec_seq))

    x = None
    for l in range(depth):
        if l == 0:
            pmps = None
            for (row0, _, _), xg in zip(groups, x_groups):
                pmps = _inproj(xg, shift[l], scale[l], w["w_main"][l], w["w_small"][l], pmps,
                               seg=seg, t_total=t, row0=row0)
            pm, ps = pmps
        else:
            pm, ps = _inproj(x, shift[l], scale[l], w["w_main"][l], w["w_small"][l], None,
                             seg=seg, t_total=t, row0=0)
        qf, kf, vf = _qkv(pm, ps, cs_tab, sn_tab, q_norm_w[l][None], kv_norm_w[l][None],
                          w["wq"][l], w["wk"][l], w["wv"][l],
                          t_prompt=t_prompt, dec_seq=dec_seq, q_rank=q_rank, kv_rank=kv_rank,
                          tkv=min(ATTN_TK, dec_seq))
        attn, hfb = None, None
        for row0, n_seq, s_len in groups:
            attn = [_attention(qf, kf, vf, pm, attn, t_total=t, row0=row0, n_seq=n_seq, seq=s_len,
                               zb_col0=cols["z_b"])]
            hfb = _mlstm(pm, ps, w["bif"][l], hfb, t_total=t, row0=row0, n_seq=n_seq, seq=s_len,
                         q_col0=cols["ml_q"])
        tail = functools.partial(
            _tail, attn=attn[0], hf=hfb[0], hb=hfb[1], pm=pm, gate=gate[l], mlw=ml_norm_w[l][None],
            wpa=w["wpa"][l], wpb=w["wpb"][l], wo=w["wo"][l], lnw=ln_w[l][None], lnb=ln_b[l][None],
            seg=seg, alpha=alpha, o_col0=cols["ml_o"], za_col0=cols["z_a"], g_col0=cols["merge_g"])
        first, last = l == 0, l == depth - 1
        if not (first or last):
            x = tail(x=x, prev=None, n_rows=t, x_row0=0, row0=0, out_rows=t, out_row0=0)
            continue
        outs, carried = [], None
        for (row0, n_seq, s_len), xg in zip(groups, x_groups):
            n_rows = n_seq * s_len
            y = tail(x=xg if first else x, prev=None if last else carried, n_rows=n_rows,
                     x_row0=0 if first else row0, row0=row0,
                     out_rows=n_rows if last else t, out_row0=0 if last else row0)
            outs.append(y)
            carried = [y]
        x = None if last else outs[-1]

    y_prompt, y_sample = outs
    return (y_prompt.reshape(batch, seq, d), y_sample.reshape(dec_batch, dec_seq, d))
```

```python
import functools
import math

import jax
import jax.numpy as jnp
from jax import lax
from jax.experimental import pallas as pl
from jax.experimental.pallas import tpu as pltpu

F32 = jnp.float32
BF16 = jnp.bfloat16

MLA_HEADS = 16
MLA_NOPE_DIM = 128
MLA_ROPE_DIM = 64
MLA_V_DIM = 128
MLA_QK_DIM = MLA_NOPE_DIM + MLA_ROPE_DIM
ROPE_THETA = 10000.0
ML_HEADS = 8
ML_HEAD_DIM = 128
ML_CHUNK = 128
LN_EPS = 1e-5
RMS_EPS = 1e-6
LOG2_E = 1.4426950408889634

LANES = 128
HEAD_LANES = 2 * LANES
VT_ROWS = MLA_V_DIM + 16
VMEM_LIMIT_BYTES = 56 * 1024 * 1024
ATTN_VMEM_BUDGET = 44 * 1024 * 1024

MOD_TN = 1024
INPROJ_TM = 1024
INPROJ_TN = 1024
QKV_TM = 256
ATTN_TQ = 2048
ATTN_TK = 512
ATTN_CB_PV = 256
TAIL_TM = 256
TAIL_SUB = 256


def _cparams(n_axes):
    return pltpu.CompilerParams(
        dimension_semantics=("arbitrary",) * n_axes,
        vmem_limit_bytes=VMEM_LIMIT_BYTES,
    )


def _sigmoid(x):
    return 1.0 / (1.0 + jnp.exp(-x))


def _silu(x):
    return x * _sigmoid(x)


def _log_sigmoid(x):
    return jnp.minimum(x, 0.0) - jnp.log(1.0 + jnp.exp(-jnp.abs(x)))


def _layer_norm_rows(x):
    mu = jnp.mean(x, axis=-1, keepdims=True)
    xc = x - mu
    var = jnp.mean(xc * xc, axis=-1, keepdims=True)
    return xc * lax.rsqrt(var + LN_EPS)


def _mod_kernel(c_ref, w_ref, b_ref, o_ref):
    c = c_ref[...]
    a = _silu(c).astype(BF16)
    o_ref[0] = jnp.dot(a, w_ref[0].astype(BF16), preferred_element_type=F32) + b_ref[0]


def _adaln_mod(c_pad, w_ada, b_ada):
    depth, d, n = w_ada.shape
    rows = c_pad.shape[0]
    tn = min(MOD_TN, n)
    return pl.pallas_call(
        _mod_kernel,
        grid=(depth, n // tn),
        in_specs=[
            pl.BlockSpec((rows, d), lambda l, j: (0, 0)),
            pl.BlockSpec((1, d, tn), lambda l, j: (l, 0, j)),
            pl.BlockSpec((1, 1, tn), lambda l, j: (l, 0, j)),
        ],
        out_specs=pl.BlockSpec((1, rows, tn), lambda l, j: (l, 0, j)),
        out_shape=jax.ShapeDtypeStruct((depth, rows, n), F32),
        compiler_params=_cparams(2),
        name="adaln_mod",
    )(c_pad, w_ada, b_ada.reshape(depth, 1, n))


def _inproj_kernel(x_ref, sh_ref, sc_ref, wm_ref, ws_ref, pm_ref, ps_ref, u_ref):
    @pl.when(pl.program_id(1) == 0)
    def _():
        u = _layer_norm_rows(x_ref[...]) * (1.0 + sc_ref[0]) + sh_ref[0]
        ub = u.astype(BF16)
        u_ref[...] = ub
        ps_ref[...] = jnp.dot(ub, ws_ref[...], preferred_element_type=F32)

    pm_ref[...] = jnp.dot(u_ref[...], wm_ref[...], preferred_element_type=F32).astype(BF16)


def _inproj(x, shift, scale, w_main, w_small, prev, *, seg, t_total, row0):
    t, d = x.shape
    n = w_main.shape[1]
    ns = w_small.shape[1]
    tm = min(INPROJ_TM, seg)
    tn = min(INPROJ_TN, n)
    rb0 = row0 // tm
    seg_of = lambda i, j: ((row0 + i * tm) // seg, 0, 0)
    kern, prev_specs, aliases, prev = _with_carried_outputs(_inproj_kernel, 5, prev)
    return pl.pallas_call(
        kern,
        grid=(t // tm, n // tn),
        in_specs=[
            pl.BlockSpec((tm, d), lambda i, j: (i, 0)),
            pl.BlockSpec((1, 1, d), seg_of),
            pl.BlockSpec((1, 1, d), seg_of),
            pl.BlockSpec((d, tn), lambda i, j: (0, j)),
            pl.BlockSpec((d, ns), lambda i, j: (0, 0)),
        ] + prev_specs,
        out_specs=[
            pl.BlockSpec((tm, tn), lambda i, j: (rb0 + i, j)),
            pl.BlockSpec((tm, ns), lambda i, j: (rb0 + i, 0)),
        ],
        out_shape=[
            jax.ShapeDtypeStruct((t_total, n), BF16),
            jax.ShapeDtypeStruct((t_total, ns), F32),
        ],
        input_output_aliases=aliases,
        scratch_shapes=[pltpu.VMEM((tm, d), BF16)],
        compiler_params=_cparams(2),
        name="inproj",
    )(x, shift, scale, w_main, w_small, *prev)


def _rope_lanes(x, cs, sn):
    return x * cs + pltpu.roll(x, LANES // 2, axis=1) * sn


def _qkv_kernel(ql_ref, kvl_ref, ps_ref, cs_ref, sn_ref, qnw_ref, kvnw_ref, wq_ref, wk_ref, wv_ref,
                qt_out, k_out, vt_out, *, heads, qk_scale):
    def rms(v, w):
        return (v * lax.rsqrt(jnp.mean(v * v, axis=-1, keepdims=True) + RMS_EPS) * w).astype(BF16)

    qn = rms(ql_ref[...].astype(F32), qnw_ref[...])
    kvn = rms(kvl_ref[...].astype(F32), kvnw_ref[...])
    cs = cs_ref[...]
    sn = sn_ref[...]
    kr = _rope_lanes(ps_ref[...], cs, sn).astype(BF16)
    tm = kr.shape[0]
    sub = lax.broadcasted_iota(jnp.int32, (VT_ROWS - MLA_V_DIM, tm), 0)
    ones_row = jnp.where(sub == 0, 1.0, 0.0).astype(BF16)
    for h in range(heads):
        qa = jnp.dot(qn, wq_ref[:, h * HEAD_LANES:(h + 1) * HEAD_LANES], preferred_element_type=F32)
        qt_out[h, 0, :LANES, :] = (qa[:, :LANES] * qk_scale).T.astype(BF16)
        qt_out[h, 0, LANES:, :] = (_rope_lanes(qa[:, LANES:], cs, sn) * qk_scale).T.astype(BF16)
        k_out[h, :, LANES:] = kr
        vt_out[h, 0, MLA_V_DIM:, :] = ones_row
    for h in range(0, heads, 2):
        ka = jnp.dot(kvn, wk_ref[:, h * LANES:(h + 2) * LANES], preferred_element_type=F32)
        va = jnp.dot(kvn, wv_ref[:, h * LANES:(h + 2) * LANES], preferred_element_type=F32)
        for e in range(2):
            k_out[h + e, :, :LANES] = ka[:, e * LANES:(e + 1) * LANES].astype(BF16)
            vt_out[h + e, 0, :MLA_V_DIM, :] = va[:, e * LANES:(e + 1) * LANES].T.astype(BF16)


def _qkv(pm, ps, cs_tab, sn_tab, qnw, kvnw, wq, wk, wv, *, t_prompt, dec_seq, q_rank, kv_rank, tkv):
    t = pm.shape[0]
    heads = MLA_HEADS
    tm = min(QKV_TM, tkv)
    assert q_rank == kv_rank and q_rank % LANES == 0 and tkv % tm == 0 and dec_seq % tkv == 0
    per_kv = tkv // tm
    np_blocks = t_prompt // tm
    dec_blocks = dec_seq // tm

    def pos_block(i):
        return (jnp.where(i < np_blocks, i, (i - np_blocks) % dec_blocks), 0)

    out_sds = jax.ShapeDtypeStruct((heads, t, HEAD_LANES), BF16)
    out_spec = pl.BlockSpec((heads, tm, HEAD_LANES), lambda i: (0, i, 0))
    kern = functools.partial(_qkv_kernel, heads=heads, qk_scale=MLA_QK_DIM ** -0.5 * LOG2_E)
    return pl.pallas_call(
        kern,
        grid=(t // tm,),
        in_specs=[
            pl.BlockSpec((tm, q_rank), lambda i: (i, 0)),
            pl.BlockSpec((tm, kv_rank), lambda i: (i, 1)),
            pl.BlockSpec((tm, LANES), lambda i: (i, 0)),
            pl.BlockSpec((tm, LANES), pos_block),
            pl.BlockSpec((tm, LANES), pos_block),
            pl.BlockSpec((1, q_rank), lambda i: (0, 0)),
            pl.BlockSpec((1, kv_rank), lambda i: (0, 0)),
            pl.BlockSpec(wq.shape, lambda i: (0, 0)),
            pl.BlockSpec(wk.shape, lambda i: (0, 0)),
            pl.BlockSpec(wv.shape, lambda i: (0, 0)),
        ],
        out_specs=[pl.BlockSpec((heads, 1, HEAD_LANES, tm), lambda i: (0, i, 0, 0)), out_spec,
                   pl.BlockSpec((heads, 1, VT_ROWS, tm), lambda i: (0, i // per_kv, 0, i % per_kv))],
        out_shape=[jax.ShapeDtypeStruct((heads, t // tm, HEAD_LANES, tm), BF16), out_sds,
                   jax.ShapeDtypeStruct((heads, t // tkv, VT_ROWS, tkv), BF16)],
        compiler_params=_cparams(1),
        name="qkv_up",
    )(pm, pm, ps, cs_tab, sn_tab, qnw, kvnw, wq, wk, wv)


def _attn_kernel(q_ref, k_ref, vt_ref, zb_ref, o_ref, *scratch, n_tiles, **kw):
    per_set = len(scratch) // min(n_tiles, 2)
    for tile in range(n_tiles):
        first = (tile % 2) * per_set
        _attn_tile(q_ref, k_ref, vt_ref, zb_ref, o_ref, *scratch[first:first + per_set],
                   tile=tile, n_tiles=n_tiles, **kw)


def _attn_tile(q_ref, k_ref, vt_ref, zb_ref, o_ref, s0_ref, s1_ref, x0_ref, x1_ref, p0_ref, p1_ref,
               a0_ref, a1_ref, m_ref, acc_ref, *, tile, n_tiles, tk, nk, cb_pv):
    slab = q_ref.shape[3]
    n_slab = q_ref.shape[1] // n_tiles
    tq = n_slab * slab
    per_slab = slab // LANES

    def scores(j, s_ref, x_ref):
        start = pl.multiple_of(j * tk, tk)
        kt = k_ref[0, pl.ds(start, tk), :]
        for b in range(n_slab):
            st = jnp.dot(kt, q_ref[0, tile * n_slab + b], preferred_element_type=F32)
            for c in range(per_slab):
                s_ref[b * per_slab + c] = st[:, c * LANES:(c + 1) * LANES]
            x_ref[:, b * slab:(b + 1) * slab] = jnp.broadcast_to(
                jnp.max(st, axis=0, keepdims=True), (x_ref.shape[0], slab))

    def softmax(s_ref, x_ref, p_ref, a_ref):
        for c in range(tq // LANES):
            cols = slice(c * LANES, (c + 1) * LANES)
            m_prev = m_ref[:, cols]
            m_new = jnp.maximum(m_prev, x_ref[:, cols])
            p_ref[c] = jnp.exp2(s_ref[c] - m_new[0:1, :]).astype(BF16)
            a_ref[:, cols] = jnp.exp2(m_prev - m_new)
            m_ref[:, cols] = m_new

    def pv(j, p_ref, a_ref):
        vt = vt_ref[0, j]
        per = cb_pv // LANES
        for c in range(tq // cb_pv):
            cols = slice(c * cb_pv, (c + 1) * cb_pv)
            pt = jnp.concatenate([p_ref[c * per + i] for i in range(per)], axis=1)
            upd = jnp.dot(vt, pt, preferred_element_type=F32)
            acc_ref[c] = acc_ref[c] * a_ref[0:1, cols] + upd

    m_ref[...] = jnp.full(m_ref.shape, -jnp.inf, F32)
    acc_ref[...] = jnp.zeros(acc_ref.shape, F32)
    scores(0, s0_ref, x0_ref)
    scores(1, s1_ref, x1_ref)
    softmax(s0_ref, x0_ref, p0_ref, a0_ref)

    def body(jj, carry):
        j = 2 * jj
        scores(j + 2, s0_ref, x0_ref)
        softmax(s1_ref, x1_ref, p1_ref, a1_ref)
        pv(j, p0_ref, a0_ref)
        scores(j + 3, s1_ref, x1_ref)
        softmax(s0_ref, x0_ref, p0_ref, a0_ref)
        pv(j + 1, p1_ref, a1_ref)
        return carry

    lax.fori_loop(0, nk // 2 - 1, body, 0)
    softmax(s1_ref, x1_ref, p1_ref, a1_ref)
    pv(nk - 2, p0_ref, a0_ref)
    pv(nk - 1, p1_ref, a1_ref)
    for c in range(tq // cb_pv):
        rows = slice(tile * tq + c * cb_pv, tile * tq + (c + 1) * cb_pv)
        o = (acc_ref[c, :MLA_V_DIM, :] / acc_ref[c, MLA_V_DIM:MLA_V_DIM + 1, :]).T
        o_ref[rows, :] = (o * _silu(zb_ref[rows, :].astype(F32))).astype(BF16)


def _with_carried_outputs(kern, n_in, prev):
    if prev is None:
        return kern, [], {}, ()

    def body(*refs):
        return kern(*refs[:n_in], *refs[n_in + len(prev):])

    specs = [pl.BlockSpec(memory_space=pl.ANY)] * len(prev)
    return body, specs, {n_in + k: k for k in range(len(prev))}, tuple(prev)


def _attention(qtf, kf, vtf, pm, prev, *, t_total, row0, n_seq, seq, zb_col0):
    heads, _, _, slab = qtf.shape
    tk = vtf.shape[3]
    tq = min(ATTN_TQ, seq)
    cb_pv = min(ATTN_CB_PV, tq)
    assert tq % slab == 0 and slab % LANES == 0
    nk = seq // tk
    assert nk % 2 == 0
    scratch = [
        pltpu.VMEM((tq // LANES, tk, LANES), F32),
        pltpu.VMEM((tq // LANES, tk, LANES), F32),
        pltpu.VMEM((8, tq), F32),
        pltpu.VMEM((8, tq), F32),
        pltpu.VMEM((tq // LANES, tk, LANES), BF16),
        pltpu.VMEM((tq // LANES, tk, LANES), BF16),
        pltpu.VMEM((8, tq), F32),
        pltpu.VMEM((8, tq), F32),
        pltpu.VMEM((8, tq), F32),
        pltpu.VMEM((tq // cb_pv, VT_ROWS, cb_pv), F32),
    ]
    scratch_bytes = 2 * tq * tk * (4 + 2) + 5 * 8 * tq * 4 + VT_ROWS * tq * 4
    kv_bytes = 2 * 2 * (seq * HEAD_LANES + nk * VT_ROWS * tk)
    io_bytes = 2 * 2 * tq * (HEAD_LANES + 2 * LANES)
    n_tiles = 2 if (seq // tq) % 2 == 0 and 2 * (scratch_bytes + io_bytes) + kv_bytes <= ATTN_VMEM_BUDGET else 1
    tqs = tq * n_tiles
    nq = seq // tqs
    qb0 = row0 // tqs
    sb0 = row0 // seq
    zc0 = zb_col0 // LANES
    assert row0 % tqs == 0
    kern = functools.partial(_attn_kernel, n_tiles=n_tiles, tk=tk, nk=nk, cb_pv=cb_pv)
    kern, prev_specs, aliases, prev = _with_carried_outputs(kern, 4, prev)
    return pl.pallas_call(
        kern,
        grid=(n_seq, heads, nq),
        in_specs=[
            pl.BlockSpec((1, tqs // slab, HEAD_LANES, slab), lambda n, h, i: (h, qb0 + n * nq + i, 0, 0)),
            pl.BlockSpec((1, seq, HEAD_LANES), lambda n, h, i: (h, sb0 + n, 0)),
            pl.BlockSpec((1, nk, VT_ROWS, tk), lambda n, h, i: (h, sb0 + n, 0, 0)),
            pl.BlockSpec((tqs, LANES), lambda n, h, i: (qb0 + n * nq + i, zc0 + h)),
        ] + prev_specs,
        out_specs=pl.BlockSpec((tqs, LANES), lambda n, h, i: (qb0 + n * nq + i, h)),
        out_shape=jax.ShapeDtypeStruct((t_total, heads * LANES), BF16),
        input_output_aliases=aliases,
        scratch_shapes=scratch * n_tiles,
        compiler_params=_cparams(3),
        name="mla_attention",
    )(qtf, kf, vtf, pm, *prev)


def _mlstm_kernel(qf_ref, kf_ref, vf_ref, gf_ref, qb_ref, kb_ref, vb_ref, gb_ref, bif_ref,
                  hf_ref, hb_ref, st_ref, m_ref, dm_ref, wi_ref, fl_ref, wr_ref, dc_ref, *, heads, log2_k_scale):
    L = ML_CHUNK
    dh = ML_HEAD_DIM

    @pl.when(pl.program_id(1) == 0)
    def _():
        st_ref[...] = jnp.zeros_like(st_ref)
        m_ref[...] = jnp.zeros_like(m_ref)

    assert L == LANES
    row = lax.broadcasted_iota(jnp.int32, (L, L), 0)
    col = lax.broadcasted_iota(jnp.int32, (L, L), 1)
    ones_blk = jnp.ones((L, LANES), BF16)

    def rep(column):
        return jnp.broadcast_to(column, (L, LANES))

    dirs = ((qf_ref, kf_ref, vf_ref, gf_ref, hf_ref, col <= row, L - 1),
            (qb_ref, kb_ref, vb_ref, gb_ref, hb_ref, col >= row, 0))
    for d, (q_ref, k_ref, v_ref, g_ref, h_ref, mask, last) in enumerate(dirs):
        g = g_ref[...] + bif_ref[...]
        lf = _log_sigmoid(g) * LOG2_E
        g = g * LOG2_E
        b_all = jnp.dot(mask.astype(F32), lf, preferred_element_type=F32,
                        precision=lax.Precision.HIGHEST)
        g_t = g.T
        b_t = b_all.T
        for h in range(heads):
            ci = 2 * heads * d + h
            cf = ci + heads
            idx = d * heads + h
            b_r = rep(b_all[:, cf:cf + 1])
            i_r = rep(g[:, ci:ci + 1])
            arow = g_t[ci:ci + 1, :] - b_t[cf:cf + 1, :]
            m_prev = m_ref[idx]
            logd = jnp.where(mask, b_r + arow, -jnp.inf)
            inter = b_r + m_prev
            m_t = jnp.maximum(inter, rep(jnp.max(logd, axis=1, keepdims=True)))
            dm_ref[idx] = jnp.exp2(logd - m_t)
            wi_ref[idx] = jnp.exp2(inter - m_t)
            fl_ref[idx] = jnp.exp2(-log2_k_scale - m_t)
            m_new = m_t[last:last + 1, :]
            b_last = b_r[last:last + 1, :]
            wr_ref[idx] = jnp.exp2(b_last - b_r + i_r - m_new)
            dc_ref[idx] = jnp.exp2(b_last + m_prev - m_new)
            m_ref[idx] = m_new

    for d, (q_ref, k_ref, v_ref, g_ref, h_ref, mask, last) in enumerate(dirs):
        for h in range(heads):
            idx = d * heads + h
            q = q_ref[:, h * dh:(h + 1) * dh]
            k = k_ref[:, h * dh:(h + 1) * dh]
            v = v_ref[:, h * dh:(h + 1) * dh]
            s = lax.dot_general(q, k, (((1,), (1,)), ((), ())), preferred_element_type=F32)
            s = (s * dm_ref[idx]).astype(BF16)
            st_prev = st_ref[idx]
            intra = jnp.dot(s, jnp.concatenate([v, ones_blk], axis=1), preferred_element_type=F32)
            inter_p = jnp.dot(q, st_prev.astype(BF16), preferred_element_type=F32)
            w_inter = wi_ref[idx]
            num = intra[:, :dh] + w_inter * inter_p[:, :dh]
            den = intra[:, dh:] + w_inter * inter_p[:, dh:]
            hout = num / jnp.maximum(jnp.abs(den), fl_ref[idx])
            h_ref[:, h * dh:(h + 1) * dh] = hout.astype(h_ref.dtype)
            w_r = wr_ref[idx]
            decay = dc_ref[idx]
            wv = jnp.concatenate([v.astype(F32) * w_r, w_r], axis=1).astype(BF16)
            upd = lax.dot_general(k, wv, (((0,), (0,)), ((), ())), preferred_element_type=F32)
            st_ref[idx] = jnp.concatenate([decay, decay], axis=1) * st_prev + upd


def _mlstm(pm, ps, bif, prev, *, t_total, row0, n_seq, seq, q_col0):
    heads = ML_HEADS
    width = heads * ML_HEAD_DIM
    L = ML_CHUNK
    nc = seq // L
    rb0 = row0 // L
    qc = q_col0 // width
    fwd = lambda n, c: rb0 + n * nc + c
    bwd = lambda n, c: rb0 + n * nc + (nc - 1 - c)

    def col_spec(chunk_of, cb):
        return pl.BlockSpec((L, width), lambda n, c: (chunk_of(n, c), cb))

    gate_f = pl.BlockSpec((L, LANES), lambda n, c: (fwd(n, c), 1))
    gate_b = pl.BlockSpec((L, LANES), lambda n, c: (bwd(n, c), 1))
    out_sds = jax.ShapeDtypeStruct((t_total, width), BF16)
    kern = functools.partial(_mlstm_kernel, heads=heads, log2_k_scale=-0.5 * math.log2(ML_HEAD_DIM))
    kern, prev_specs, aliases, prev = _with_carried_outputs(kern, 9, prev)
    return pl.pallas_call(
        kern,
        grid=(n_seq, nc),
        in_specs=[
            col_spec(fwd, qc), col_spec(fwd, qc + 1), col_spec(fwd, qc + 2), gate_f,
            col_spec(bwd, qc), col_spec(bwd, qc + 1), col_spec(bwd, qc + 2), gate_b,
            pl.BlockSpec((1, LANES), lambda n, c: (0, 0)),
        ] + prev_specs,
        out_specs=[
            pl.BlockSpec((L, width), lambda n, c: (fwd(n, c), 0)),
            pl.BlockSpec((L, width), lambda n, c: (bwd(n, c), 0)),
        ],
        out_shape=[out_sds, out_sds],
        input_output_aliases=aliases,
        scratch_shapes=[
            pltpu.VMEM((2 * heads, ML_HEAD_DIM, HEAD_LANES), F32),
            pltpu.VMEM((2 * heads, 1, LANES), F32),
            pltpu.VMEM((2 * heads, L, L), F32),
            pltpu.VMEM((2 * heads, L, LANES), F32),
            pltpu.VMEM((2 * heads, L, LANES), F32),
            pltpu.VMEM((2 * heads, L, LANES), F32),
            pltpu.VMEM((2 * heads, 1, LANES), F32),
        ],
        compiler_params=_cparams(2),
        name="mlstm_bidir",
    )(pm, pm, pm, ps, pm, pm, pm, ps, bif, *prev)


def _tail_kernel(x_ref, attn_ref, hf_ref, hb_ref, o_ref, za_ref, ga_ref, gb_ref, gate_ref,
                 mlw_ref, wpa_ref, wpb_ref, wo_ref, lnw_ref, lnb_ref, y_ref, *, heads, alpha):
    dh = ML_HEAD_DIM
    tm = x_ref.shape[0]
    sub = min(TAIL_SUB, tm)
    for r0 in range(0, tm, sub):
        rows = slice(r0, r0 + sub)
        hs = hf_ref[rows, :].astype(F32) + hb_ref[rows, :].astype(F32)
        hn = jnp.concatenate(
            [_layer_norm_rows(hs[:, h * dh:(h + 1) * dh]) for h in range(heads)], axis=1)
        a_in = (hn * mlw_ref[...] * _sigmoid(o_ref[rows, :].astype(F32))
                * _silu(za_ref[rows, :].astype(F32)))
        y_a = jnp.dot(a_in.astype(BF16), wpa_ref[...], preferred_element_type=F32)
        y_b = jnp.dot(attn_ref[rows, :], wpb_ref[...], preferred_element_type=F32)
        merged = (_sigmoid(ga_ref[rows, :].astype(F32)) * y_a
                  + _sigmoid(gb_ref[rows, :].astype(F32)) * y_b)
        out = jnp.dot(merged.astype(BF16), wo_ref[...], preferred_element_type=F32)
        r = alpha * x_ref[rows, :] + gate_ref[0] * out
        y_ref[rows, :] = _layer_norm_rows(r) * lnw_ref[...] + lnb_ref[...]


def _tail(x, attn, hf, hb, pm, gate, mlw, wpa, wpb, wo, lnw, lnb, prev, *, n_rows, x_row0, row0, out_rows,
          out_row0, seg, alpha, o_col0, za_col0, g_col0):
    d = x.shape[1]
    wa = hf.shape[1]
    tm = min(TAIL_TM, seg)
    xb0, gb0, ob0 = x_row0 // tm, row0 // tm, out_row0 // tm
    const = lambda shape: pl.BlockSpec(shape, lambda i: (0,) * len(shape), pipeline_mode=pl.Buffered(1))
    kern = functools.partial(_tail_kernel, heads=ML_HEADS, alpha=alpha)
    kern, prev_specs, aliases, prev = _with_carried_outputs(kern, 15, prev)
    return pl.pallas_call(
        kern,
        grid=(n_rows // tm,),
        in_specs=[
            pl.BlockSpec((tm, d), lambda i: (xb0 + i, 0)),
            pl.BlockSpec((tm, attn.shape[1]), lambda i: (gb0 + i, 0)),
            pl.BlockSpec((tm, wa), lambda i: (gb0 + i, 0)),
            pl.BlockSpec((tm, wa), lambda i: (gb0 + i, 0)),
            pl.BlockSpec((tm, wa), lambda i: (gb0 + i, o_col0 // wa)),
            pl.BlockSpec((tm, wa), lambda i: (gb0 + i, za_col0 // wa)),
            pl.BlockSpec((tm, d), lambda i: (gb0 + i, g_col0 // d)),
            pl.BlockSpec((tm, d), lambda i: (gb0 + i, g_col0 // d + 1)),
            pl.BlockSpec((1, 1, d), lambda i: ((row0 + i * tm) // seg, 0, 0)),
            const((1, wa)),
            const(wpa.shape),
            const(wpb.shape),
            const(wo.shape),
            const((1, d)),
            const((1, d)),
        ] + prev_specs,
        out_specs=pl.BlockSpec((tm, d), lambda i: (ob0 + i, 0)),
        out_shape=jax.ShapeDtypeStruct((out_rows, d), F32),
        input_output_aliases=aliases,
        compiler_params=_cparams(1),
        name="tail",
    )(x, attn, hf, hb, pm, pm, pm, pm, gate, mlw, wpa, wpb, wo, lnw, lnb, *prev)


def _rot_half(w):
    half = w.shape[-1] // 2
    return jnp.concatenate([-w[..., half:], w[..., :half]], axis=-1)


def _prep_weights(w_in, b_if, w_q_b, w_kv_b, w_proj_a, w_proj_b, w_out, d_model, q_rank, kv_rank):
    depth = w_in.shape[0]
    wa = ML_HEADS * ML_HEAD_DIM
    wb = MLA_HEADS * MLA_V_DIM
    sizes = [q_rank, kv_rank, MLA_ROPE_DIM, wb, wa, wa, wa, wa, wa, 4 * ML_HEADS, 2 * d_model]
    offs = [0]
    for s in sizes:
        offs.append(offs[-1] + s)
    part = lambda i: w_in[:, :, offs[i]:offs[i + 1]]
    (q_lat, kv_lat, k_rope, z_b, ml_q, ml_k, ml_v, ml_o, z_a, ml_g, merge_g) = [part(i) for i in range(11)]
    main_parts = [q_lat, kv_lat, z_b, ml_q, ml_k, ml_v, ml_o, z_a, merge_g]
    w_main = jnp.concatenate(main_parts, axis=-1).astype(BF16)
    names = ["q_lat", "kv_lat", "z_b", "ml_q", "ml_k", "ml_v", "ml_o", "z_a", "merge_g"]
    cols, o = {}, 0
    for nme, p in zip(names, main_parts):
        cols[nme] = o
        o += p.shape[-1]
    pad = jnp.zeros(w_in.shape[:2] + (LANES - 4 * ML_HEADS,), w_in.dtype)
    w_small = jnp.concatenate([k_rope, _rot_half(k_rope), ml_g, pad], axis=-1).astype(BF16)

    wq = w_q_b.reshape(depth, q_rank, MLA_HEADS, MLA_QK_DIM)
    wq_r = wq[..., MLA_NOPE_DIM:]
    wq = jnp.concatenate([wq[..., :MLA_NOPE_DIM], wq_r, _rot_half(wq_r)], axis=-1)
    wq = wq.reshape(depth, q_rank, MLA_HEADS * HEAD_LANES).astype(BF16)
    wkv = w_kv_b.reshape(depth, kv_rank, MLA_HEADS, MLA_NOPE_DIM + MLA_V_DIM)
    wk = wkv[..., :MLA_NOPE_DIM].reshape(depth, kv_rank, MLA_HEADS * MLA_NOPE_DIM).astype(BF16)
    wv = wkv[..., MLA_NOPE_DIM:].reshape(depth, kv_rank, MLA_HEADS * MLA_V_DIM).astype(BF16)
    bif = jnp.pad(b_if.reshape(depth, 1, 4 * ML_HEADS), ((0, 0), (0, 0), (0, LANES - 4 * ML_HEADS)))
    return dict(w_main=w_main, w_small=w_small, cols=cols, wq=wq, wk=wk, wv=wv, bif=bif,
                wpa=w_proj_a.astype(BF16), wpb=w_proj_b.astype(BF16), wo=w_out.astype(BF16))


def _rope_lane_tables(seq_len):
    inv_freq = ROPE_THETA ** (-jnp.arange(0, MLA_ROPE_DIM, 2, dtype=F32) / MLA_ROPE_DIM)
    ang = jnp.arange(seq_len, dtype=F32)[:, None] * inv_freq[None, :]
    zeros = jnp.zeros((seq_len, LANES - MLA_ROPE_DIM), F32)
    cos, sin = jnp.cos(ang), jnp.sin(ang)
    return (jnp.concatenate([cos, cos, zeros], axis=1), jnp.concatenate([sin, sin, zeros], axis=1))


def kernel(x_prompt, x_sample, c_prompt, c_sample, w_ada, b_ada, w_in, b_if, q_norm_w, kv_norm_w,
           w_q_b, w_kv_b, ml_norm_w, w_proj_a, w_proj_b, w_out, ln_w, ln_b):
    batch, seq, d = x_prompt.shape
    dec_batch, dec_seq, _ = x_sample.shape
    depth = w_ada.shape[0]
    q_rank = q_norm_w.shape[1]
    kv_rank = kv_norm_w.shape[1]
    alpha = (2 * depth) ** 0.25
    assert seq % dec_seq == 0 and dec_seq % ML_CHUNK == 0
    seg = dec_seq
    t_prompt = batch * seq
    t = t_prompt + dec_batch * dec_seq

    x_groups = (x_prompt.reshape(t_prompt, d), x_sample.reshape(dec_batch * dec_seq, d))
    c_all = jnp.concatenate([c_prompt, c_sample], axis=0)
    n_cond = c_all.shape[0]
    rows = -(-n_cond // 16) * 16
    c_pad = jnp.pad(c_all, ((0, rows - n_cond), (0, 0)))
    seg_cond = jnp.concatenate([jnp.repeat(jnp.arange(batch), seq // seg),
                                batch + jnp.arange(dec_batch)])

    mod = _adaln_mod(c_pad, w_ada, b_ada)
    mod = mod[:, seg_cond, :].reshape(depth, t // seg, 1, 3, d)
    shift, scale, gate = mod[..., 0, :], mod[..., 1, :], mod[..., 2, :]

    w = _prep_weights(w_in, b_if, w_q_b, w_kv_b, w_proj_a, w_proj_b, w_out, d, q_rank, kv_rank)
    cols = w["cols"]
    cs_tab, sn_tab = _rope_lane_tables(max(seq, dec_seq))
    groups = ((0, batch, seq), (t_prompt, dec_batch, dec_seq))

    x = None
    for l in range(depth):
        if l == 0:
            pmps = None
            for (row0, _, _), xg in zip(groups, x_groups):
                pmps = _inproj(xg, shift[l], scale[l], w["w_main"][l], w["w_small"][l], pmps,
                               seg=seg, t_total=t, row0=row0)
            pm, ps = pmps
        else:
            pm, ps = _inproj(x, shift[l], scale[l], w["w_main"][l], w["w_small"][l], None,
                             seg=seg, t_total=t, row0=0)
        qf, kf, vf = _qkv(pm, ps, cs_tab, sn_tab, q_norm_w[l][None], kv_norm_w[l][None],
                          w["wq"][l], w["wk"][l], w["wv"][l],
                          t_prompt=t_prompt, dec_seq=dec_seq, q_rank=q_rank, kv_rank=kv_rank,
                          tkv=min(ATTN_TK, dec_seq))
        attn, hfb = None, None
        for row0, n_seq, s_len in groups:
            attn = [_attention(qf, kf, vf, pm, attn, t_total=t, row0=row0, n_seq=n_seq, seq=s_len,
                               zb_col0=cols["z_b"])]
            hfb = _mlstm(pm, ps, w["bif"][l], hfb, t_total=t, row0=row0, n_seq=n_seq, seq=s_len,
                         q_col0=cols["ml_q"])
        tail = functools.partial(
            _tail, attn=attn[0], hf=hfb[0], hb=hfb[1], pm=pm, gate=gate[l], mlw=ml_norm_w[l][None],
            wpa=w["wpa"][l], wpb=w["wpb"][l], wo=w["wo"][l], lnw=ln_w[l][None], lnb=ln_b[l][None],
            seg=seg, alpha=alpha, o_col0=cols["ml_o"], za_col0=cols["z_a"], g_col0=cols["merge_g"])
        first, last = l == 0, l == depth - 1
        if not (first or last):
            x = tail(x=x, prev=None, n_rows=t, x_row0=0, row0=0, out_rows=t, out_row0=0)
            continue
        outs, carried = [], None
        for (row0, n_seq, s_len), xg in zip(groups, x_groups):
            n_rows = n_seq * s_len
            y = tail(x=xg if first else x, prev=None if last else carried, n_rows=n_rows,
                     x_row0=0 if first else row0, row0=row0,
                     out_rows=n_rows if last else t, out_row0=0 if last else row0)
            outs.append(y)
            carried = [y]
        x = None if last else outs[-1]

    y_prompt, y_sample = outs
    return (y_prompt.reshape(batch, seq, d), y_sample.reshape(dec_batch, dec_seq, d))
```

```python
import functools
import math

import jax
import jax.numpy as jnp
from jax import lax
from jax.experimental import pallas as pl
from jax.experimental.pallas import tpu as pltpu

F32 = jnp.float32
BF16 = jnp.bfloat16

MLA_HEADS = 16
MLA_NOPE_DIM = 128
MLA_ROPE_DIM = 64
MLA_V_DIM = 128
MLA_QK_DIM = MLA_NOPE_DIM + MLA_ROPE_DIM
ROPE_THETA = 10000.0
ML_HEADS = 8
ML_HEAD_DIM = 128
ML_CHUNK = 128
LN_EPS = 1e-5
RMS_EPS = 1e-6
LOG2_E = 1.4426950408889634

LANES = 128
HEAD_LANES = 2 * LANES
VT_ROWS = MLA_V_DIM + 16
VMEM_LIMIT_BYTES = 56 * 1024 * 1024
ATTN_VMEM_BUDGET = 44 * 1024 * 1024

MOD_TN = 1024
INPROJ_TM = 1024
INPROJ_TN = 1024
QKV_TM = 256
ATTN_TQ = 2048
ATTN_TK = 512
ATTN_CB_PV = 256
TAIL_TM = 256
TAIL_SUB = 256


def _cparams(n_axes):
    return pltpu.CompilerParams(
        dimension_semantics=("arbitrary",) * n_axes,
        vmem_limit_bytes=VMEM_LIMIT_BYTES,
    )


def _sigmoid(x):
    return 1.0 / (1.0 + jnp.exp(-x))


def _silu(x):
    return x * _sigmoid(x)


def _log_sigmoid(x):
    return jnp.minimum(x, 0.0) - jnp.log(1.0 + jnp.exp(-jnp.abs(x)))


def _layer_norm_rows(x):
    mu = jnp.mean(x, axis=-1, keepdims=True)
    xc = x - mu
    var = jnp.mean(xc * xc, axis=-1, keepdims=True)
    return xc * lax.rsqrt(var + LN_EPS)


def _mod_kernel(c_ref, w_ref, b_ref, o_ref):
    c = c_ref[...]
    a = _silu(c).astype(BF16)
    o_ref[0] = jnp.dot(a, w_ref[0].astype(BF16), preferred_element_type=F32) + b_ref[0]


def _adaln_mod(c_pad, w_ada, b_ada):
    depth, d, n = w_ada.shape
    rows = c_pad.shape[0]
    tn = min(MOD_TN, n)
    return pl.pallas_call(
        _mod_kernel,
        grid=(depth, n // tn),
        in_specs=[
            pl.BlockSpec((rows, d), lambda l, j: (0, 0)),
            pl.BlockSpec((1, d, tn), lambda l, j: (l, 0, j)),
            pl.BlockSpec((1, 1, tn), lambda l, j: (l, 0, j)),
        ],
        out_specs=pl.BlockSpec((1, rows, tn), lambda l, j: (l, 0, j)),
        out_shape=jax.ShapeDtypeStruct((depth, rows, n), F32),
        compiler_params=_cparams(2),
        name="adaln_mod",
    )(c_pad, w_ada, b_ada.reshape(depth, 1, n))


def _inproj_kernel(x_ref, sh_ref, sc_ref, wm_ref, ws_ref, pm_ref, ps_ref, u_ref):
    @pl.when(pl.program_id(1) == 0)
    def _():
        u = _layer_norm_rows(x_ref[...]) * (1.0 + sc_ref[0]) + sh_ref[0]
        ub = u.astype(BF16)
        u_ref[...] = ub
        ps_ref[...] = jnp.dot(ub, ws_ref[...], preferred_element_type=F32)

    pm_ref[...] = jnp.dot(u_ref[...], wm_ref[...], preferred_element_type=F32).astype(BF16)


def _inproj(x, shift, scale, w_main, w_small, prev, *, seg, t_total, row0):
    t, d = x.shape
    n = w_main.shape[1]
    ns = w_small.shape[1]
    tm = min(INPROJ_TM, seg)
    tn = min(INPROJ_TN, n)
    rb0 = row0 // tm
    seg_of = lambda i, j: ((row0 + i * tm) // seg, 0, 0)
    kern, prev_specs, aliases, prev = _with_carried_outputs(_inproj_kernel, 5, prev)
    return pl.pallas_call(
        kern,
        grid=(t // tm, n // tn),
        in_specs=[
            pl.BlockSpec((tm, d), lambda i, j: (i, 0)),
            pl.BlockSpec((1, 1, d), seg_of),
            pl.BlockSpec((1, 1, d), seg_of),
            pl.BlockSpec((d, tn), lambda i, j: (0, j)),
            pl.BlockSpec((d, ns), lambda i, j: (0, 0)),
        ] + prev_specs,
        out_specs=[
            pl.BlockSpec((tm, tn), lambda i, j: (rb0 + i, j)),
            pl.BlockSpec((tm, ns), lambda i, j: (rb0 + i, 0)),
        ],
        out_shape=[
            jax.ShapeDtypeStruct((t_total, n), BF16),
            jax.ShapeDtypeStruct((t_total, ns), F32),
        ],
        input_output_aliases=aliases,
        scratch_shapes=[pltpu.VMEM((tm, d), BF16)],
        compiler_params=_cparams(2),
        name="inproj",
    )(x, shift, scale, w_main, w_small, *prev)


def _rope_lanes(x, cs, sn):
    return x * cs + pltpu.roll(x, LANES // 2, axis=1) * sn


def _qkv_kernel(ql_ref, kvl_ref, ps_ref, cs_ref, sn_ref, qnw_ref, kvnw_ref, wq_ref, wk_ref, wv_ref,
                qt_out, k_out, vt_out, *, heads, qk_scale):
    def rms(v, w):
        return (v * lax.rsqrt(jnp.mean(v * v, axis=-1, keepdims=True) + RMS_EPS) * w).astype(BF16)

    qn = rms(ql_ref[...].astype(F32), qnw_ref[...])
    kvn = rms(kvl_ref[...].astype(F32), kvnw_ref[...])
    cs = cs_ref[...]
    sn = sn_ref[...]
    kr = _rope_lanes(ps_ref[...], cs, sn).astype(BF16)
    tm = kr.shape[0]
    sub = lax.broadcasted_iota(jnp.int32, (VT_ROWS - MLA_V_DIM, tm), 0)
    ones_row = jnp.where(sub == 0, 1.0, 0.0).astype(BF16)
    for h in range(heads):
        qa = jnp.dot(qn, wq_ref[:, h * HEAD_LANES:(h + 1) * HEAD_LANES], preferred_element_type=F32)
        qt_out[h, 0, :LANES, :] = (qa[:, :LANES] * qk_scale).T.astype(BF16)
        qt_out[h, 0, LANES:, :] = (_rope_lanes(qa[:, LANES:], cs, sn) * qk_scale).T.astype(BF16)
        k_out[h, :, LANES:] = kr
        vt_out[h, 0, MLA_V_DIM:, :] = ones_row
    for h in range(0, heads, 2):
        ka = jnp.dot(kvn, wk_ref[:, h * LANES:(h + 2) * LANES], preferred_element_type=F32)
        va = jnp.dot(kvn, wv_ref[:, h * LANES:(h + 2) * LANES], preferred_element_type=F32)
        for e in range(2):
            k_out[h + e, :, :LANES] = ka[:, e * LANES:(e + 1) * LANES].astype(BF16)
            vt_out[h + e, 0, :MLA_V_DIM, :] = va[:, e * LANES:(e + 1) * LANES].T.astype(BF16)


def _qkv(pm, ps, cs_tab, sn_tab, qnw, kvnw, wq, wk, wv, *, t_prompt, dec_seq, q_rank, kv_rank, tkv):
    t = pm.shape[0]
    heads = MLA_HEADS
    tm = min(QKV_TM, tkv)
    assert q_rank == kv_rank and q_rank % LANES == 0 and tkv % tm == 0 and dec_seq % tkv == 0
    per_kv = tkv // tm
    np_blocks = t_prompt // tm
    dec_blocks = dec_seq // tm

    def pos_block(i):
        return (jnp.where(i < np_blocks, i, (i - np_blocks) % dec_blocks), 0)

    out_sds = jax.ShapeDtypeStruct((heads, t, HEAD_LANES), BF16)
    out_spec = pl.BlockSpec((heads, tm, HEAD_LANES), lambda i: (0, i, 0))
    kern = functools.partial(_qkv_kernel, heads=heads, qk_scale=MLA_QK_DIM ** -0.5 * LOG2_E)
    return pl.pallas_call(
        kern,
        grid=(t // tm,),
        in_specs=[
            pl.BlockSpec((tm, q_rank), lambda i: (i, 0)),
            pl.BlockSpec((tm, kv_rank), lambda i: (i, 1)),
            pl.BlockSpec((tm, LANES), lambda i: (i, 0)),
            pl.BlockSpec((tm, LANES), pos_block),
            pl.BlockSpec((tm, LANES), pos_block),
            pl.BlockSpec((1, q_rank), lambda i: (0, 0)),
            pl.BlockSpec((1, kv_rank), lambda i: (0, 0)),
            pl.BlockSpec(wq.shape, lambda i: (0, 0)),
            pl.BlockSpec(wk.shape, lambda i: (0, 0)),
            pl.BlockSpec(wv.shape, lambda i: (0, 0)),
        ],
        out_specs=[pl.BlockSpec((heads, 1, HEAD_LANES, tm), lambda i: (0, i, 0, 0)), out_spec,
                   pl.BlockSpec((heads, 1, VT_ROWS, tm), lambda i: (0, i // per_kv, 0, i % per_kv))],
        out_shape=[jax.ShapeDtypeStruct((heads, t // tm, HEAD_LANES, tm), BF16), out_sds,
                   jax.ShapeDtypeStruct((heads, t // tkv, VT_ROWS, tkv), BF16)],
        compiler_params=_cparams(1),
        name="qkv_up",
    )(pm, pm, ps, cs_tab, sn_tab, qnw, kvnw, wq, wk, wv)


def _attn_kernel(q_ref, k_ref, vt_ref, zb_ref, o_ref, *scratch, n_tiles, **kw):
    per_set = len(scratch) // min(n_tiles, 2)
    for tile in range(n_tiles):
        first = (tile % 2) * per_set
        _attn_tile(q_ref, k_ref, vt_ref, zb_ref, o_ref, *scratch[first:first + per_set],
                   tile=tile, n_tiles=n_tiles, **kw)


def _attn_tile(q_ref, k_ref, vt_ref, zb_ref, o_ref, s0_ref, s1_ref, x0_ref, x1_ref, p0_ref, p1_ref,
               a0_ref, a1_ref, m_ref, acc_ref, *, tile, n_tiles, tk, nk, cb_pv):
    slab = q_ref.shape[3]
    n_slab = q_ref.shape[1] // n_tiles
    tq = n_slab * slab
    per_slab = slab // LANES

    def scores(j, s_ref, x_ref):
        start = pl.multiple_of(j * tk, tk)
        kt = k_ref[0, pl.ds(start, tk), :]
        for b in range(n_slab):
            st = jnp.dot(kt, q_ref[0, tile * n_slab + b], preferred_element_type=F32)
            for c in range(per_slab):
                s_ref[b * per_slab + c] = st[:, c * LANES:(c + 1) * LANES]
            x_ref[:, b * slab:(b + 1) * slab] = jnp.broadcast_to(
                jnp.max(st, axis=0, keepdims=True), (x_ref.shape[0], slab))

    def softmax(s_ref, x_ref, p_ref, a_ref):
        for c in range(tq // LANES):
            cols = slice(c * LANES, (c + 1) * LANES)
            m_prev = m_ref[:, cols]
            m_new = jnp.maximum(m_prev, x_ref[:, cols])
            p_ref[c] = jnp.exp2(s_ref[c] - m_new[0:1, :]).astype(BF16)
            a_ref[:, cols] = jnp.exp2(m_prev - m_new)
            m_ref[:, cols] = m_new

    def pv(j, p_ref, a_ref):
        vt = vt_ref[0, j]
        per = cb_pv // LANES
        for c in range(tq // cb_pv):
            cols = slice(c * cb_pv, (c + 1) * cb_pv)
            pt = jnp.concatenate([p_ref[c * per + i] for i in range(per)], axis=1)
            upd = jnp.dot(vt, pt, preferred_element_type=F32)
            acc_ref[c] = acc_ref[c] * a_ref[0:1, cols] + upd

    m_ref[...] = jnp.full(m_ref.shape, -jnp.inf, F32)
    acc_ref[...] = jnp.zeros(acc_ref.shape, F32)
    scores(0, s0_ref, x0_ref)
    scores(1, s1_ref, x1_ref)
    softmax(s0_ref, x0_ref, p0_ref, a0_ref)

    def body(jj, carry):
        j = 2 * jj
        scores(j + 2, s0_ref, x0_ref)
        softmax(s1_ref, x1_ref, p1_ref, a1_ref)
        pv(j, p0_ref, a0_ref)
        scores(j + 3, s1_ref, x1_ref)
        softmax(s0_ref, x0_ref, p0_ref, a0_ref)
        pv(j + 1, p1_ref, a1_ref)
        return carry

    lax.fori_loop(0, nk // 2 - 1, body, 0)
    softmax(s1_ref, x1_ref, p1_ref, a1_ref)
    pv(nk - 2, p0_ref, a0_ref)
    pv(nk - 1, p1_ref, a1_ref)
    for c in range(tq // cb_pv):
        rows = slice(tile * tq + c * cb_pv, tile * tq + (c + 1) * cb_pv)
        o = (acc_ref[c, :MLA_V_DIM, :] / acc_ref[c, MLA_V_DIM:MLA_V_DIM + 1, :]).T
        o_ref[rows, :] = (o * _silu(zb_ref[rows, :].astype(F32))).astype(BF16)


def _with_carried_outputs(kern, n_in, prev):
    if prev is None:
        return kern, [], {}, ()

    def body(*refs):
        return kern(*refs[:n_in], *refs[n_in + len(prev):])

    specs = [pl.BlockSpec(memory_space=pl.ANY)] * len(prev)
    return body, specs, {n_in + k: k for k in range(len(prev))}, tuple(prev)


def _attention(qtf, kf, vtf, pm, prev, *, t_total, row0, n_seq, seq, zb_col0):
    heads, _, _, slab = qtf.shape
    tk = vtf.shape[3]
    tq = min(ATTN_TQ, seq)
    cb_pv = min(ATTN_CB_PV, tq)
    assert tq % slab == 0 and slab % LANES == 0
    nk = seq // tk
    assert nk % 2 == 0
    scratch = [
        pltpu.VMEM((tq // LANES, tk, LANES), F32),
        pltpu.VMEM((tq // LANES, tk, LANES), F32),
        pltpu.VMEM((8, tq), F32),
        pltpu.VMEM((8, tq), F32),
        pltpu.VMEM((tq // LANES, tk, LANES), BF16),
        pltpu.VMEM((tq // LANES, tk, LANES), BF16),
        pltpu.VMEM((8, tq), F32),
        pltpu.VMEM((8, tq), F32),
        pltpu.VMEM((8, tq), F32),
        pltpu.VMEM((tq // cb_pv, VT_ROWS, cb_pv), F32),
    ]
    scratch_bytes = 2 * tq * tk * (4 + 2) + 5 * 8 * tq * 4 + VT_ROWS * tq * 4
    kv_bytes = 2 * 2 * (seq * HEAD_LANES + nk * VT_ROWS * tk)
    io_bytes = 2 * 2 * tq * (HEAD_LANES + 2 * LANES)
    n_tiles = 2 if (seq // tq) % 2 == 0 and 2 * (scratch_bytes + io_bytes) + kv_bytes <= ATTN_VMEM_BUDGET else 1
    tqs = tq * n_tiles
    nq = seq // tqs
    qb0 = row0 // tqs
    sb0 = row0 // seq
    zc0 = zb_col0 // LANES
    assert row0 % tqs == 0
    kern = functools.partial(_attn_kernel, n_tiles=n_tiles, tk=tk, nk=nk, cb_pv=cb_pv)
    kern, prev_specs, aliases, prev = _with_carried_outputs(kern, 4, prev)
    return pl.pallas_call(
        kern,
        grid=(n_seq, heads, nq),
        in_specs=[
            pl.BlockSpec((1, tqs // slab, HEAD_LANES, slab), lambda n, h, i: (h, qb0 + n * nq + i, 0, 0)),
            pl.BlockSpec((1, seq, HEAD_LANES), lambda n, h, i: (h, sb0 + n, 0)),
            pl.BlockSpec((1, nk, VT_ROWS, tk), lambda n, h, i: (h, sb0 + n, 0, 0)),
            pl.BlockSpec((tqs, LANES), lambda n, h, i: (qb0 + n * nq + i, zc0 + h)),
        ] + prev_specs,
        out_specs=pl.BlockSpec((tqs, LANES), lambda n, h, i: (qb0 + n * nq + i, h)),
        out_shape=jax.ShapeDtypeStruct((t_total, heads * LANES), BF16),
        input_output_aliases=aliases,
        scratch_shapes=scratch * n_tiles,
        compiler_params=_cparams(3),
        name="mla_attention",
    )(qtf, kf, vtf, pm, *prev)


def _mlstm_kernel(qf_ref, kf_ref, vf_ref, gf_ref, qb_ref, kb_ref, vb_ref, gb_ref, bif_ref,
                  hf_ref, hb_ref, st_ref, m_ref, dm_ref, wi_ref, fl_ref, wr_ref, dc_ref, *, heads, log2_k_scale):
    L = ML_CHUNK
    dh = ML_HEAD_DIM

    @pl.when(pl.program_id(1) == 0)
    def _():
        st_ref[...] = jnp.zeros_like(st_ref)
        m_ref[...] = jnp.zeros_like(m_ref)

    assert L == LANES
    row = lax.broadcasted_iota(jnp.int32, (L, L), 0)
    col = lax.broadcasted_iota(jnp.int32, (L, L), 1)
    ones_blk = jnp.ones((L, LANES), BF16)

    def rep(column):
        return jnp.broadcast_to(column, (L, LANES))

    dirs = ((qf_ref, kf_ref, vf_ref, gf_ref, hf_ref, col <= row, L - 1),
            (qb_ref, kb_ref, vb_ref, gb_ref, hb_ref, col >= row, 0))
    for d, (q_ref, k_ref, v_ref, g_ref, h_ref, mask, last) in enumerate(dirs):
        g = g_ref[...] + bif_ref[...]
        lf = _log_sigmoid(g) * LOG2_E
        g = g * LOG2_E
        b_all = jnp.dot(mask.astype(F32), lf, preferred_element_type=F32,
                        precision=lax.Precision.HIGHEST)
        g_t = g.T
        b_t = b_all.T
        for h in range(heads):
            ci = 2 * heads * d + h
            cf = ci + heads
            idx = d * heads + h
            b_r = rep(b_all[:, cf:cf + 1])
            i_r = rep(g[:, ci:ci + 1])
            arow = g_t[ci:ci + 1, :] - b_t[cf:cf + 1, :]
            m_prev = m_ref[idx]
            logd = jnp.where(mask, b_r + arow, -jnp.inf)
            inter = b_r + m_prev
            m_t = jnp.maximum(inter, rep(jnp.max(logd, axis=1, keepdims=True)))
            dm_ref[idx] = jnp.exp2(logd - m_t)
            wi_ref[idx] = jnp.exp2(inter - m_t)
            fl_ref[idx] = jnp.exp2(-log2_k_scale - m_t)
            m_new = m_t[last:last + 1, :]
            b_last = b_r[last:last + 1, :]
            wr_ref[idx] = jnp.exp2(b_last - b_r + i_r - m_new)
            dc_ref[idx] = jnp.exp2(b_last + m_prev - m_new)
            m_ref[idx] = m_new

    for d, (q_ref, k_ref, v_ref, g_ref, h_ref, mask, last) in enumerate(dirs):
        for h in range(heads):
            idx = d * heads + h
            q = q_ref[:, h * dh:(h + 1) * dh]
            k = k_ref[:, h * dh:(h + 1) * dh]
            v = v_ref[:, h * dh:(h + 1) * dh]
            s = lax.dot_general(q, k, (((1,), (1,)), ((), ())), preferred_element_type=F32)
            s = (s * dm_ref[idx]).astype(BF16)
            st_prev = st_ref[idx]
            intra = jnp.dot(s, jnp.concatenate([v, ones_blk], axis=1), preferred_element_type=F32)
            inter_p = jnp.dot(q, st_prev.astype(BF16), preferred_element_type=F32)
            w_inter = wi_ref[idx]
            num = intra[:, :dh] + w_inter * inter_p[:, :dh]
            den = intra[:, dh:] + w_inter * inter_p[:, dh:]
            hout = num / jnp.maximum(jnp.abs(den), fl_ref[idx])
            h_ref[:, h * dh:(h + 1) * dh] = hout.astype(h_ref.dtype)
            w_r = wr_ref[idx]
            decay = dc_ref[idx]
            wv = jnp.concatenate([v.astype(F32) * w_r, w_r], axis=1).astype(BF16)
            upd = lax.dot_general(k, wv, (((0,), (0,)), ((), ())), preferred_element_type=F32)
            st_ref[idx] = jnp.concatenate([decay, decay], axis=1) * st_prev + upd


def _mlstm(pm, ps, bif, prev, *, t_total, row0, n_seq, seq, q_col0):
    heads = ML_HEADS
    width = heads * ML_HEAD_DIM
    L = ML_CHUNK
    nc = seq // L
    rb0 = row0 // L
    qc = q_col0 // width
    fwd = lambda n, c: rb0 + n * nc + c
    bwd = lambda n, c: rb0 + n * nc + (nc - 1 - c)

    def col_spec(chunk_of, cb):
        return pl.BlockSpec((L, width), lambda n, c: (chunk_of(n, c), cb))

    gate_f = pl.BlockSpec((L, LANES), lambda n, c: (fwd(n, c), 1))
    gate_b = pl.BlockSpec((L, LANES), lambda n, c: (bwd(n, c), 1))
    out_sds = jax.ShapeDtypeStruct((t_total, width), BF16)
    kern = functools.partial(_mlstm_kernel, heads=heads, log2_k_scale=-0.5 * math.log2(ML_HEAD_DIM))
    kern, prev_specs, aliases, prev = _with_carried_outputs(kern, 9, prev)
    return pl.pallas_call(
        kern,
        grid=(n_seq, nc),
        in_specs=[
            col_spec(fwd, qc), col_spec(fwd, qc + 1), col_spec(fwd, qc + 2), gate_f,
            col_spec(bwd, qc), col_spec(bwd, qc + 1), col_spec(bwd, qc + 2), gate_b,
            pl.BlockSpec((1, LANES), lambda n, c: (0, 0)),
        ] + prev_specs,
        out_specs=[
            pl.BlockSpec((L, width), lambda n, c: (fwd(n, c), 0)),
            pl.BlockSpec((L, width), lambda n, c: (bwd(n, c), 0)),
        ],
        out_shape=[out_sds, out_sds],
        input_output_aliases=aliases,
        scratch_shapes=[
            pltpu.VMEM((2 * heads, ML_HEAD_DIM, HEAD_LANES), F32),
            pltpu.VMEM((2 * heads, 1, LANES), F32),
            pltpu.VMEM((2 * heads, L, L), F32),
            pltpu.VMEM((2 * heads, L, LANES), F32),
            pltpu.VMEM((2 * heads, L, LANES), F32),
            pltpu.VMEM((2 * heads, L, LANES), F32),
            pltpu.VMEM((2 * heads, 1, LANES), F32),
        ],
        compiler_params=_cparams(2),
        name="mlstm_bidir",
    )(pm, pm, pm, ps, pm, pm, pm, ps, bif, *prev)


def _tail_kernel(x_ref, attn_ref, hf_ref, hb_ref, o_ref, za_ref, ga_ref, gb_ref, gate_ref,
                 mlw_ref, wpa_ref, wpb_ref, wo_ref, lnw_ref, lnb_ref, y_ref, *, heads, alpha):
    dh = ML_HEAD_DIM
    tm = x_ref.shape[0]
    sub = min(TAIL_SUB, tm)
    for r0 in range(0, tm, sub):
        rows = slice(r0, r0 + sub)
        hs = hf_ref[rows, :].astype(F32) + hb_ref[rows, :].astype(F32)
        hn = jnp.concatenate(
            [_layer_norm_rows(hs[:, h * dh:(h + 1) * dh]) for h in range(heads)], axis=1)
        a_in = (hn * mlw_ref[...] * _sigmoid(o_ref[rows, :].astype(F32))
                * _silu(za_ref[rows, :].astype(F32)))
        y_a = jnp.dot(a_in.astype(BF16), wpa_ref[...], preferred_element_type=F32)
        y_b = jnp.dot(attn_ref[rows, :], wpb_ref[...], preferred_element_type=F32)
        merged = (_sigmoid(ga_ref[rows, :].astype(F32)) * y_a
                  + _sigmoid(gb_ref[rows, :].astype(F32)) * y_b)
        out = jnp.dot(merged.astype(BF16), wo_ref[...], preferred_element_type=F32)
        r = alpha * x_ref[rows, :] + gate_ref[0] * out
        y_ref[rows, :] = _layer_norm_rows(r) * lnw_ref[...] + lnb_ref[...]


def _tail(x, attn, hf, hb, pm, gate, mlw, wpa, wpb, wo, lnw, lnb, prev, *, n_rows, x_row0, row0, out_rows,
          out_row0, seg, alpha, o_col0, za_col0, g_col0):
    d = x.shape[1]
    wa = hf.shape[1]
    tm = min(TAIL_TM, seg)
    xb0, gb0, ob0 = x_row0 // tm, row0 // tm, out_row0 // tm
    const = lambda shape: pl.BlockSpec(shape, lambda i: (0,) * len(shape), pipeline_mode=pl.Buffered(1))
    kern = functools.partial(_tail_kernel, heads=ML_HEADS, alpha=alpha)
    kern, prev_specs, aliases, prev = _with_carried_outputs(kern, 15, prev)
    return pl.pallas_call(
        kern,
        grid=(n_rows // tm,),
        in_specs=[
            pl.BlockSpec((tm, d), lambda i: (xb0 + i, 0)),
            pl.BlockSpec((tm, attn.shape[1]), lambda i: (gb0 + i, 0)),
            pl.BlockSpec((tm, wa), lambda i: (gb0 + i, 0)),
            pl.BlockSpec((tm, wa), lambda i: (gb0 + i, 0)),
            pl.BlockSpec((tm, wa), lambda i: (gb0 + i, o_col0 // wa)),
            pl.BlockSpec((tm, wa), lambda i: (gb0 + i, za_col0 // wa)),
            pl.BlockSpec((tm, d), lambda i: (gb0 + i, g_col0 // d)),
            pl.BlockSpec((tm, d), lambda i: (gb0 + i, g_col0 // d + 1)),
            pl.BlockSpec((1, 1, d), lambda i: ((row0 + i * tm) // seg, 0, 0)),
            const((1, wa)),
            const(wpa.shape),
            const(wpb.shape),
            const(wo.shape),
            const((1, d)),
            const((1, d)),
        ] + prev_specs,
        out_specs=pl.BlockSpec((tm, d), lambda i: (ob0 + i, 0)),
        out_shape=jax.ShapeDtypeStruct((out_rows, d), F32),
        input_output_aliases=aliases,
        compiler_params=_cparams(1),
        name="tail",
    )(x, attn, hf, hb, pm, pm, pm, pm, gate, mlw, wpa, wpb, wo, lnw, lnb, *prev)


def _rot_half(w):
    half = w.shape[-1] // 2
    return jnp.concatenate([-w[..., half:], w[..., :half]], axis=-1)


def _prep_weights(w_in, b_if, w_q_b, w_kv_b, w_proj_a, w_proj_b, w_out, d_model, q_rank, kv_rank):
    depth = w_in.shape[0]
    w_in, w_q_b, w_kv_b = w_in.astype(BF16), w_q_b.astype(BF16), w_kv_b.astype(BF16)
    wa = ML_HEADS * ML_HEAD_DIM
    wb = MLA_HEADS * MLA_V_DIM
    sizes = [q_rank, kv_rank, MLA_ROPE_DIM, wb, wa, wa, wa, wa, wa, 4 * ML_HEADS, 2 * d_model]
    offs = [0]
    for s in sizes:
        offs.append(offs[-1] + s)
    part = lambda i: w_in[:, :, offs[i]:offs[i + 1]]
    (q_lat, kv_lat, k_rope, z_b, ml_q, ml_k, ml_v, ml_o, z_a, ml_g, merge_g) = [part(i) for i in range(11)]
    main_parts = [q_lat, kv_lat, z_b, ml_q, ml_k, ml_v, ml_o, z_a, merge_g]
    w_main = jnp.concatenate(main_parts, axis=-1)
    names = ["q_lat", "kv_lat", "z_b", "ml_q", "ml_k", "ml_v", "ml_o", "z_a", "merge_g"]
    cols, o = {}, 0
    for nme, p in zip(names, main_parts):
        cols[nme] = o
        o += p.shape[-1]
    pad = jnp.zeros(w_in.shape[:2] + (LANES - 4 * ML_HEADS,), w_in.dtype)
    w_small = jnp.concatenate([k_rope, _rot_half(k_rope), ml_g, pad], axis=-1)

    wq = w_q_b.reshape(depth, q_rank, MLA_HEADS, MLA_QK_DIM)
    wq_r = wq[..., MLA_NOPE_DIM:]
    wq = jnp.concatenate([wq[..., :MLA_NOPE_DIM], wq_r, _rot_half(wq_r)], axis=-1)
    wq = wq.reshape(depth, q_rank, MLA_HEADS * HEAD_LANES)
    wkv = w_kv_b.reshape(depth, kv_rank, MLA_HEADS, MLA_NOPE_DIM + MLA_V_DIM)
    wk = wkv[..., :MLA_NOPE_DIM].reshape(depth, kv_rank, MLA_HEADS * MLA_NOPE_DIM)
    wv = wkv[..., MLA_NOPE_DIM:].reshape(depth, kv_rank, MLA_HEADS * MLA_V_DIM)
    bif = jnp.pad(b_if.reshape(depth, 1, 4 * ML_HEADS), ((0, 0), (0, 0), (0, LANES - 4 * ML_HEADS)))
    return dict(w_main=w_main, w_small=w_small, cols=cols, wq=wq, wk=wk, wv=wv, bif=bif,
                wpa=w_proj_a.astype(BF16), wpb=w_proj_b.astype(BF16), wo=w_out.astype(BF16))


def _rope_lane_tables(seq_len):
    inv_freq = ROPE_THETA ** (-jnp.arange(0, MLA_ROPE_DIM, 2, dtype=F32) / MLA_ROPE_DIM)
    ang = jnp.arange(seq_len, dtype=F32)[:, None] * inv_freq[None, :]
    zeros = jnp.zeros((seq_len, LANES - MLA_ROPE_DIM), F32)
    cos, sin = jnp.cos(ang), jnp.sin(ang)
    return (jnp.concatenate([cos, cos, zeros], axis=1), jnp.concatenate([sin, sin, zeros], axis=1))


def kernel(x_prompt, x_sample, c_prompt, c_sample, w_ada, b_ada, w_in, b_if, q_norm_w, kv_norm_w,
           w_q_b, w_kv_b, ml_norm_w, w_proj_a, w_proj_b, w_out, ln_w, ln_b):
    batch, seq, d = x_prompt.shape
    dec_batch, dec_seq, _ = x_sample.shape
    depth = w_ada.shape[0]
    q_rank = q_norm_w.shape[1]
    kv_rank = kv_norm_w.shape[1]
    alpha = (2 * depth) ** 0.25
    assert seq % dec_seq == 0 and dec_seq % ML_CHUNK == 0
    seg = dec_seq
    t_prompt = batch * seq
    t = t_prompt + dec_batch * dec_seq

    x_groups = (x_prompt.reshape(t_prompt, d), x_sample.reshape(dec_batch * dec_seq, d))
    c_all = jnp.concatenate([c_prompt, c_sample], axis=0)
    n_cond = c_all.shape[0]
    rows = -(-n_cond // 16) * 16
    c_pad = jnp.pad(c_all, ((0, rows - n_cond), (0, 0)))
    seg_cond = jnp.concatenate([jnp.repeat(jnp.arange(batch), seq // seg),
                                batch + jnp.arange(dec_batch)])

    mod = _adaln_mod(c_pad, w_ada, b_ada)
    mod = mod[:, seg_cond, :].reshape(depth, t // seg, 1, 3, d)
    shift, scale, gate = mod[..., 0, :], mod[..., 1, :], mod[..., 2, :]

    w = _prep_weights(w_in, b_if, w_q_b, w_kv_b, w_proj_a, w_proj_b, w_out, d, q_rank, kv_rank)
    cols = w["cols"]
    cs_tab, sn_tab = _rope_lane_tables(max(seq, dec_seq))
    groups = ((0, batch, seq), (t_prompt, dec_batch, dec_seq))

    x = None
    for l in range(depth):
        if l == 0:
            pmps = None
            for (row0, _, _), xg in zip(groups, x_groups):
                pmps = _inproj(xg, shift[l], scale[l], w["w_main"][l], w["w_small"][l], pmps,
                               seg=seg, t_total=t, row0=row0)
            pm, ps = pmps
        else:
            pm, ps = _inproj(x, shift[l], scale[l], w["w_main"][l], w["w_small"][l], None,
                             seg=seg, t_total=t, row0=0)
        qf, kf, vf = _qkv(pm, ps, cs_tab, sn_tab, q_norm_w[l][None], kv_norm_w[l][None],
                          w["wq"][l], w["wk"][l], w["wv"][l],
                          t_prompt=t_prompt, dec_seq=dec_seq, q_rank=q_rank, kv_rank=kv_rank,
                          tkv=min(ATTN_TK, dec_seq))
        attn, hfb = None, None
        for row0, n_seq, s_len in groups:
            attn = [_attention(qf, kf, vf, pm, attn, t_total=t, row0=row0, n_seq=n_seq, seq=s_len,
                               zb_col0=cols["z_b"])]
            hfb = _mlstm(pm, ps, w["bif"][l], hfb, t_total=t, row0=row0, n_seq=n_seq, seq=s_len,
                         q_col0=cols["ml_q"])
        tail = functools.partial(
            _tail, attn=attn[0], hf=hfb[0], hb=hfb[1], pm=pm, gate=gate[l], mlw=ml_norm_w[l][None],
            wpa=w["wpa"][l], wpb=w["wpb"][l], wo=w["wo"][l], lnw=ln_w[l][None], lnb=ln_b[l][None],
            seg=seg, alpha=alpha, o_col0=cols["ml_o"], za_col0=cols["z_a"], g_col0=cols["merge_g"])
        first, last = l == 0, l == depth - 1
        if not (first or last):
            x = tail(x=x, prev=None, n_rows=t, x_row0=0, row0=0, out_rows=t, out_row0=0)
            continue
        outs, carried = [], None
        for (row0, n_seq, s_len), xg in zip(groups, x_groups):
            n_rows = n_seq * s_len
            y = tail(x=xg if first else x, prev=None if last else carried, n_rows=n_rows,
                     x_row0=0 if first else row0, row0=row0,
                     out_rows=n_rows if last else t, out_row0=0 if last else row0)
            outs.append(y)
            carried = [y]
        x = None if last else outs[-1]

    y_prompt, y_sample = outs
    return (y_prompt.reshape(batch, seq, d), y_sample.reshape(dec_batch, dec_seq, d))
```
